```python
import math
import jax, jax.numpy as jnp
from jax import lax
import numpy as np

D_MODEL = 1024
BATCH = 32
SEQ = 256
DEPTH = 4
DEC_BATCH = 2
DEC_SEQ = 1024
PAST_LEN = 256

GRID_W = 64
N_EVEN = (DEPTH + 1) // 2
N_ODD = DEPTH // 2
LRU_WIDTH = D_MODEL // 2
LRU_HEADS = 8
LRU_BLOCK = LRU_WIDTH // LRU_HEADS
LRU_C = 8.0
LRU_CONV = 4
SC_WIDTH = D_MODEL // 2
SC_CONV = 3
IN_WIDTH = 2 * LRU_WIDTH + 3 * SC_WIDTH
N_HEADS = 16
HEAD_DIM = D_MODEL // N_HEADS
WIN_R = 8
WIN_C = 16
Q_BLOCK = 128
D_FF = 2816
N_EXPERTS = 8
TOP_K = 2
EPS = 1e-6
NEG_INF = -1e30

kernel_name = "hybrid_diffusion_rglru_shortconv_natten_moe_step"


def rmsnorm(x, g):
    xf = x.astype(jnp.float32)
    y = xf * lax.rsqrt(jnp.mean(xf * xf, axis=-1, keepdims=True) + EPS)
    return (y * g.astype(jnp.float32)).astype(x.dtype)


def adaln(cond, w_mod_l, b_mod_l):
    mod = jax.nn.silu(cond) @ w_mod_l + b_mod_l
    return jnp.split(mod[:, None, :], 6, axis=-1)


def modulate(x, shift, scale):
    return x * (1 + scale) + shift


def depthwise_conv(x, w, b, pad_left):
    k_w = w.shape[0]
    seq = x.shape[1]
    xp = jnp.pad(x, ((0, 0), (pad_left, k_w - 1 - pad_left), (0, 0)))
    y = b + xp[:, 0:seq] * w[0]
    for k in range(1, k_w):
        y = y + xp[:, k:k + seq] * w[k]
    return y


def block_diag(x, w, b):
    bsz, seq = x.shape[0], x.shape[1]
    xh = x.reshape(bsz, seq, LRU_HEADS, LRU_BLOCK)
    return jnp.einsum('blhi,hij->blhj', xh, w).reshape(bsz, seq, LRU_WIDTH) + b


def linear_scan(a, b, h0, reverse):
    edge = -1 if reverse else 0
    b = b.at[:, edge].add(a[:, edge] * h0)

    def combine(left, right):
        a_l, b_l = left
        a_r, b_r = right
        return a_l * a_r, a_r * b_l + b_r

    _, h = lax.associative_scan(combine, (a, b), axis=1, reverse=reverse)
    return h


def rg_lru(xc, wa, ba, wx, bx, lam, h0, reverse):
    xf = xc.astype(jnp.float32)
    r = jax.nn.sigmoid(block_diag(xc, wa, ba)).astype(jnp.float32)
    i = jax.nn.sigmoid(block_diag(xc, wx, bx)).astype(jnp.float32)
    log_a = -LRU_C * r * jax.nn.softplus(-lam.astype(jnp.float32))
    a = jnp.exp(log_a)
    b = jnp.sqrt(-jnp.expm1(2.0 * log_a)) * (i * xf)
    return linear_scan(a, b, h0.astype(jnp.float32), reverse)


def recurrent_conv_mixer(hn, p, i, h0):
    r_w, s_w = LRU_WIDTH, SC_WIDTH
    proj = hn @ p['w_in'][i]
    x_lru, g_lru, b_sc, c_sc, v_sc = jnp.split(proj, [r_w, 2 * r_w, 2 * r_w + s_w, 2 * r_w + 2 * s_w], axis=-1)
    xc = depthwise_conv(x_lru, p['lru_conv_w'][i], p['lru_conv_b'][i], LRU_CONV // 2)
    h_f = rg_lru(xc, p['lru_wa'][i, 0], p['lru_ba'][i, 0], p['lru_wx'][i, 0], p['lru_bx'][i, 0],
                 p['lru_lam'][i, 0], h0[:, 0], False)
    h_b = rg_lru(xc, p['lru_wa'][i, 1], p['lru_ba'][i, 1], p['lru_wx'][i, 1], p['lru_bx'][i, 1],
                 p['lru_lam'][i, 1], h0[:, 1], True)
    y_lru = (h_f + h_b).astype(hn.dtype) * jax.nn.gelu(g_lru)
    y_sc = b_sc * depthwise_conv(c_sc * v_sc, p['sc_conv_w'][i], p['sc_conv_b'][i], SC_CONV // 2)
    out = jnp.concatenate([y_lru, y_sc], axis=-1) @ p['w_out'][i]
    return out, h_f, h_b


def qkv_heads(hn, w_qkv_l):
    bsz, seq = hn.shape[0], hn.shape[1]
    qkv = (hn @ w_qkv_l).reshape(bsz, seq, 3, N_HEADS, HEAD_DIM).transpose(2, 0, 3, 1, 4)
    return qkv[0], qkv[1], qkv[2]


def merge_heads(o):
    bsz, _, seq, _ = o.shape
    return o.transpose(0, 2, 1, 3).reshape(bsz, seq, D_MODEL)


def context_attention(q, k, v):
    bsz, heads, seq, dh = q.shape
    n_blk = seq // Q_BLOCK
    qb = jnp.moveaxis((q * (dh ** -0.5)).reshape(bsz, heads, n_blk, Q_BLOCK, dh), 2, 0)

    def one_block(q_blk):
        s = jnp.einsum('bhqd,bhkd->bhqk', q_blk, k).astype(jnp.float32)
        pr = jax.nn.softmax(s, axis=-1).astype(v.dtype)
        return jnp.einsum('bhqk,bhkd->bhqd', pr, v)

    o = lax.map(one_block, qb)
    return jnp.moveaxis(o, 0, 2).reshape(bsz, heads, seq, dh)


def neighbourhood_attention(q, k, v, k_ctx, v_ctx, rpb_l):
    bsz, heads, n_tok, dh = q.shape
    rows = n_tok // GRID_W
    kr = min(WIN_R, rows)
    kc = WIN_C
    r_idx = jnp.arange(rows)
    col = jnp.arange(GRID_W)
    row_start = jnp.clip(r_idx - kr // 2, 0, rows - kr)
    key_rows = row_start[:, None] + jnp.arange(kr)
    col_start = jnp.clip(col - kc // 2, 0, GRID_W - kc)
    col_mask = (col[None, :] >= col_start[:, None]) & (col[None, :] < col_start[:, None] + kc)
    dr = key_rows - r_idx[:, None] + (WIN_R - 1)
    dc = jnp.clip(col[None, :] - col[:, None], -(kc - 1), kc - 1) + (WIN_C - 1)
    bias = rpb_l[:, dr[:, None, :, None], dc[None, :, None, :]]
    qg = (q * (dh ** -0.5)).reshape(bsz, heads, rows, GRID_W, dh)
    kg = k.reshape(bsz, heads, rows, GRID_W, dh)[:, :, key_rows]
    vg = v.reshape(bsz, heads, rows, GRID_W, dh)[:, :, key_rows]
    s_win = jnp.einsum('bhrqd,bhrkwd->bhrqkw', qg, kg).astype(jnp.float32) + bias.astype(jnp.float32)
    s_win = jnp.where(col_mask[:, None, :], s_win, NEG_INF).reshape(bsz, heads, rows, GRID_W, kr * GRID_W)
    s_ctx = jnp.einsum('bhrqd,bhld->bhrql', qg, k_ctx).astype(jnp.float32)
    pr = jax.nn.softmax(jnp.concatenate([s_win, s_ctx], axis=-1), axis=-1).astype(v.dtype)
    p_win = pr[..., :kr * GRID_W].reshape(bsz, heads, rows, GRID_W, kr, GRID_W)
    p_ctx = pr[..., kr * GRID_W:]
    o = (jnp.einsum('bhrqkw,bhrkwd->bhrqd', p_win, vg)
         + jnp.einsum('bhrql,bhld->bhrqd', p_ctx, v_ctx))
    return o.reshape(bsz, heads, n_tok, dh)


def swiglu(x, wg, wu, wd):
    return (jax.nn.silu(x @ wg) * (x @ wu)) @ wd


def moe(x, w_router_l, wg, wu, wd):
    logits = (x @ w_router_l).astype(jnp.float32)
    top_v, top_i = lax.top_k(logits, TOP_K)
    gates = jax.nn.softmax(top_v, axis=-1)
    combine = jnp.sum(jax.nn.one_hot(top_i, N_EXPERTS, dtype=jnp.float32) * gates[..., None], axis=-2)
    combine = combine.astype(x.dtype)
    out = jnp.zeros_like(x)
    for e in range(N_EXPERTS):
        out = out + combine[..., e:e + 1] * swiglu(x, wg[e], wu[e], wd[e])
    return out


def channel_mixer(hn, p, l):
    idx = l // 2
    if l % 2 == 0:
        return swiglu(hn, p['ffn_w_gate'][idx], p['ffn_w_up'][idx], p['ffn_w_down'][idx])
    return moe(hn, p['w_router'][idx], p['moe_w_gate'][idx], p['moe_w_up'][idx], p['moe_w_down'][idx])


def context_pass(x, c_ctx, p):
    cond = c_ctx[None, :]
    lru_states, ks, vs = [], [], []
    for l in range(DEPTH):
        idx = l // 2
        sh1, sc1, g1, sh2, sc2, g2 = adaln(cond, p['w_mod'][l], p['b_mod'][l])
        hn = modulate(rmsnorm(x, p['norm1_g'][l]), sh1, sc1)
        if l % 2 == 0:
            h0 = jnp.zeros((x.shape[0], 2, LRU_WIDTH), jnp.float32)
            mix, h_f, h_b = recurrent_conv_mixer(hn, p, idx, h0)
            lru_states.append(jnp.stack([h_f[:, -1], h_b[:, 0]], axis=1).astype(x.dtype))
        else:
            q, k, v = qkv_heads(hn, p['w_qkv'][idx])
            mix = merge_heads(context_attention(q, k, v)) @ p['w_o'][idx]
            ks.append(k)
            vs.append(v)
        x = x + g1 * mix
        hn = modulate(rmsnorm(x, p['norm2_g'][l]), sh2, sc2)
        x = x + g2 * channel_mixer(hn, p, l)
    y = rmsnorm(x, p['final_g'])
    return y, jnp.stack(lru_states, axis=1), jnp.stack(ks, axis=1), jnp.stack(vs, axis=1)


def denoise_pass(x, c, state_lru, cache_k, cache_v, p):
    for l in range(DEPTH):
        idx = l // 2
        sh1, sc1, g1, sh2, sc2, g2 = adaln(c, p['w_mod'][l], p['b_mod'][l])
        hn = modulate(rmsnorm(x, p['norm1_g'][l]), sh1, sc1)
        if l % 2 == 0:
            mix, _, _ = recurrent_conv_mixer(hn, p, idx, state_lru[:, idx])
        else:
            q, k, v = qkv_heads(hn, p['w_qkv'][idx])
            o = neighbourhood_attention(q, k, v, cache_k[:, idx], cache_v[:, idx], p['rpb'][idx])
            mix = merge_heads(o) @ p['w_o'][idx]
        x = x + g1 * mix
        hn = modulate(rmsnorm(x, p['norm2_g'][l]), sh2, sc2)
        x = x + g2 * channel_mixer(hn, p, l)
    return rmsnorm(x, p['final_g'])


def setup_inputs(seed: int = 0) -> dict:
    key = jax.random.key(seed)
    k = jax.random.split(key, 33)

    def nrm(i, shape, scale=1.0):
        return jax.random.normal(k[i], shape, jnp.float32) * scale

    d, f = D_MODEL, D_FF
    a_c = jax.random.uniform(k[21], (N_EVEN, 2, LRU_WIDTH), jnp.float32, 0.9, 0.999)
    sig = a_c ** (1.0 / LRU_C)
    lru_lam = jnp.log(sig) - jnp.log1p(-sig)
    mix_w = LRU_WIDTH + SC_WIDTH
    return {
        'x_prompt': nrm(0, (BATCH, SEQ, d)),
        'x_sample': nrm(1, (DEC_BATCH, DEC_SEQ, d)),
        'state_lru': nrm(2, (DEC_BATCH, N_EVEN, 2, LRU_WIDTH), 0.5),
        'cache_k': nrm(3, (DEC_BATCH, N_ODD, N_HEADS, PAST_LEN, HEAD_DIM)),
        'cache_v': nrm(4, (DEC_BATCH, N_ODD, N_HEADS, PAST_LEN, HEAD_DIM)),
        'c': nrm(5, (DEC_BATCH, d)),
        'c_ctx': nrm(6, (d,)),
        'norm1_g': 1.0 + nrm(7, (DEPTH, d), 0.05),
        'norm2_g': 1.0 + nrm(8, (DEPTH, d), 0.05),
        'w_mod': nrm(9, (DEPTH, d, 6 * d), 0.5 * d ** -0.5),
        'b_mod': nrm(10, (DEPTH, 6 * d), 0.02),
        'w_in': nrm(11, (N_EVEN, d, IN_WIDTH), d ** -0.5),
        'lru_conv_w': nrm(12, (N_EVEN, LRU_CONV, LRU_WIDTH), LRU_CONV ** -0.5),
        'lru_conv_b': nrm(13, (N_EVEN, LRU_WIDTH), 0.02),
        'lru_wa': nrm(14, (N_EVEN, 2, LRU_HEADS, LRU_BLOCK, LRU_BLOCK), LRU_BLOCK ** -0.5),
        'lru_ba': nrm(15, (N_EVEN, 2, LRU_WIDTH), 0.02),
        'lru_wx': nrm(16, (N_EVEN, 2, LRU_HEADS, LRU_BLOCK, LRU_BLOCK), LRU_BLOCK ** -0.5),
        'lru_bx': nrm(17, (N_EVEN, 2, LRU_WIDTH), 0.02),
        'lru_lam': lru_lam,
        'sc_conv_w': nrm(18, (N_EVEN, SC_CONV, SC_WIDTH), SC_CONV ** -0.5),
        'sc_conv_b': nrm(19, (N_EVEN, SC_WIDTH), 0.02),
        'w_out': nrm(20, (N_EVEN, mix_w, d), mix_w ** -0.5),
        'ffn_w_gate': nrm(22, (N_EVEN, d, f), d ** -0.5),
        'ffn_w_up': nrm(23, (N_EVEN, d, f), d ** -0.5),
        'ffn_w_down': nrm(24, (N_EVEN, f, d), f ** -0.5),
        'w_qkv': nrm(25, (N_ODD, d, 3 * d), d ** -0.5),
        'w_o': nrm(26, (N_ODD, d, d), d ** -0.5),
        'rpb': nrm(27, (N_ODD, N_HEADS, 2 * WIN_R - 1, 2 * WIN_C - 1), 0.2),
        'w_router': nrm(28, (N_ODD, d, N_EXPERTS), d ** -0.5),
        'moe_w_gate': nrm(29, (N_ODD, N_EXPERTS, d, f), d ** -0.5),
        'moe_w_up': nrm(30, (N_ODD, N_EXPERTS, d, f), d ** -0.5),
        'moe_w_down': nrm(31, (N_ODD, N_EXPERTS, f, d), f ** -0.5),
        'final_g': 1.0 + nrm(32, (d,), 0.05),
    }


def reference(x_prompt, x_sample, state_lru, cache_k, cache_v, c, c_ctx, norm1_g, norm2_g, w_mod, b_mod,
              w_in, lru_conv_w, lru_conv_b, lru_wa, lru_ba, lru_wx, lru_bx, lru_lam, sc_conv_w, sc_conv_b,
              w_out, ffn_w_gate, ffn_w_up, ffn_w_down, w_qkv, w_o, rpb, w_router, moe_w_gate, moe_w_up,
              moe_w_down, final_g):
    p = {
        'norm1_g': norm1_g, 'norm2_g': norm2_g, 'w_mod': w_mod, 'b_mod': b_mod,
        'w_in': w_in, 'lru_conv_w': lru_conv_w, 'lru_conv_b': lru_conv_b,
        'lru_wa': lru_wa, 'lru_ba': lru_ba, 'lru_wx': lru_wx, 'lru_bx': lru_bx, 'lru_lam': lru_lam,
        'sc_conv_w': sc_conv_w, 'sc_conv_b': sc_conv_b, 'w_out': w_out,
        'ffn_w_gate': ffn_w_gate, 'ffn_w_up': ffn_w_up, 'ffn_w_down': ffn_w_down,
        'w_qkv': w_qkv, 'w_o': w_o, 'rpb': rpb,
        'w_router': w_router, 'moe_w_gate': moe_w_gate, 'moe_w_up': moe_w_up, 'moe_w_down': moe_w_down,
        'final_g': final_g,
    }
    y_prompt, new_state_lru, new_cache_k, new_cache_v = context_pass(x_prompt, c_ctx, p)
    y_sample = denoise_pass(x_sample, c, state_lru, cache_k, cache_v, p)
    return (y_prompt, y_sample, new_state_lru, new_cache_k, new_cache_v)
```

```python
import functools

import jax
import jax.numpy as jnp
from jax import lax
from jax.experimental import pallas as pl
from jax.experimental.pallas import tpu as pltpu

F32 = jnp.float32
BF16 = jnp.bfloat16

D = 1024
BATCH = 32
SEQ = 256
DEPTH = 4
DEC_BATCH = 2
DEC_SEQ = 1024
PAST_LEN = 256
GRID_W = 64
GRID_ROWS = DEC_SEQ // GRID_W
LRU_W = 512
LRU_HEADS = 8
LRU_BLOCK = LRU_W // LRU_HEADS
LRU_C = 8.0
SC_W = 512
IN_W = 2 * LRU_W + 3 * SC_W
N_HEADS = 16
HEAD_DIM = D // N_HEADS
WIN_R = 8
WIN_C = 16
D_FF = 2816
N_EXPERTS = 8
EPS = 1e-6
NEG_INF = -1e30

T_CTX = BATCH * SEQ
T_LAT = DEC_BATCH * DEC_SEQ
T = T_CTX + T_LAT
N_COND = 3
COND_ROWS = 8

TM = 1024
TF = 256
N_FCH = D_FF // TF
MOE_TM = 1024
MOE_TILES = (2 * T + N_EXPERTS * (MOE_TM - 1)) // MOE_TM
MOE_ROWS = MOE_TILES * MOE_TM
VMEM_LIMIT = 52 * 1024 * 1024


def _params(sem):
    return pltpu.CompilerParams(dimension_semantics=sem, vmem_limit_bytes=VMEM_LIMIT)


def _cond_of_tile(i, tm=TM):
    r = i * tm
    return jnp.where(r < T_CTX, 0, 1 + (r - T_CTX) // DEC_SEQ)


def _mod_spec(layer, k, grid_rank=1, tm=TM):
    if grid_rank == 1:
        return pl.BlockSpec((1, 1, 1, 1, D), lambda i: (layer, k, _cond_of_tile(i, tm), 0, 0))
    return pl.BlockSpec((1, 1, 1, 1, D), lambda i, f, *_: (layer, k, _cond_of_tile(i, tm), 0, 0))


def _rms_mod(x, g, scale, shift):
    y = x * lax.rsqrt(jnp.mean(x * x, axis=-1, keepdims=True) + EPS)
    return (y * g) * (1.0 + scale) + shift


def _mod_kernel(c_ref, w_ref, b_ref, o_ref):
    c = c_ref[...]
    s = c * jax.nn.sigmoid(c)
    o_ref[0, 0] = jnp.dot(s.astype(BF16), w_ref[0].astype(BF16), preferred_element_type=F32) + b_ref[0, 0]


def _modulation(cond, w_mod, b_mod):
    return pl.pallas_call(
        _mod_kernel,
        grid=(DEPTH, 6),
        in_specs=[
            pl.BlockSpec((COND_ROWS, D), lambda l, k: (0, 0)),
            pl.BlockSpec((1, D, D), lambda l, k: (l, 0, k)),
            pl.BlockSpec((1, 1, 1, D), lambda l, k: (l, k, 0, 0)),
        ],
        out_specs=pl.BlockSpec((1, 1, COND_ROWS, D), lambda l, k: (l, k, 0, 0)),
        out_shape=jax.ShapeDtypeStruct((DEPTH, 6, COND_ROWS, D), F32),
        compiler_params=_params(("arbitrary", "arbitrary")),
        name="adaln_mod",
    )(cond, w_mod, b_mod.reshape(DEPTH, 6, 1, D))


def _norm_kernel(x_ref, g_ref, sc_ref, sh_ref, o_ref):
    o_ref[...] = _rms_mod(x_ref[...], g_ref[...], sc_ref[0, 0, 0], sh_ref[0, 0, 0]).astype(o_ref.dtype)


def _first_norm(x, g, mods, layer):
    return pl.pallas_call(
        _norm_kernel,
        grid=(T // TM,),
        in_specs=[
            pl.BlockSpec((TM, D), lambda i: (i, 0)),
            pl.BlockSpec((1, D), lambda i: (0, 0)),
            _mod_spec(layer, 1),
            _mod_spec(layer, 0),
        ],
        out_specs=pl.BlockSpec((TM, D), lambda i: (i, 0)),
        out_shape=jax.ShapeDtypeStruct((T, D), BF16),
        compiler_params=_params(("arbitrary",)),
        name="first_norm",
    )(x, g.reshape(1, D), mods, mods)


def _mm_kernel(a_ref, w_ref, o_ref, wb_ref):
    @pl.when(pl.program_id(1) == 0)
    def _():
        wb_ref[...] = w_ref[...].astype(BF16)

    o_ref[...] = jnp.dot(a_ref[...], wb_ref[...], preferred_element_type=F32).astype(o_ref.dtype)


def _matmul(a, w, tn, out_dtype):
    m, k = a.shape
    n = w.shape[1]
    return pl.pallas_call(
        _mm_kernel,
        grid=(n // tn, m // TM),
        in_specs=[
            pl.BlockSpec((TM, k), lambda j, i: (i, 0)),
            pl.BlockSpec((k, tn), lambda j, i: (0, j)),
        ],
        out_specs=pl.BlockSpec((TM, tn), lambda j, i: (i, j)),
        out_shape=jax.ShapeDtypeStruct((m, n), out_dtype),
        scratch_shapes=[pltpu.VMEM((k, tn), BF16)],
        compiler_params=_params(("arbitrary", "arbitrary")),
        name="matmul",
    )(a, w)


def _mm_heads_kernel(a_ref, w_ref, o_ref, wb_ref, *, n_seq, seq_len, scale):
    @pl.when(pl.program_id(0) == 0)
    def _():
        wb_ref[...] = w_ref[...].astype(BF16)

    acc = jnp.dot(a_ref[...], wb_ref[...], preferred_element_type=F32)
    if scale != 1.0:
        acc = acc * scale
    for s in range(n_seq):
        for h in range(N_HEADS):
            o_ref[s, 0, h] = acc[s * seq_len:(s + 1) * seq_len,
                                 h * HEAD_DIM:(h + 1) * HEAD_DIM].astype(o_ref.dtype)


def _matmul_heads(hn, w_qkv_l, part, *, latent, out_dtype, scale=1.0, into=None, slot=0, n_slots=1):
    if latent:
        n_batch, seq_len, row0 = DEC_BATCH, DEC_SEQ, T_CTX // TM
    else:
        n_batch, seq_len, row0 = BATCH, SEQ, 0
    n_seq = TM // seq_len
    kern = functools.partial(_mm_heads_kernel, n_seq=n_seq, seq_len=seq_len, scale=scale)
    in_specs = [
        pl.BlockSpec((TM, D), lambda i: (row0 + i, 0)),
        pl.BlockSpec((D, D), lambda i: (0, part)),
    ]
    args = [hn, w_qkv_l]
    aliases = {}
    if into is not None:
        in_specs.append(pl.BlockSpec(memory_space=pl.ANY))
        args.append(into)
        aliases = {2: 0}
        body = lambda a, w, _, o, wb: kern(a, w, o, wb)
    else:
        body = kern
    return pl.pallas_call(
        body,
        grid=(n_batch // n_seq,),
        in_specs=in_specs,
        out_specs=pl.BlockSpec((n_seq, 1, N_HEADS, seq_len, HEAD_DIM), lambda i: (i, slot, 0, 0, 0)),
        out_shape=jax.ShapeDtypeStruct((n_batch, n_slots, N_HEADS, seq_len, HEAD_DIM), out_dtype),
        scratch_shapes=[pltpu.VMEM((D, D), BF16)],
        input_output_aliases=aliases,
        compiler_params=_params(("arbitrary",)),
        name="qkv_heads",
    )(*args)


def _softmax_pv(s_list, v_list):
    m = s_list[0].max(axis=-1, keepdims=True)
    for s in s_list[1:]:
        m = jnp.maximum(m, s.max(axis=-1, keepdims=True))
    den = None
    out = None
    for s, v in zip(s_list, v_list):
        p = jnp.exp(s - m)
        d = p.sum(axis=-1, keepdims=True)
        o = jnp.dot(p.astype(BF16), v, preferred_element_type=F32)
        den = d if den is None else den + d
        out = o if out is None else out + o
    return out / den


def _qk(q, k):
    return lax.dot_general(q, k, (((1,), (1,)), ((), ())), preferred_element_type=F32)


def _ctx_attn_kernel(q_ref, k_ref, v_ref, o_ref):
    for h in range(N_HEADS):
        q = q_ref[0, 0, h]
        k = k_ref[0, 0, h].astype(BF16)
        v = v_ref[0, 0, h].astype(BF16)
        o_ref[0, h] = _softmax_pv([_qk(q, k)], [v]).astype(o_ref.dtype)


def _context_attention(q, kc, vc, slot):
    kv_spec = pl.BlockSpec((1, 1, N_HEADS, SEQ, HEAD_DIM), lambda b: (b, slot, 0, 0, 0))
    return pl.pallas_call(
        _ctx_attn_kernel,
        grid=(BATCH,),
        in_specs=[pl.BlockSpec((1, 1, N_HEADS, SEQ, HEAD_DIM), lambda b: (b, 0, 0, 0, 0)), kv_spec, kv_spec],
        out_specs=pl.BlockSpec((1, N_HEADS, SEQ, HEAD_DIM), lambda b: (b, 0, 0, 0)),
        out_shape=jax.ShapeDtypeStruct((BATCH, N_HEADS, SEQ, HEAD_DIM), BF16),
        compiler_params=_params(("arbitrary",)),
        name="ctx_attention",
    )(q, kc, vc)


def _row_start(r):
    return min(max(r - WIN_R // 2, 0), GRID_ROWS - WIN_R)


def _nbr_attn_kernel(q_ref, k_ref, v_ref, kc_ref, vc_ref, bias_ref, o_ref):
    kc = kc_ref[0, 0, 0].astype(BF16)
    vc = vc_ref[0, 0, 0].astype(BF16)
    for r in range(GRID_ROWS):
        rs = _row_start(r)
        q = q_ref[0, 0, 0, r * GRID_W:(r + 1) * GRID_W, :]
        kw = k_ref[0, 0, 0, rs * GRID_W:(rs + WIN_R) * GRID_W, :].astype(BF16)
        vw = v_ref[0, 0, 0, rs * GRID_W:(rs + WIN_R) * GRID_W, :].astype(BF16)
        s_win = _qk(q, kw) + bias_ref[0, r - rs]
        s_ctx = _qk(q, kc)
        o_ref[0, 0, r * GRID_W:(r + 1) * GRID_W, :] = _softmax_pv([s_win, s_ctx], [vw, vc]).astype(o_ref.dtype)


def _window_bias(rpb_l):
    col = jnp.arange(GRID_W)
    col_start = jnp.clip(col - WIN_C // 2, 0, GRID_W - WIN_C)
    col_mask = (col[None, :] >= col_start[:, None]) & (col[None, :] < col_start[:, None] + WIN_C)
    dc = jnp.clip(col[None, :] - col[:, None], -(WIN_C - 1), WIN_C - 1) + (WIN_C - 1)
    dr = jnp.arange(WIN_R)[None, :] - jnp.arange(WIN_R)[:, None] + (WIN_R - 1)
    bias = rpb_l[:, dr[:, None, :, None], dc[None, :, None, :]]
    bias = jnp.where(col_mask[None, None, :, None, :], bias, NEG_INF)
    return bias.reshape(N_HEADS, WIN_R, GRID_W, WIN_R * GRID_W)


def _neighbourhood_attention(q, k, v, cache_k, cache_v, bias, slot):
    lat = lambda: pl.BlockSpec((1, 1, 1, DEC_SEQ, HEAD_DIM), lambda b, h: (b, 0, h, 0, 0))
    ctx = lambda: pl.BlockSpec((1, 1, 1, PAST_LEN, HEAD_DIM), lambda b, h: (b, slot, h, 0, 0))
    return pl.pallas_call(
        _nbr_attn_kernel,
        grid=(DEC_BATCH, N_HEADS),
        in_specs=[lat(), lat(), lat(), ctx(), ctx(),
                  pl.BlockSpec((1, WIN_R, GRID_W, WIN_R * GRID_W), lambda b, h: (h, 0, 0, 0))],
        out_specs=pl.BlockSpec((1, 1, DEC_SEQ, HEAD_DIM), lambda b, h: (b, h, 0, 0)),
        out_shape=jax.ShapeDtypeStruct((DEC_BATCH, N_HEADS, DEC_SEQ, HEAD_DIM), BF16),
        compiler_params=_params(("arbitrary", "arbitrary")),
        name="nbr_attention",
    )(q, k, v, cache_k, cache_v, bias)


def _top2(logits):
    col = lax.broadcasted_iota(jnp.int32, logits.shape, 1)
    m1 = jnp.max(logits, axis=-1, keepdims=True)
    i1 = jnp.min(jnp.where(logits == m1, col, N_EXPERTS), axis=-1, keepdims=True)
    rest = jnp.where(col == i1, -jnp.inf, logits)
    m2 = jnp.max(rest, axis=-1, keepdims=True)
    i2 = jnp.min(jnp.where(rest == m2, col, N_EXPERTS), axis=-1, keepdims=True)
    e = jnp.exp(m2 - m1)
    return i1, i2, 1.0 / (1.0 + e), e / (1.0 + e)


def _proj_res_norm_kernel(a_ref, w_ref, x_ref, g1_ref, ng_ref, sc_ref, sh_ref, *rest, router):
    if router:
        wr_ref, xo_ref, hn_ref, idx_ref, gate_ref, wb_ref = rest
    else:
        xo_ref, hn_ref, wb_ref = rest

    @pl.when(pl.program_id(0) == 0)
    def _():
        wb_ref[...] = w_ref[...].astype(BF16)

    mix = jnp.dot(a_ref[...], wb_ref[...], preferred_element_type=F32)
    x = x_ref[...] + g1_ref[0, 0, 0] * mix
    xo_ref[...] = x
    hn = _rms_mod(x, ng_ref[...], sc_ref[0, 0, 0], sh_ref[0, 0, 0])
    hn_ref[...] = hn.astype(hn_ref.dtype)
    if router:
        logits = jnp.dot(hn, wr_ref[...], preferred_element_type=F32, precision=lax.Precision.HIGHEST)
        i1, i2, p1, p2 = _top2(logits)
        first = lax.broadcasted_iota(jnp.int32, (logits.shape[0], 2), 1) == 0
        idx_ref[...] = jnp.where(first, i1, i2)
        gate_ref[...] = jnp.where(first, p1, p2)


def _proj_res_norm(a, w, x, norm_g, mods, layer, w_router=None):
    router = w_router is not None
    row = lambda n: pl.BlockSpec((TM, n), lambda i: (i, 0))
    in_specs = [row(D), pl.BlockSpec((D, D), lambda i: (0, 0)), row(D),
                _mod_spec(layer, 2), pl.BlockSpec((1, D), lambda i: (0, 0)), _mod_spec(layer, 4), _mod_spec(layer, 3)]
    args = [a, w, x, mods, norm_g.reshape(1, D), mods, mods]
    out_specs = [row(D), row(D)]
    out_shape = [jax.ShapeDtypeStruct((T, D), F32), jax.ShapeDtypeStruct((T, D), BF16)]
    if router:
        in_specs.append(pl.BlockSpec((D, N_EXPERTS), lambda i: (0, 0)))
        args.append(w_router)
        out_specs += [row(2), row(2)]
        out_shape += [jax.ShapeDtypeStruct((T, 2), jnp.int32), jax.ShapeDtypeStruct((T, 2), F32)]
    return pl.pallas_call(
        functools.partial(_proj_res_norm_kernel, router=router),
        grid=(T // TM,),
        in_specs=in_specs,
        out_specs=out_specs,
        out_shape=out_shape,
        scratch_shapes=[pltpu.VMEM((D, D), BF16)],
        compiler_params=_params(("arbitrary",)),
        name="proj_res_norm",
    )(*args)


def _swiglu_chunk(hn, wg_ref, wu_ref, wd_ref, acc_ref):
    f = pl.program_id(1)
    g = jnp.dot(hn, wg_ref[0].astype(BF16), preferred_element_type=F32)
    u = jnp.dot(hn, wu_ref[0].astype(BF16), preferred_element_type=F32)
    h = (g * jax.nn.sigmoid(g) * u).astype(BF16)
    part = jnp.dot(h, wd_ref[0].astype(BF16), preferred_element_type=F32)

    @pl.when(f == 0)
    def _():
        acc_ref[...] = part

    @pl.when(f > 0)
    def _():
        acc_ref[...] += part


def _ffn_kernel(hn_ref, wg_ref, wu_ref, wd_ref, x_ref, g2_ref, ng_ref, sc_ref, sh_ref, xo_ref, hno_ref, acc_ref):
    _swiglu_chunk(hn_ref[...], wg_ref, wu_ref, wd_ref, acc_ref)

    @pl.when(pl.program_id(1) == N_FCH - 1)
    def _():
        x = x_ref[...] + g2_ref[0, 0, 0] * acc_ref[...]
        xo_ref[...] = x
        hno_ref[...] = _rms_mod(x, ng_ref[...], sc_ref[0, 0, 0], sh_ref[0, 0, 0]).astype(hno_ref.dtype)


def _ffn(hn, wg, wu, wd, idx, x, mods, layer, next_g):
    row = lambda: pl.BlockSpec((TM, D), lambda i, f: (i, 0))
    return pl.pallas_call(
        _ffn_kernel,
        grid=(T // TM, N_FCH),
        in_specs=[row(),
                  pl.BlockSpec((1, D, TF), lambda i, f: (idx, 0, f)),
                  pl.BlockSpec((1, D, TF), lambda i, f: (idx, 0, f)),
                  pl.BlockSpec((1, TF, D), lambda i, f: (idx, f, 0)),
                  row(), _mod_spec(layer, 5, 2), pl.BlockSpec((1, D), lambda i, f: (0, 0)),
                  _mod_spec(layer + 1, 1, 2), _mod_spec(layer + 1, 0, 2)],
        out_specs=[row(), row()],
        out_shape=[jax.ShapeDtypeStruct((T, D), F32), jax.ShapeDtypeStruct((T, D), BF16)],
        scratch_shapes=[pltpu.VMEM((TM, D), F32)],
        compiler_params=_params(("arbitrary", "arbitrary")),
        name="ffn",
    )(hn, wg, wu, wd, x, mods, next_g.reshape(1, D), mods, mods)


def _moe_kernel(te_ref, nu_ref, xs_ref, wg_ref, wu_ref, wd_ref, y_ref, acc_ref):
    i = pl.program_id(0)
    last = pl.program_id(1) == N_FCH - 1

    @pl.when(i < nu_ref[0])
    def _():
        _swiglu_chunk(xs_ref[...], wg_ref, wu_ref, wd_ref, acc_ref)

        @pl.when(last)
        def _():
            y_ref[...] = acc_ref[...]

    @pl.when(jnp.logical_and(i >= nu_ref[0], last))
    def _():
        y_ref[...] = jnp.zeros_like(y_ref)


def _moe_experts(xs, tile_expert, n_used, wg, wu, wd, idx):
    def wspec(shape, col):
        def imap(i, f, te, nu):
            fe = jnp.where(i < nu[0], f, N_FCH - 1)
            return (idx, te[i], 0, fe) if col else (idx, te[i], fe, 0)
        return pl.BlockSpec(shape, imap)

    def kern(te, nu, xs_ref, wg_ref, wu_ref, wd_ref, y_ref, acc_ref):
        _moe_kernel(te, nu, xs_ref, wg_ref.at[0], wu_ref.at[0], wd_ref.at[0], y_ref, acc_ref)

    return pl.pallas_call(
        kern,
        grid_spec=pltpu.PrefetchScalarGridSpec(
            num_scalar_prefetch=2,
            grid=(MOE_TILES, N_FCH),
            in_specs=[pl.BlockSpec((MOE_TM, D), lambda i, f, te, nu: (i, 0)),
                      wspec((1, 1, D, TF), True), wspec((1, 1, D, TF), True), wspec((1, 1, TF, D), False)],
            out_specs=pl.BlockSpec((MOE_TM, D), lambda i, f, te, nu: (i, 0)),
            scratch_shapes=[pltpu.VMEM((MOE_TM, D), F32)],
        ),
        out_shape=jax.ShapeDtypeStruct((MOE_ROWS, D), F32),
        compiler_params=_params(("arbitrary", "arbitrary")),
        name="moe_experts",
    )(tile_expert, n_used, xs, wg, wu, wd)


def _combine_kernel(x_ref, y0_ref, y1_ref, gate_ref, g2_ref, ng_ref, sc_ref, sh_ref, *outs, final):
    gates = gate_ref[...]
    moe = gates[:, 0:1] * y0_ref[...] + gates[:, 1:2] * y1_ref[...]
    x = x_ref[...] + g2_ref[0, 0, 0] * moe
    if final:
        (yo_ref,) = outs
        yo_ref[...] = _rms_mod(x, ng_ref[...], 0.0, 0.0)
    else:
        xo_ref, hno_ref = outs
        xo_ref[...] = x
        hno_ref[...] = _rms_mod(x, ng_ref[...], sc_ref[0, 0, 0], sh_ref[0, 0, 0]).astype(hno_ref.dtype)


def _moe_combine(x, y0, y1, gates, mods, layer, next_g, final):
    row = lambda n=D: pl.BlockSpec((TM, n), lambda i: (i, 0))
    nl = layer if final else layer + 1
    if final:
        out_specs, out_shape = [row()], [jax.ShapeDtypeStruct((T, D), F32)]
    else:
        out_specs = [row(), row()]
        out_shape = [jax.ShapeDtypeStruct((T, D), F32), jax.ShapeDtypeStruct((T, D), BF16)]
    return pl.pallas_call(
        functools.partial(_combine_kernel, final=final),
        grid=(T // TM,),
        in_specs=[row(), row(), row(), row(2), _mod_spec(layer, 5), pl.BlockSpec((1, D), lambda i: (0, 0)),
                  _mod_spec(nl, 1), _mod_spec(nl, 0)],
        out_specs=out_specs,
        out_shape=out_shape,
        compiler_params=_params(("arbitrary",)),
        name="moe_combine",
    )(x, y0, y1, gates, mods, next_g.reshape(1, D), mods, mods)


LRU_CHUNK = 256


def _shift_rows(v, s, row):
    n = v.shape[0]
    rolled = pltpu.roll(v, s % n, axis=0)
    keep = (row >= s) if s > 0 else (row < n + s)
    return jnp.where(keep, rolled, 0.0)


def _lru_kernel(proj_ref, h0_ref, cw_ref, cb_ref, wbd_ref, bbd_ref, lam_ref, scw_ref, scb_ref,
                y_ref, st_ref, xc_ref, a_ref, b_ref, *, seq_len):
    row = lax.broadcasted_iota(jnp.int32, (seq_len, 1), 0)
    x = proj_ref[:, 0:LRU_W]
    xc = (cb_ref[...] + _shift_rows(x, 2, row) * cw_ref[0:1] + _shift_rows(x, 1, row) * cw_ref[1:2]
          + x * cw_ref[2:3] + _shift_rows(x, -1, row) * cw_ref[3:4])
    xc_ref[...] = xc

    lam = lam_ref[...]
    neg_csp = -LRU_C * (jnp.maximum(-lam, 0.0) + jnp.log1p(jnp.exp(-jnp.abs(lam))))

    for c in range(seq_len // LRU_CHUNK):
        rows = pl.ds(c * LRU_CHUNK, LRU_CHUNK)
        xcc = xc_ref[rows, :]
        gates = jnp.dot(xcc.astype(BF16), wbd_ref[...], preferred_element_type=F32) + bbd_ref[...]
        for d in range(2):
            r = jax.nn.sigmoid(gates[:, (2 * d) * LRU_W:(2 * d + 1) * LRU_W])
            i = jax.nn.sigmoid(gates[:, (2 * d + 1) * LRU_W:(2 * d + 2) * LRU_W])
            log_a = neg_csp[d:d + 1] * r
            t = jnp.tanh(log_a)
            one_minus_a2 = -2.0 * t / (1.0 - t)
            a_ref[d, rows, :] = jnp.exp(log_a)
            b_ref[d, rows, :] = jnp.sqrt(one_minus_a2) * (i * xcc)

    n_grp = seq_len // 8

    def scan_body(g, carry):
        hf, hb = carry
        rf = pl.ds(pl.multiple_of(g * 8, 8), 8)
        rb = pl.ds(pl.multiple_of((n_grp - 1 - g) * 8, 8), 8)
        af, bf = a_ref[0, rf, :], b_ref[0, rf, :]
        ab, bb = a_ref[1, rb, :], b_ref[1, rb, :]
        outs_f, outs_b = [], []
        for j in range(8):
            hf = af[j:j + 1] * hf + bf[j:j + 1]
            outs_f.append(hf)
            hb = ab[7 - j:8 - j] * hb + bb[7 - j:8 - j]
            outs_b.append(hb)
        b_ref[0, rf, :] = jnp.concatenate(outs_f, axis=0)
        b_ref[1, rb, :] = jnp.concatenate(outs_b[::-1], axis=0)
        return hf, hb

    hf, hb = lax.fori_loop(0, n_grp, scan_body, (h0_ref[0, 0:1, :], h0_ref[0, 1:2, :]))
    st_ref[0, 0:1, :] = hf
    st_ref[0, 1:2, :] = hb

    h_sum = b_ref[0] + b_ref[1]
    y_ref[:, 0:LRU_W] = (h_sum * jax.nn.gelu(proj_ref[:, LRU_W:2 * LRU_W])).astype(y_ref.dtype)
    o = 2 * LRU_W
    cv = proj_ref[:, o + SC_W:o + 2 * SC_W] * proj_ref[:, o + 2 * SC_W:o + 3 * SC_W]
    conv = scb_ref[...] + _shift_rows(cv, 1, row) * scw_ref[0:1] + cv * scw_ref[1:2] + _shift_rows(cv, -1, row) * scw_ref[2:3]
    y_ref[:, LRU_W:LRU_W + SC_W] = (proj_ref[:, o:o + SC_W] * conv).astype(y_ref.dtype)


def _lru_mixer(proj, h0, cw, cb, wbd, bbd, lam, scw, scb, *, latent):
    if latent:
        n_seq, seq_len, blk0 = DEC_BATCH, DEC_SEQ, T_CTX // DEC_SEQ
    else:
        n_seq, seq_len, blk0 = BATCH, SEQ, 0
    full = lambda shape: pl.BlockSpec(shape, lambda b: (0,) * len(shape))
    return pl.pallas_call(
        functools.partial(_lru_kernel, seq_len=seq_len),
        grid=(n_seq,),
        in_specs=[pl.BlockSpec((seq_len, IN_W), lambda b: (blk0 + b, 0)),
                  pl.BlockSpec((1, 2, LRU_W), lambda b: (b, 0, 0)),
                  full((4, LRU_W)), full((1, LRU_W)), full((LRU_W, 4 * LRU_W)), full((1, 4 * LRU_W)),
                  full((2, LRU_W)), full((3, SC_W)), full((1, SC_W))],
        out_specs=[pl.BlockSpec((seq_len, D), lambda b: (b, 0)),
                   pl.BlockSpec((1, 2, LRU_W), lambda b: (b, 0, 0))],
        out_shape=[jax.ShapeDtypeStruct((n_seq * seq_len, D), BF16),
                   jax.ShapeDtypeStruct((n_seq, 2, LRU_W), F32)],
        scratch_shapes=[pltpu.VMEM((seq_len, LRU_W), F32),
                        pltpu.VMEM((2, seq_len, LRU_W), F32),
                        pltpu.VMEM((2, seq_len, LRU_W), F32)],
        compiler_params=_params(("arbitrary",)),
        name="lru_mixer",
    )(proj, h0, cw, cb.reshape(1, LRU_W), wbd, bbd, lam, scw, scb.reshape(1, SC_W))


def _block_diag_dense(w):
    eye = jnp.eye(LRU_HEADS, dtype=w.dtype)
    return jnp.einsum('hij,hg->higj', w, eye).reshape(LRU_W, LRU_W)


def _route(idx):
    e_flat = idx.reshape(-1)
    onehot = (e_flat[:, None] == jnp.arange(N_EXPERTS)[None, :]).astype(jnp.int32)
    csum = jnp.cumsum(onehot, axis=0)
    counts = csum[-1]
    rank = jnp.take_along_axis(csum, e_flat[:, None], axis=1)[:, 0] - 1
    padded = (counts + MOE_TM - 1) // MOE_TM * MOE_TM
    ends = jnp.cumsum(padded)
    dest = (ends - padded)[e_flat] + rank
    src = jnp.zeros((MOE_ROWS,), jnp.int32).at[dest].set(jnp.arange(2 * T, dtype=jnp.int32) // 2)
    tile_end = ends // MOE_TM
    n_used = tile_end[-1]
    tiles = jnp.minimum(jnp.arange(MOE_TILES), n_used - 1)
    tile_expert = jnp.sum(tiles[:, None] >= tile_end[None, :], axis=1).astype(jnp.int32)
    return dest.reshape(T, 2), src, tile_expert, n_used.reshape(1).astype(jnp.int32)


def kernel(x_prompt, x_sample, state_lru, cache_k, cache_v, c, c_ctx, norm1_g, norm2_g, w_mod, b_mod, w_in, lru_conv_w, lru_conv_b, lru_wa, lru_ba, lru_wx, lru_bx, lru_lam, sc_conv_w, sc_conv_b, w_out, ffn_w_gate, ffn_w_up, ffn_w_down, w_qkv, w_o, rpb, w_router, moe_w_gate, moe_w_up, moe_w_down, final_g):
    x = jnp.concatenate([x_prompt.reshape(T_CTX, D), x_sample.reshape(T_LAT, D)], axis=0)
    cond = jnp.concatenate([c_ctx[None, :], c, jnp.zeros((COND_ROWS - N_COND, D), F32)], axis=0)
    mods = _modulation(cond, w_mod, b_mod).reshape(DEPTH, 6, COND_ROWS, 1, D)

    hn = _first_norm(x, norm1_g[0], mods, 0)
    zero_state = jnp.zeros((BATCH, 2, LRU_W), F32)
    states = []
    new_k = new_v = None
    y_final = None
    for l in range(DEPTH):
        idx = l // 2
        if l % 2 == 0:
            proj = _matmul(hn, w_in[idx], IN_W // 2, F32)
            wbd = jnp.concatenate([_block_diag_dense(lru_wa[idx, 0]), _block_diag_dense(lru_wx[idx, 0]),
                                   _block_diag_dense(lru_wa[idx, 1]), _block_diag_dense(lru_wx[idx, 1])],
                                  axis=1).astype(BF16)
            bbd = jnp.concatenate([lru_ba[idx, 0], lru_bx[idx, 0], lru_ba[idx, 1], lru_bx[idx, 1]]).reshape(1, 4 * LRU_W)
            lru_args = (lru_conv_w[idx], lru_conv_b[idx], wbd, bbd, lru_lam[idx], sc_conv_w[idx], sc_conv_b[idx])
            y_ctx, st = _lru_mixer(proj, zero_state, *lru_args, latent=False)
            y_lat, _ = _lru_mixer(proj, state_lru[:, idx], *lru_args, latent=True)
            states.append(st)
            mix_in = jnp.concatenate([y_ctx, y_lat], axis=0)
            x, hn2 = _proj_res_norm(mix_in, w_out[idx], x, norm2_g[l], mods, l)
            x, hn = _ffn(hn2, ffn_w_gate, ffn_w_up, ffn_w_down, idx, x, mods, l, norm1_g[l + 1])
        else:
            wq = w_qkv[idx]
            q_scale = HEAD_DIM ** -0.5
            q_ctx = _matmul_heads(hn, wq, 0, latent=False, out_dtype=BF16, scale=q_scale)
            new_k = _matmul_heads(hn, wq, 1, latent=False, out_dtype=F32, into=new_k, slot=idx, n_slots=2)
            new_v = _matmul_heads(hn, wq, 2, latent=False, out_dtype=F32, into=new_v, slot=idx, n_slots=2)
            q_lat = _matmul_heads(hn, wq, 0, latent=True, out_dtype=BF16, scale=q_scale)
            k_lat = _matmul_heads(hn, wq, 1, latent=True, out_dtype=F32)
            v_lat = _matmul_heads(hn, wq, 2, latent=True, out_dtype=F32)
            o_ctx = _context_attention(q_ctx, new_k, new_v, idx)
            o_lat = _neighbourhood_attention(q_lat, k_lat, v_lat, cache_k, cache_v, _window_bias(rpb[idx]), idx)
            attn = jnp.concatenate([o_ctx.transpose(0, 2, 1, 3).reshape(T_CTX, D),
                                    o_lat.transpose(0, 2, 1, 3).reshape(T_LAT, D)], axis=0)
            x, hn2, top_i, top_p = _proj_res_norm(attn, w_o[idx], x, norm2_g[l], mods, l, w_router[idx])
            dest, src, tile_expert, n_used = _route(top_i)
            xs = jnp.take(hn2, src, axis=0)
            y = _moe_experts(xs, tile_expert, n_used, moe_w_gate, moe_w_up, moe_w_down, idx)
            y0 = jnp.take(y, dest[:, 0], axis=0)
            y1 = jnp.take(y, dest[:, 1], axis=0)
            if l == DEPTH - 1:
                (y_final,) = _moe_combine(x, y0, y1, top_p, mods, l, final_g, True)
            else:
                x, hn = _moe_combine(x, y0, y1, top_p, mods, l, norm1_g[l + 1], False)

    y_prompt = y_final[:T_CTX].reshape(BATCH, SEQ, D)
    y_sample = y_final[T_CTX:].reshape(DEC_BATCH, DEC_SEQ, D)
    return (y_prompt, y_sample, jnp.stack(states, axis=1), new_k, new_v)
```

```python
import functools

import jax
import jax.numpy as jnp
from jax import lax
from jax.experimental import pallas as pl
from jax.experimental.pallas import tpu as pltpu

F32 = jnp.float32
BF16 = jnp.bfloat16

D = 1024
BATCH = 32
SEQ = 256
DEPTH = 4
DEC_BATCH = 2
DEC_SEQ = 1024
PAST_LEN = 256
GRID_W = 64
GRID_ROWS = DEC_SEQ // GRID_W
LRU_W = 512
LRU_HEADS = 8
LRU_BLOCK = LRU_W // LRU_HEADS
LRU_C = 8.0
SC_W = 512
IN_W = 2 * LRU_W + 3 * SC_W
N_HEADS = 16
HEAD_DIM = D // N_HEADS
WIN_R = 8
WIN_C = 16
D_FF = 2816
N_EXPERTS = 8
EPS = 1e-6
NEG_INF = -1e30

T_CTX = BATCH * SEQ
T_LAT = DEC_BATCH * DEC_SEQ
T = T_CTX + T_LAT
N_COND = 3
COND_ROWS = 8

TM = 1024
TF = 256
N_FCH = D_FF // TF
MOE_TM = 1024
MOE_TILES = (2 * T + N_EXPERTS * (MOE_TM - 1)) // MOE_TM
MOE_ROWS = MOE_TILES * MOE_TM
VMEM_LIMIT = 52 * 1024 * 1024


def _params(sem):
    return pltpu.CompilerParams(dimension_semantics=sem, vmem_limit_bytes=VMEM_LIMIT)


def _cond_of_tile(i, tm=TM):
    r = i * tm
    return jnp.where(r < T_CTX, 0, 1 + (r - T_CTX) // DEC_SEQ)


def _mod_spec(layer, k, grid_rank=1, tm=TM):
    if grid_rank == 1:
        return pl.BlockSpec((1, 1, 1, 1, D), lambda i: (layer, k, _cond_of_tile(i, tm), 0, 0))
    return pl.BlockSpec((1, 1, 1, 1, D), lambda i, f, *_: (layer, k, _cond_of_tile(i, tm), 0, 0))


def _rms_mod(x, g, scale, shift):
    y = x * lax.rsqrt(jnp.mean(x * x, axis=-1, keepdims=True) + EPS)
    return (y * g) * (1.0 + scale) + shift


def _mod_kernel(c_ref, w_ref, b_ref, o_ref):
    c = c_ref[...]
    s = c * jax.nn.sigmoid(c)
    o_ref[0, 0] = jnp.dot(s.astype(BF16), w_ref[0].astype(BF16), preferred_element_type=F32) + b_ref[0, 0]


def _modulation(cond, w_mod, b_mod):
    return pl.pallas_call(
        _mod_kernel,
        grid=(DEPTH, 6),
        in_specs=[
            pl.BlockSpec((COND_ROWS, D), lambda l, k: (0, 0)),
            pl.BlockSpec((1, D, D), lambda l, k: (l, 0, k)),
            pl.BlockSpec((1, 1, 1, D), lambda l, k: (l, k, 0, 0)),
        ],
        out_specs=pl.BlockSpec((1, 1, COND_ROWS, D), lambda l, k: (l, k, 0, 0)),
        out_shape=jax.ShapeDtypeStruct((DEPTH, 6, COND_ROWS, D), F32),
        compiler_params=_params(("arbitrary", "arbitrary")),
        name="adaln_mod",
    )(cond, w_mod, b_mod.reshape(DEPTH, 6, 1, D))


def _norm_kernel(x_ref, g_ref, sc_ref, sh_ref, o_ref):
    o_ref[...] = _rms_mod(x_ref[...], g_ref[...], sc_ref[0, 0, 0], sh_ref[0, 0, 0]).astype(o_ref.dtype)


def _first_norm(x, g, mods, layer):
    return pl.pallas_call(
        _norm_kernel,
        grid=(T // TM,),
        in_specs=[
            pl.BlockSpec((TM, D), lambda i: (i, 0)),
            pl.BlockSpec((1, D), lambda i: (0, 0)),
            _mod_spec(layer, 1),
            _mod_spec(layer, 0),
        ],
        out_specs=pl.BlockSpec((TM, D), lambda i: (i, 0)),
        out_shape=jax.ShapeDtypeStruct((T, D), BF16),
        compiler_params=_params(("arbitrary",)),
        name="first_norm",
    )(x, g.reshape(1, D), mods, mods)


def _mm_kernel(a_ref, w_ref, o_ref, wb_ref):
    @pl.when(pl.program_id(1) == 0)
    def _():
        wb_ref[...] = w_ref[...].astype(BF16)

    o_ref[...] = jnp.dot(a_ref[...], wb_ref[...], preferred_element_type=F32).astype(o_ref.dtype)


def _matmul(a, w, tn, out_dtype):
    m, k = a.shape
    n = w.shape[1]
    return pl.pallas_call(
        _mm_kernel,
        grid=(n // tn, m // TM),
        in_specs=[
            pl.BlockSpec((TM, k), lambda j, i: (i, 0)),
            pl.BlockSpec((k, tn), lambda j, i: (0, j)),
        ],
        out_specs=pl.BlockSpec((TM, tn), lambda j, i: (i, j)),
        out_shape=jax.ShapeDtypeStruct((m, n), out_dtype),
        scratch_shapes=[pltpu.VMEM((k, tn), BF16)],
        compiler_params=_params(("arbitrary", "arbitrary")),
        name="matmul",
    )(a, w)


def _mm_heads_kernel(a_ref, w_ref, o_ref, wb_ref, acc_ref, *, n_seq, seq_len, scale, transposed):
    @pl.when(pl.program_id(0) == 0)
    def _():
        wb_ref[...] = w_ref[...].astype(BF16)

    acc = jnp.dot(a_ref[...], wb_ref[...], preferred_element_type=F32)
    if scale != 1.0:
        acc = acc * scale
    if transposed:
        acc_ref[...] = acc
        acc_t = acc_ref[...].T
        for s in range(n_seq):
            for h in range(N_HEADS):
                o_ref[s, 0, h] = acc_t[h * HEAD_DIM:(h + 1) * HEAD_DIM,
                                       s * seq_len:(s + 1) * seq_len].astype(o_ref.dtype)
    else:
        for s in range(n_seq):
            for h in range(N_HEADS):
                o_ref[s, 0, h] = acc[s * seq_len:(s + 1) * seq_len,
                                     h * HEAD_DIM:(h + 1) * HEAD_DIM].astype(o_ref.dtype)


def _matmul_heads(hn, w_qkv_l, part, *, latent, out_dtype, scale=1.0, into=None, slot=0, n_slots=1,
                  transposed=False):
    if latent:
        n_batch, seq_len, row0 = DEC_BATCH, DEC_SEQ, T_CTX // TM
    else:
        n_batch, seq_len, row0 = BATCH, SEQ, 0
    n_seq = TM // seq_len
    tail = (HEAD_DIM, seq_len) if transposed else (seq_len, HEAD_DIM)
    kern = functools.partial(_mm_heads_kernel, n_seq=n_seq, seq_len=seq_len, scale=scale, transposed=transposed)
    in_specs = [
        pl.BlockSpec((TM, D), lambda i: (row0 + i, 0)),
        pl.BlockSpec((D, D), lambda i: (0, part)),
    ]
    args = [hn, w_qkv_l]
    aliases = {}
    if into is not None:
        in_specs.append(pl.BlockSpec(memory_space=pl.ANY))
        args.append(into)
        aliases = {2: 0}
        body = lambda a, w, _, o, wb, acc: kern(a, w, o, wb, acc)
    else:
        body = kern
    return pl.pallas_call(
        body,
        grid=(n_batch // n_seq,),
        in_specs=in_specs,
        out_specs=pl.BlockSpec((n_seq, 1, N_HEADS) + tail, lambda i: (i, slot, 0, 0, 0)),
        out_shape=jax.ShapeDtypeStruct((n_batch, n_slots, N_HEADS) + tail, out_dtype),
        scratch_shapes=[pltpu.VMEM((D, D), BF16), pltpu.VMEM((TM, D), F32)],
        input_output_aliases=aliases,
        compiler_params=_params(("arbitrary",)),
        name="qkv_heads",
    )(*args)


def _dot_nt(a, b):
    return lax.dot_general(a, b, (((1,), (1,)), ((), ())), preferred_element_type=F32)


def _softmax_pv(s_list, v_list, v_transposed=False):
    m = s_list[0].max(axis=-1, keepdims=True)
    for s in s_list[1:]:
        m = jnp.maximum(m, s.max(axis=-1, keepdims=True))
    den = None
    out = None
    for s, v in zip(s_list, v_list):
        p = jnp.exp(s - m)
        d = p.sum(axis=-1, keepdims=True)
        pb = p.astype(BF16)
        o = _dot_nt(pb, v) if v_transposed else jnp.dot(pb, v, preferred_element_type=F32)
        den = d if den is None else den + d
        out = o if out is None else out + o
    return out / den


HEAD_PAIR_W = 2 * HEAD_DIM


def _ctx_attn_kernel(q_ref, kt_ref, vt_ref, o_ref):
    for hp in range(N_HEADS // 2):
        pair = []
        for h in (2 * hp, 2 * hp + 1):
            s = jnp.dot(q_ref[0, 0, h], kt_ref[0, 0, h].astype(BF16), preferred_element_type=F32)
            pair.append(_softmax_pv([s], [vt_ref[0, 0, h].astype(BF16)], v_transposed=True))
        o_ref[:, hp * HEAD_PAIR_W:(hp + 1) * HEAD_PAIR_W] = jnp.concatenate(pair, axis=-1).astype(o_ref.dtype)


def _context_attention(q, kt, vt, slot):
    kv_spec = pl.BlockSpec((1, 1, N_HEADS, HEAD_DIM, SEQ), lambda b: (b, slot, 0, 0, 0))
    return pl.pallas_call(
        _ctx_attn_kernel,
        grid=(BATCH,),
        in_specs=[pl.BlockSpec((1, 1, N_HEADS, SEQ, HEAD_DIM), lambda b: (b, 0, 0, 0, 0)), kv_spec, kv_spec],
        out_specs=pl.BlockSpec((SEQ, D), lambda b: (b, 0)),
        out_shape=jax.ShapeDtypeStruct((T_CTX, D), BF16),
        compiler_params=_params(("arbitrary",)),
        name="ctx_attention",
    )(q, kt, vt)


def _row_start(r):
    return min(max(r - WIN_R // 2, 0), GRID_ROWS - WIN_R)


def _nbr_attn_kernel(q_ref, k_ref, v_ref, kc_ref, vc_ref, bias_ref, o_ref):
    kc =[kc_ref[0, 0, hh].astype(BF16) for hh in range(2)]
    vc = [vc_ref[0, 0, hh].astype(BF16) for hh in range(2)]
    for r in range(GRID_ROWS):
        rs = _row_start(r)
        pair = []
        for hh in range(2):
            q = q_ref[0, 0, hh, r * GRID_W:(r + 1) * GRID_W, :]
            kw = k_ref[0, 0, hh, rs * GRID_W:(rs + WIN_R) * GRID_W, :].astype(BF16)
            vw = v_ref[0, 0, hh, rs * GRID_W:(rs + WIN_R) * GRID_W, :].astype(BF16)
            s_win = _dot_nt(q, kw) + bias_ref[hh, r - rs]
            s_ctx = _dot_nt(q, kc[hh])
            pair.append(_softmax_pv([s_win, s_ctx], [vw, vc[hh]]))
        o_ref[r * GRID_W:(r + 1) * GRID_W, :] = jnp.concatenate(pair, axis=-1).astype(o_ref.dtype)


def _window_bias(rpb_l):
    col = jnp.arange(GRID_W)
    col_start = jnp.clip(col - WIN_C // 2, 0, GRID_W - WIN_C)
    col_mask = (col[None, :] >= col_start[:, None]) & (col[None, :] < col_start[:, None] + WIN_C)
    dc = jnp.clip(col[None, :] - col[:, None], -(WIN_C - 1), WIN_C - 1) + (WIN_C - 1)
    dr = jnp.arange(WIN_R)[None, :] - jnp.arange(WIN_R)[:, None] + (WIN_R - 1)
    bias = rpb_l[:, dr[:, None, :, None], dc[None, :, None, :]]
    bias = jnp.where(col_mask[None, None, :, None, :], bias, NEG_INF)
    return bias.reshape(N_HEADS, WIN_R, GRID_W, WIN_R * GRID_W)


def _neighbourhood_attention(q, k, v, cache_k, cache_v, bias, slot):
    lat = lambda: pl.BlockSpec((1, 1, 2, DEC_SEQ, HEAD_DIM), lambda b, hp: (b, 0, hp, 0, 0))
    ctx = lambda: pl.BlockSpec((1, 1, 2, PAST_LEN, HEAD_DIM), lambda b, hp: (b, slot, hp, 0, 0))
    return pl.pallas_call(
        _nbr_attn_kernel,
        grid=(DEC_BATCH, N_HEADS // 2),
        in_specs=[lat(), lat(), lat(), ctx(), ctx(),
                  pl.BlockSpec((2, WIN_R, GRID_W, WIN_R * GRID_W), lambda b, hp: (hp, 0, 0, 0))],
        out_specs=pl.BlockSpec((DEC_SEQ, HEAD_PAIR_W), lambda b, hp: (b, hp)),
        out_shape=jax.ShapeDtypeStruct((T_LAT, D), BF16),
        compiler_params=_params(("arbitrary", "arbitrary")),
        name="nbr_attention",
    )(q, k, v, cache_k, cache_v, bias)


def _top2(logits):
    col = lax.broadcasted_iota(jnp.int32, logits.shape, 1)
    m1 = jnp.max(logits, axis=-1, keepdims=True)
    i1 = jnp.min(jnp.where(logits == m1, col, N_EXPERTS), axis=-1, keepdims=True)
    rest = jnp.where(col == i1, -jnp.inf, logits)
    m2 = jnp.max(rest, axis=-1, keepdims=True)
    i2 = jnp.min(jnp.where(rest == m2, col, N_EXPERTS), axis=-1, keepdims=True)
    e = jnp.exp(m2 - m1)
    return i1, i2, 1.0 / (1.0 + e), e / (1.0 + e)


CTX_TILES = T_CTX // TM


def _proj_res_norm_kernel(ac_ref, al_ref, w_ref, x_ref, g1_ref, ng_ref, sc_ref, sh_ref, *rest, router):
    if router:
        wr_ref, xo_ref, hn_ref, idx_ref, gate_ref, wb_ref = rest
    else:
        xo_ref, hn_ref, wb_ref = rest

    @pl.when(pl.program_id(0) == 0)
    def _():
        wb_ref[...] = w_ref[...].astype(BF16)

    a = jnp.where(pl.program_id(0) < CTX_TILES, ac_ref[...], al_ref[...])
    mix = jnp.dot(a, wb_ref[...], preferred_element_type=F32)
    x = x_ref[...] + g1_ref[0, 0, 0] * mix
    xo_ref[...] = x
    hn = _rms_mod(x, ng_ref[...], sc_ref[0, 0, 0], sh_ref[0, 0, 0])
    hn_ref[...] = hn.astype(hn_ref.dtype)
    if router:
        hi = hn.astype(BF16)
        lo = (hn - hi.astype(F32)).astype(BF16)
        l_hi = jnp.dot(hi, wr_ref[...], preferred_element_type=F32)
        l_lo = jnp.dot(lo, wr_ref[...], preferred_element_type=F32)
        logits = l_hi[:, :N_EXPERTS] + l_hi[:, N_EXPERTS:] + l_lo[:, :N_EXPERTS]
        i1, i2, p1, p2 = _top2(logits)
        first = lax.broadcasted_iota(jnp.int32, (logits.shape[0], 2), 1) == 0
        idx_ref[...] = jnp.where(first, i1, i2)
        gate_ref[...] = jnp.where(first, p1, p2)


def _proj_res_norm(a_ctx, a_lat, w, x, norm_g, mods, layer, w_router=None):
    router = w_router is not None
    row = lambda n: pl.BlockSpec((TM, n), lambda i: (i, 0))
    in_specs = [pl.BlockSpec((TM, D), lambda i: (jnp.minimum(i, CTX_TILES - 1), 0)),
                pl.BlockSpec((TM, D), lambda i: (jnp.maximum(i - CTX_TILES, 0), 0)),
                pl.BlockSpec((D, D), lambda i: (0, 0)), row(D),
                _mod_spec(layer, 2), pl.BlockSpec((1, D), lambda i: (0, 0)), _mod_spec(layer, 4), _mod_spec(layer, 3)]
    args = [a_ctx, a_lat, w, x, mods, norm_g.reshape(1, D), mods, mods]
    out_specs = [row(D), row(D)]
    out_shape = [jax.ShapeDtypeStruct((T, D), F32), jax.ShapeDtypeStruct((T, D), F32 if router else BF16)]
    if router:
        w_hi = w_router.astype(BF16)
        w_lo = (w_router - w_hi.astype(F32)).astype(BF16)
        in_specs.append(pl.BlockSpec((D, 2 * N_EXPERTS), lambda i: (0, 0)))
        args.append(jnp.concatenate([w_hi, w_lo], axis=1))
        out_specs += [row(2), row(2)]
        out_shape += [jax.ShapeDtypeStruct((T, 2), jnp.int32), jax.ShapeDtypeStruct((T, 2), F32)]
    return pl.pallas_call(
        functools.partial(_proj_res_norm_kernel, router=router),
        grid=(T // TM,),
        in_specs=in_specs,
        out_specs=out_specs,
        out_shape=out_shape,
        scratch_shapes=[pltpu.VMEM((D, D), BF16)],
        compiler_params=_params(("arbitrary",)),
        name="proj_res_norm",
    )(*args)


def _swiglu_chunk(hn, wg_ref, wu_ref, wd_ref, acc_ref):
    g = jnp.dot(hn, wg_ref[0].astype(BF16), preferred_element_type=F32)
    u = jnp.dot(hn, wu_ref[0].astype(BF16), preferred_element_type=F32)
    h = (g * jax.nn.sigmoid(g) * u).astype(BF16)
    acc_ref[...] += jnp.dot(h, wd_ref[0].astype(BF16), preferred_element_type=F32)


def _ffn_kernel(hn_ref, wg_ref, wu_ref, wd_ref, x_ref, g2_ref, ng_ref, sc_ref, sh_ref, xo_ref, hno_ref, acc_ref):
    @pl.when(pl.program_id(1) == 0)
    def _():
        acc_ref[...] = jnp.zeros_like(acc_ref)

    _swiglu_chunk(hn_ref[...], wg_ref, wu_ref, wd_ref, acc_ref)

    @pl.when(pl.program_id(1) == N_FCH - 1)
    def _():
        x = x_ref[...] + g2_ref[0, 0, 0] * acc_ref[...]
        xo_ref[...] = x
        hno_ref[...] = _rms_mod(x, ng_ref[...], sc_ref[0, 0, 0], sh_ref[0, 0, 0]).astype(hno_ref.dtype)


def _ffn(hn, wg, wu, wd, idx, x, mods, layer, next_g):
    row = lambda: pl.BlockSpec((TM, D), lambda i, f: (i, 0))
    return pl.pallas_call(
        _ffn_kernel,
        grid=(T // TM, N_FCH),
        in_specs=[row(),
                  pl.BlockSpec((1, D, TF), lambda i, f: (idx, 0, f)),
                  pl.BlockSpec((1, D, TF), lambda i, f: (idx, 0, f)),
                  pl.BlockSpec((1, TF, D), lambda i, f: (idx, f, 0)),
                  row(), _mod_spec(layer, 5, 2), pl.BlockSpec((1, D), lambda i, f: (0, 0)),
                  _mod_spec(layer + 1, 1, 2), _mod_spec(layer + 1, 0, 2)],
        out_specs=[row(), row()],
        out_shape=[jax.ShapeDtypeStruct((T, D), F32), jax.ShapeDtypeStruct((T, D), BF16)],
        scratch_shapes=[pltpu.VMEM((TM, D), F32)],
        compiler_params=_params(("arbitrary", "arbitrary")),
        name="ffn",
    )(hn, wg, wu, wd, x, mods, next_g.reshape(1, D), mods, mods)


ROW_UNROLL = 8
SCATTER_TM = 512
COMBINE_TM = 512


def _wait_rows(src_ref, dst_ref, sem, n_rows):
    pltpu.make_async_copy(src_ref.at[pl.ds(0, n_rows)], dst_ref.at[pl.ds(0, n_rows)], sem).wait()


def _scatter_kernel(dest_ref, hn_ref, xs_in_ref, xs_ref, sem):
    del xs_in_ref
    base = pl.program_id(0) * (2 * SCATTER_TM)

    def body(g, carry):
        for u in range(ROW_UNROLL):
            r = g * ROW_UNROLL + u
            for k in range(2):
                d = dest_ref[base + 2 * r + k]
                pltpu.make_async_copy(hn_ref.at[pl.ds(r, 1)], xs_ref.at[pl.ds(d, 1)], sem).start()
        return carry

    lax.fori_loop(0, SCATTER_TM // ROW_UNROLL, body, 0)
    for _ in range(2):
        _wait_rows(hn_ref, xs_ref, sem, SCATTER_TM)


def _moe_scatter(hn2, dest_flat):
    return pl.pallas_call(
        _scatter_kernel,
        grid_spec=pltpu.PrefetchScalarGridSpec(
            num_scalar_prefetch=1,
            grid=(T // SCATTER_TM,),
            in_specs=[pl.BlockSpec((SCATTER_TM, D), lambda j, dest: (j, 0)),
                      pl.BlockSpec(memory_space=pl.ANY)],
            out_specs=pl.BlockSpec(memory_space=pl.ANY),
            scratch_shapes=[pltpu.SemaphoreType.DMA],
        ),
        out_shape=jax.ShapeDtypeStruct((MOE_ROWS, D), F32),
        input_output_aliases={2: 0},
        compiler_params=_params(("arbitrary",)),
        name="moe_scatter",
    )(dest_flat, hn2, jnp.zeros((MOE_ROWS, D), F32))


def _moe_kernel(te_ref, nu_ref, xs_ref, wg_ref, wu_ref, wd_ref, y_ref, xsb_ref, acc_ref):
    i = pl.program_id(0)
    f = pl.program_id(1)

    @pl.when(i < nu_ref[0])
    def _():
        @pl.when(f == 0)
        def _():
            xsb_ref[...] = xs_ref[...].astype(BF16)
            acc_ref[...] = jnp.zeros_like(acc_ref)

        _swiglu_chunk(xsb_ref[...], wg_ref, wu_ref, wd_ref, acc_ref)

        @pl.when(f == N_FCH - 1)
        def _():
            y_ref[...] = acc_ref[...]

    @pl.when(jnp.logical_and(i >= nu_ref[0], f == N_FCH - 1))
    def _():
        y_ref[...] = jnp.zeros_like(y_ref)


def _moe_experts(xs, tile_expert, n_used, wg, wu, wd, idx):
    def wspec(shape, col):
        def imap(i, f, te, nu):
            fe = jnp.where(i < nu[0], f, N_FCH - 1)
            return (idx, te[i], 0, fe) if col else (idx, te[i], fe, 0)
        return pl.BlockSpec(shape, imap)

    def kern(te, nu, xs_ref, wg_ref, wu_ref, wd_ref, y_ref, xsb_ref, acc_ref):
        _moe_kernel(te, nu, xs_ref, wg_ref.at[0], wu_ref.at[0], wd_ref.at[0], y_ref, xsb_ref, acc_ref)

    return pl.pallas_call(
        kern,
        grid_spec=pltpu.PrefetchScalarGridSpec(
            num_scalar_prefetch=2,
            grid=(MOE_TILES, N_FCH),
            in_specs=[pl.BlockSpec((MOE_TM, D), lambda i, f, te, nu: (jnp.minimum(i, nu[0] - 1), 0)),
                      wspec((1, 1, D, TF), True), wspec((1, 1, D, TF), True), wspec((1, 1, TF, D), False)],
            out_specs=pl.BlockSpec((MOE_TM, D), lambda i, f, te, nu: (i, 0)),
            scratch_shapes=[pltpu.VMEM((MOE_TM, D), BF16), pltpu.VMEM((MOE_TM, D), F32)],
        ),
        out_shape=jax.ShapeDtypeStruct((MOE_ROWS, D), F32),
        compiler_params=_params(("arbitrary", "arbitrary")),
        name="moe_experts",
    )(tile_expert, n_used, xs, wg, wu, wd)


CTX_COMBINE_TILES = T_CTX // COMBINE_TM


def _combine_kernel(dest_ref, x_ref, y_hbm, gate_ref, g2_ref, ng_ref, sc_ref, sh_ref, *rest, final):
    *outs, ybuf, sem = rest
    j = pl.program_id(0)
    slot = j % 2

    def issue(tile, s):
        base = tile * (2 * COMBINE_TM)

        def body(g, carry):
            for u in range(ROW_UNROLL):
                r = g * ROW_UNROLL + u
                for k in range(2):
                    d = dest_ref[base + 2 * r + k]
                    pltpu.make_async_copy(y_hbm.at[pl.ds(d, 1)], ybuf.at[s, pl.ds(k * COMBINE_TM + r, 1)],
                                          sem.at[s]).start()
            return carry

        lax.fori_loop(0, COMBINE_TM // ROW_UNROLL, body, 0)

    @pl.when(j == 0)
    def _():
        issue(0, 0)

    @pl.when(j + 1 < pl.num_programs(0))
    def _():
        issue(j + 1, 1 - slot)

    _wait_rows(y_hbm, ybuf.at[slot], sem.at[slot], 2 * COMBINE_TM)
    gates = gate_ref[...]
    moe = gates[:, 0:1] * ybuf[slot, 0:COMBINE_TM] + gates[:, 1:2] * ybuf[slot, COMBINE_TM:2 * COMBINE_TM]
    x = x_ref[...] + g2_ref[0, 0, 0] * moe
    if final:
        yp_ref, ys_ref = outs
        y = _rms_mod(x, ng_ref[...], 0.0, 0.0)

        @pl.when(j < CTX_COMBINE_TILES)
        def _():
            yp_ref[...] = y

        @pl.when(j >= CTX_COMBINE_TILES)
        def _():
            ys_ref[...] = y
    else:
        xo_ref, hno_ref = outs
        xo_ref[...] = x
        hno_ref[...] = _rms_mod(x, ng_ref[...], sc_ref[0, 0, 0], sh_ref[0, 0, 0]).astype(hno_ref.dtype)


def _moe_combine(x, y, dest_flat, gates, mods, layer, next_g, final):
    row = lambda n=D: pl.BlockSpec((COMBINE_TM, n), lambda j, dest: (j, 0))
    vec = lambda: pl.BlockSpec((1, D), lambda j, dest: (0, 0))
    mod = lambda l, k: pl.BlockSpec((1, 1, 1, 1, D), lambda j, dest: (l, k, _cond_of_tile(j, COMBINE_TM), 0, 0))
    nl = layer if final else layer + 1
    if final:
        out_specs = [pl.BlockSpec((COMBINE_TM, D), lambda j, dest: (jnp.minimum(j, CTX_COMBINE_TILES - 1), 0)),
                     pl.BlockSpec((COMBINE_TM, D), lambda j, dest: (jnp.maximum(j - CTX_COMBINE_TILES, 0), 0))]
        out_shape = [jax.ShapeDtypeStruct((T_CTX, D), F32), jax.ShapeDtypeStruct((T_LAT, D), F32)]
    else:
        out_specs = [row(), row()]
        out_shape = [jax.ShapeDtypeStruct((T, D), F32), jax.ShapeDtypeStruct((T, D), BF16)]
    return pl.pallas_call(
        functools.partial(_combine_kernel, final=final),
        grid_spec=pltpu.PrefetchScalarGridSpec(
            num_scalar_prefetch=1,
            grid=(T // COMBINE_TM,),
            in_specs=[row(), pl.BlockSpec(memory_space=pl.ANY), row(2), mod(layer, 5), vec(), mod(nl, 1), mod(nl, 0)],
            out_specs=out_specs,
            scratch_shapes=[pltpu.VMEM((2, 2 * COMBINE_TM, D), F32), pltpu.SemaphoreType.DMA((2,))],
        ),
        out_shape=out_shape,
        compiler_params=_params(("arbitrary",)),
        name="moe_combine",
    )(dest_flat, x, y, gates, mods, next_g.reshape(1, D), mods, mods)


LRU_CHUNK = 256


def _shift_rows(v, s, row):
    n = v.shape[0]
    rolled = pltpu.roll(v, s % n, axis=0)
    keep = (row >= s) if s > 0 else (row < n + s)
    return jnp.where(keep, rolled, 0.0)


def _lru_kernel(proj_ref, h0_ref, cw_ref, cb_ref, wbd_ref, bbd_ref, lam_ref, scw_ref, scb_ref,
                y_ref, st_ref, xc_ref, a_ref, b_ref, *, seq_len):
    row = lax.broadcasted_iota(jnp.int32, (seq_len, 1), 0)
    x = proj_ref[:, 0:LRU_W]
    xc = (cb_ref[...] + _shift_rows(x, 2, row) * cw_ref[0:1] + _shift_rows(x, 1, row) * cw_ref[1:2]
          + x * cw_ref[2:3] + _shift_rows(x, -1, row) * cw_ref[3:4])
    xc_ref[...] = xc

    lam = lam_ref[...]
    neg_csp = -LRU_C * (jnp.maximum(-lam, 0.0) + jnp.log1p(jnp.exp(-jnp.abs(lam))))

    for c in range(seq_len // LRU_CHUNK):
        rows = pl.ds(c * LRU_CHUNK, LRU_CHUNK)
        xcc = xc_ref[rows, :]
        gates = jnp.dot(xcc.astype(BF16), wbd_ref[...], preferred_element_type=F32) + bbd_ref[...]
        for d in range(2):
            r = jax.nn.sigmoid(gates[:, (2 * d) * LRU_W:(2 * d + 1) * LRU_W])
            i = jax.nn.sigmoid(gates[:, (2 * d + 1) * LRU_W:(2 * d + 2) * LRU_W])
            log_a = neg_csp[d:d + 1] * r
            t = jnp.tanh(log_a)
            one_minus_a2 = -2.0 * t / (1.0 - t)
            a_ref[d, rows, :] = jnp.exp(log_a)
            b_ref[d, rows, :] = jnp.sqrt(one_minus_a2) * (i * xcc)

    n_grp = seq_len // 8

    def scan_body(g, carry):
        hf, hb = carry
        rf = pl.ds(pl.multiple_of(g * 8, 8), 8)
        rb = pl.ds(pl.multiple_of((n_grp - 1 - g) * 8, 8), 8)
        af, bf = a_ref[0, rf, :], b_ref[0, rf, :]
        ab, bb = a_ref[1, rb, :], b_ref[1, rb, :]
        outs_f, outs_b = [], []
        for j in range(8):
            hf = af[j:j + 1] * hf + bf[j:j + 1]
            outs_f.append(hf)
            hb = ab[7 - j:8 - j] * hb + bb[7 - j:8 - j]
            outs_b.append(hb)
        b_ref[0, rf, :] = jnp.concatenate(outs_f, axis=0)
        b_ref[1, rb, :] = jnp.concatenate(outs_b[::-1], axis=0)
        return hf, hb

    hf, hb = lax.fori_loop(0, n_grp, scan_body, (h0_ref[0, 0:1, :], h0_ref[0, 1:2, :]))
    st_ref[0, 0:1, :] = hf
    st_ref[0, 1:2, :] = hb

    h_sum = b_ref[0] + b_ref[1]
    y_ref[:, 0:LRU_W] = (h_sum * jax.nn.gelu(proj_ref[:, LRU_W:2 * LRU_W])).astype(y_ref.dtype)
    o = 2 * LRU_W
    cv = proj_ref[:, o + SC_W:o + 2 * SC_W] * proj_ref[:, o + 2 * SC_W:o + 3 * SC_W]
    conv = scb_ref[...] + _shift_rows(cv, 1, row) * scw_ref[0:1] + cv * scw_ref[1:2] + _shift_rows(cv, -1, row) * scw_ref[2:3]
    y_ref[:, LRU_W:LRU_W + SC_W] = (proj_ref[:, o:o + SC_W] * conv).astype(y_ref.dtype)


def _lru_mixer(proj, h0, cw, cb, wbd, bbd, lam, scw, scb, *, latent):
    if latent:
        n_seq, seq_len, blk0 = DEC_BATCH, DEC_SEQ, T_CTX // DEC_SEQ
    else:
        n_seq, seq_len, blk0 = BATCH, SEQ, 0
    full = lambda shape: pl.BlockSpec(shape, lambda b: (0,) * len(shape))
    return pl.pallas_call(
        functools.partial(_lru_kernel, seq_len=seq_len),
        grid=(n_seq,),
        in_specs=[pl.BlockSpec((seq_len, IN_W), lambda b: (blk0 + b, 0)),
                  pl.BlockSpec((1, 2, LRU_W), lambda b: (b, 0, 0)),
                  full((4, LRU_W)), full((1, LRU_W)), full((LRU_W, 4 * LRU_W)), full((1, 4 * LRU_W)),
                  full((2, LRU_W)), full((3, SC_W)), full((1, SC_W))],
        out_specs=[pl.BlockSpec((seq_len, D), lambda b: (b, 0)),
                   pl.BlockSpec((1, 2, LRU_W), lambda b: (b, 0, 0))],
        out_shape=[jax.ShapeDtypeStruct((n_seq * seq_len, D), BF16),
                   jax.ShapeDtypeStruct((n_seq, 2, LRU_W), F32)],
        scratch_shapes=[pltpu.VMEM((seq_len, LRU_W), F32),
                        pltpu.VMEM((2, seq_len, LRU_W), F32),
                        pltpu.VMEM((2, seq_len, LRU_W), F32)],
        compiler_params=_params(("arbitrary",)),
        name="lru_mixer",
    )(proj, h0, cw, cb.reshape(1, LRU_W), wbd, bbd, lam, scw, scb.reshape(1, SC_W))


def _block_diag_dense(w):
    eye = jnp.eye(LRU_HEADS, dtype=w.dtype)
    return jnp.einsum('hij,hg->higj', w, eye).reshape(LRU_W, LRU_W)


def _route(idx):
    e_flat = idx.reshape(-1)
    onehot = (e_flat[:, None] == jnp.arange(N_EXPERTS)[None, :]).astype(jnp.int32)
    csum = jnp.cumsum(onehot, axis=0)
    counts = csum[-1]
    rank = jnp.take_along_axis(csum, e_flat[:, None], axis=1)[:, 0] - 1
    padded = (counts + MOE_TM - 1) // MOE_TM * MOE_TM
    ends = jnp.cumsum(padded)
    dest = ((ends - padded)[e_flat] + rank).astype(jnp.int32)
    tile_end = ends // MOE_TM
    n_used = tile_end[-1]
    tiles = jnp.minimum(jnp.arange(MOE_TILES), n_used - 1)
    tile_expert = jnp.sum(tiles[:, None] >= tile_end[None, :], axis=1).astype(jnp.int32)
    return dest, tile_expert, n_used.reshape(1).astype(jnp.int32)


def kernel(x_prompt, x_sample, state_lru, cache_k, cache_v, c, c_ctx, norm1_g, norm2_g, w_mod, b_mod, w_in, lru_conv_w, lru_conv_b, lru_wa, lru_ba, lru_wx, lru_bx, lru_lam, sc_conv_w, sc_conv_b, w_out, ffn_w_gate, ffn_w_up, ffn_w_down, w_qkv, w_o, rpb, w_router, moe_w_gate, moe_w_up, moe_w_down, final_g):
    x = jnp.concatenate([x_prompt.reshape(T_CTX, D), x_sample.reshape(T_LAT, D)], axis=0)
    cond = jnp.concatenate([c_ctx[None, :], c, jnp.zeros((COND_ROWS - N_COND, D), F32)], axis=0)
    mods = _modulation(cond, w_mod, b_mod).reshape(DEPTH, 6, COND_ROWS, 1, D)

    hn = _first_norm(x, norm1_g[0], mods, 0)
    zero_state = jnp.zeros((BATCH, 2, LRU_W), F32)
    states = []
    new_k = jnp.zeros((BATCH, DEPTH // 2, N_HEADS, HEAD_DIM, SEQ), F32)
    new_v = jnp.zeros((BATCH, DEPTH // 2, N_HEADS, HEAD_DIM, SEQ), F32)
    for l in range(DEPTH):
        idx = l // 2
        if l % 2 == 0:
            proj = _matmul(hn, w_in[idx], IN_W // 2, F32)
            wbd = jnp.concatenate([_block_diag_dense(lru_wa[idx, 0]), _block_diag_dense(lru_wx[idx, 0]),
                                   _block_diag_dense(lru_wa[idx, 1]), _block_diag_dense(lru_wx[idx, 1])],
                                  axis=1).astype(BF16)
            bbd = jnp.concatenate([lru_ba[idx, 0], lru_bx[idx, 0], lru_ba[idx, 1], lru_bx[idx, 1]]).reshape(1, 4 * LRU_W)
            lru_args = (lru_conv_w[idx], lru_conv_b[idx], wbd, bbd, lru_lam[idx], sc_conv_w[idx], sc_conv_b[idx])
            y_ctx, st = _lru_mixer(proj, zero_state, *lru_args, latent=False)
            y_lat, _ = _lru_mixer(proj, state_lru[:, idx], *lru_args, latent=True)
            states.append(st)
            x, hn2 = _proj_res_norm(y_ctx, y_lat, w_out[idx], x, norm2_g[l], mods, l)
            x, hn = _ffn(hn2, ffn_w_gate, ffn_w_up, ffn_w_down, idx, x, mods, l, norm1_g[l + 1])
        else:
            wq = w_qkv[idx]
            q_scale = HEAD_DIM ** -0.5
            q_ctx = _matmul_heads(hn, wq, 0, latent=False, out_dtype=BF16, scale=q_scale)
            new_k = _matmul_heads(hn, wq, 1, latent=False, out_dtype=F32, into=new_k, slot=idx, n_slots=2,
                                  transposed=True)
            new_v = _matmul_heads(hn, wq, 2, latent=False, out_dtype=F32, into=new_v, slot=idx, n_slots=2,
                                  transposed=True)
            q_lat = _matmul_heads(hn, wq, 0, latent=True, out_dtype=BF16, scale=q_scale)
            k_lat = _matmul_heads(hn, wq, 1, latent=True, out_dtype=F32)
            v_lat = _matmul_heads(hn, wq, 2, latent=True, out_dtype=F32)
            o_ctx = _context_attention(q_ctx, new_k, new_v, idx)
            o_lat = _neighbourhood_attention(q_lat, k_lat, v_lat, cache_k, cache_v, _window_bias(rpb[idx]), idx)
            x, hn2, top_i, top_p = _proj_res_norm(o_ctx, o_lat,w_o[idx], x, norm2_g[l], mods, l, w_router[idx])
            dest, tile_expert, n_used = _route(top_i)
            xs = _moe_scatter(hn2, dest)
            y = _moe_experts(xs, tile_expert, n_used, moe_w_gate, moe_w_up, moe_w_down, idx)
            if l == DEPTH - 1:
                y_prompt, y_sample = _moe_combine(x, y, dest, top_p, mods, l, final_g, True)
            else:
                x, hn = _moe_combine(x, y, dest, top_p, mods, l, norm1_g[l + 1], False)

    return (y_prompt.reshape(BATCH, SEQ, D), y_sample.reshape(DEC_BATCH, DEC_SEQ, D), jnp.stack(states, axis=1),
            jnp.swapaxes(new_k, -1, -2), jnp.swapaxes(new_v, -1, -2))
```

```python
import functools

import jax
import jax.numpy as jnp
import numpy as np
from jax import lax
from jax.experimental import pallas as pl
from jax.experimental.pallas import tpu as pltpu

F32 = jnp.float32
BF16 = jnp.bfloat16

D = 1024
BATCH = 32
SEQ = 256
DEPTH = 4
DEC_BATCH = 2
DEC_SEQ = 1024
PAST_LEN = 256
GRID_W = 64
GRID_ROWS = DEC_SEQ // GRID_W
LRU_W = 512
LRU_HEADS = 8
LRU_BLOCK = LRU_W // LRU_HEADS
LRU_C = 8.0
SC_W = 512
IN_W = 2 * LRU_W + 3 * SC_W
N_HEADS = 16
HEAD_DIM = D // N_HEADS
WIN_R = 8
WIN_C = 16
D_FF = 2816
N_EXPERTS = 8
EPS = 1e-6
NEG_INF = -1e30

T_CTX = BATCH * SEQ
T_LAT = DEC_BATCH * DEC_SEQ
T = T_CTX + T_LAT
N_COND = 3
COND_ROWS = 8

TM = 1024
TF = 256
N_FCH = D_FF // TF
MOE_TM = 1024
MOE_TILES = (2 * T + N_EXPERTS * (MOE_TM - 1)) // MOE_TM
MOE_ROWS = MOE_TILES * MOE_TM
VMEM_LIMIT = 52 * 1024 * 1024


def _params(sem):
    return pltpu.CompilerParams(dimension_semantics=sem, vmem_limit_bytes=VMEM_LIMIT)


def _cond_of_tile(i, tm=TM):
    r = i * tm
    return jnp.where(r < T_CTX, 0, 1 + (r - T_CTX) // DEC_SEQ)


def _mod_spec(layer, k, grid_rank=1, tm=TM):
    if grid_rank == 1:
        return pl.BlockSpec((1, 1, 1, 1, D), lambda i: (layer, k, _cond_of_tile(i, tm), 0, 0))
    return pl.BlockSpec((1, 1, 1, 1, D), lambda i, f, *_: (layer, k, _cond_of_tile(i, tm), 0, 0))


def _rms_mod(x, g, scale, shift):
    y = x * lax.rsqrt(jnp.mean(x * x, axis=-1, keepdims=True) + EPS)
    return (y * g) * (1.0 + scale) + shift


def _mod_kernel(c_ref, w_ref, b_ref, o_ref):
    c = c_ref[...]
    s = c * jax.nn.sigmoid(c)
    o_ref[0, 0] = jnp.dot(s.astype(BF16), w_ref[0].astype(BF16), preferred_element_type=F32) + b_ref[0, 0]


def _modulation(cond, w_mod, b_mod):
    return pl.pallas_call(
        _mod_kernel,
        grid=(DEPTH, 6),
        in_specs=[
            pl.BlockSpec((COND_ROWS, D), lambda l, k: (0, 0)),
            pl.BlockSpec((1, D, D), lambda l, k: (l, 0, k)),
            pl.BlockSpec((1, 1, 1, D), lambda l, k: (l, k, 0, 0)),
        ],
        out_specs=pl.BlockSpec((1, 1, COND_ROWS, D), lambda l, k: (l, k, 0, 0)),
        out_shape=jax.ShapeDtypeStruct((DEPTH, 6, COND_ROWS, D), F32),
        compiler_params=_params(("arbitrary", "arbitrary")),
        name="adaln_mod",
    )(cond, w_mod, b_mod.reshape(DEPTH, 6, 1, D))


def _norm_kernel(x_ref, g_ref, sc_ref, sh_ref, o_ref):
    o_ref[...] = _rms_mod(x_ref[...], g_ref[...], sc_ref[0, 0, 0], sh_ref[0, 0, 0]).astype(o_ref.dtype)


def _first_norm(x, g, mods, layer):
    return pl.pallas_call(
        _norm_kernel,
        grid=(T // TM,),
        in_specs=[
            pl.BlockSpec((TM, D), lambda i: (i, 0)),
            pl.BlockSpec((1, D), lambda i: (0, 0)),
            _mod_spec(layer, 1),
            _mod_spec(layer, 0),
        ],
        out_specs=pl.BlockSpec((TM, D), lambda i: (i, 0)),
        out_shape=jax.ShapeDtypeStruct((T, D), BF16),
        compiler_params=_params(("arbitrary",)),
        name="first_norm",
    )(x, g.reshape(1, D), mods, mods)


def _mm_kernel(a_ref, w_ref, o_ref, wb_ref):
    @pl.when(pl.program_id(1) == 0)
    def _():
        wb_ref[...] = w_ref[...].astype(BF16)

    o_ref[...] = jnp.dot(a_ref[...], wb_ref[...], preferred_element_type=F32).astype(o_ref.dtype)


def _matmul(a, w, tn, out_dtype):
    m, k = a.shape
    n = w.shape[1]
    return pl.pallas_call(
        _mm_kernel,
        grid=(n // tn, m // TM),
        in_specs=[
            pl.BlockSpec((TM, k), lambda j, i: (i, 0)),
            pl.BlockSpec((k, tn), lambda j, i: (0, j)),
        ],
        out_specs=pl.BlockSpec((TM, tn), lambda j, i: (i, j)),
        out_shape=jax.ShapeDtypeStruct((m, n), out_dtype),
        scratch_shapes=[pltpu.VMEM((k, tn), BF16)],
        compiler_params=_params(("arbitrary", "arbitrary")),
        name="matmul",
    )(a, w)


def _mm_heads_kernel(a_ref, w_ref, o_ref, wb_ref, acc_ref, *, n_seq, seq_len, scale, transposed):
    @pl.when(pl.program_id(0) == 0)
    def _():
        wb_ref[...] = w_ref[...].astype(BF16)

    acc = jnp.dot(a_ref[...], wb_ref[...], preferred_element_type=F32)
    if scale != 1.0:
        acc = acc * scale
    if transposed:
        acc_ref[...] = acc
        acc_t = acc_ref[...].T
        for s in range(n_seq):
            for h in range(N_HEADS):
                o_ref[s, 0, h] = acc_t[h * HEAD_DIM:(h + 1) * HEAD_DIM,
                                       s * seq_len:(s + 1) * seq_len].astype(o_ref.dtype)
    else:
        for s in range(n_seq):
            for h in range(N_HEADS):
                o_ref[s, 0, h] = acc[s * seq_len:(s + 1) * seq_len,
                                     h * HEAD_DIM:(h + 1) * HEAD_DIM].astype(o_ref.dtype)


def _matmul_heads(hn, w_qkv_l, part, *, latent, out_dtype, scale=1.0, into=None, slot=0, n_slots=1,
                  transposed=False):
    if latent:
        n_batch, seq_len, row0 = DEC_BATCH, DEC_SEQ, T_CTX // TM
    else:
        n_batch, seq_len, row0 = BATCH, SEQ, 0
    n_seq = TM // seq_len
    tail = (HEAD_DIM, seq_len) if transposed else (seq_len, HEAD_DIM)
    kern = functools.partial(_mm_heads_kernel, n_seq=n_seq, seq_len=seq_len, scale=scale, transposed=transposed)
    in_specs = [
        pl.BlockSpec((TM, D), lambda i: (row0 + i, 0)),
        pl.BlockSpec((D, D), lambda i: (0, part)),
    ]
    args = [hn, w_qkv_l]
    aliases = {}
    if into is not None:
        in_specs.append(pl.BlockSpec(memory_space=pl.ANY))
        args.append(into)
        aliases = {2: 0}
        body = lambda a, w, _, o, wb, acc: kern(a, w, o, wb, acc)
    else:
        body = kern
    return pl.pallas_call(
        body,
        grid=(n_batch // n_seq,),
        in_specs=in_specs,
        out_specs=pl.BlockSpec((n_seq, 1, N_HEADS) + tail, lambda i: (i, slot, 0, 0, 0)),
        out_shape=jax.ShapeDtypeStruct((n_batch, n_slots, N_HEADS) + tail, out_dtype),
        scratch_shapes=[pltpu.VMEM((D, D), BF16), pltpu.VMEM((TM, D), F32)],
        input_output_aliases=aliases,
        compiler_params=_params(("arbitrary",)),
        name="qkv_heads",
    )(*args)


def _dot_nt(a, b):
    return lax.dot_general(a, b, (((1,), (1,)), ((), ())), preferred_element_type=F32)


def _softmax_pv(s_list, v_list, v_transposed=False):
    m = s_list[0].max(axis=-1, keepdims=True)
    for s in s_list[1:]:
        m = jnp.maximum(m, s.max(axis=-1, keepdims=True))
    den = None
    out = None
    for s, v in zip(s_list, v_list):
        p = jnp.exp(s - m)
        d = p.sum(axis=-1, keepdims=True)
        pb = p.astype(BF16)
        o = _dot_nt(pb, v) if v_transposed else jnp.dot(pb, v, preferred_element_type=F32)
        den = d if den is None else den + d
        out = o if out is None else out + o
    return out / den


HEAD_PAIR_W = 2 * HEAD_DIM


def _ctx_attn_kernel(q_ref, kt_ref, vt_ref, o_ref):
    for hp in range(N_HEADS // 2):
        pair = []
        for h in (2 * hp, 2 * hp + 1):
            s = jnp.dot(q_ref[0, 0, h], kt_ref[0, 0, h].astype(BF16), preferred_element_type=F32)
            pair.append(_softmax_pv([s], [vt_ref[0, 0, h].astype(BF16)], v_transposed=True))
        o_ref[:, hp * HEAD_PAIR_W:(hp + 1) * HEAD_PAIR_W] = jnp.concatenate(pair, axis=-1).astype(o_ref.dtype)


def _context_attention(q, kt, vt, slot):
    kv_spec = pl.BlockSpec((1, 1, N_HEADS, HEAD_DIM, SEQ), lambda b: (b, slot, 0, 0, 0))
    return pl.pallas_call(
        _ctx_attn_kernel,
        grid=(BATCH,),
        in_specs=[pl.BlockSpec((1, 1, N_HEADS, SEQ, HEAD_DIM), lambda b: (b, 0, 0, 0, 0)), kv_spec, kv_spec],
        out_specs=pl.BlockSpec((SEQ, D), lambda b: (b, 0)),
        out_shape=jax.ShapeDtypeStruct((T_CTX, D), BF16),
        compiler_params=_params(("arbitrary",)),
        name="ctx_attention",
    )(q, kt, vt)


def _row_start(r):
    return min(max(r - WIN_R // 2, 0), GRID_ROWS - WIN_R)


def _nbr_attn_kernel(q_ref, k_ref, v_ref, kc_ref, vc_ref, bias_ref, o_ref):
    kc = [kc_ref[0, 0, hh].astype(BF16) for hh in range(2)]
    vc = [vc_ref[0, 0, hh].astype(BF16) for hh in range(2)]
    for r0, r1 in _ROW_GROUPS:
        rs = _row_start(r0)
        n_q = (r1 - r0) * GRID_W
        pair = []
        for hh in range(2):
            q = q_ref[0, 0, hh, r0 * GRID_W:r1 * GRID_W, :]
            kw = k_ref[0, 0, hh, rs * GRID_W:(rs + WIN_R) * GRID_W, :].astype(BF16)
            vw = v_ref[0, 0, hh, rs * GRID_W:(rs + WIN_R) * GRID_W, :].astype(BF16)
            bias = bias_ref[hh, r0 - rs:r1 - rs].reshape(n_q, WIN_R * GRID_W)
            s_win = _dot_nt(q, kw) + bias
            s_ctx = _dot_nt(q, kc[hh])
            pair.append(_softmax_pv([s_win, s_ctx], [vw, vc[hh]]))
        o_ref[r0 * GRID_W:r1 * GRID_W, :] = jnp.concatenate(pair, axis=-1).astype(o_ref.dtype)


def _row_groups():
    groups, r0 = [], 0
    for r in range(1, GRID_ROWS + 1):
        if r == GRID_ROWS or _row_start(r) != _row_start(r0):
            groups.append((r0, r))
            r0 = r
    return groups


_ROW_GROUPS = _row_groups()


def _window_bias(rpb_l):
    col = np.arange(GRID_W)
    col_start = np.clip(col - WIN_C // 2, 0, GRID_W - WIN_C)
    col_mask = (col[None, :] >= col_start[:, None]) & (col[None, :] < col_start[:, None] + WIN_C)
    dc = np.clip(col[None, :] - col[:, None], -(WIN_C - 1), WIN_C - 1) + (WIN_C - 1)
    onehot = (dc[None, :, :] == np.arange(2 * WIN_C - 1)[:, None, None]).astype(np.float32)
    rows = jnp.stack([rpb_l[:, WIN_R - 1 - v:2 * WIN_R - 1 - v, :] for v in range(WIN_R)], axis=1)
    bias = jnp.einsum('hvkd,dqc->hvqkc', rows, jnp.asarray(onehot), precision=lax.Precision.HIGHEST)
    bias = jnp.where(jnp.asarray(col_mask)[None, None, :, None, :], bias, NEG_INF)
    return bias.reshape(N_HEADS, WIN_R, GRID_W, WIN_R * GRID_W)


def _neighbourhood_attention(q, k, v, cache_k, cache_v, bias, slot):
    lat = lambda: pl.BlockSpec((1, 1, 2, DEC_SEQ, HEAD_DIM), lambda b, hp: (b, 0, hp, 0, 0))
    ctx = lambda: pl.BlockSpec((1, 1, 2, PAST_LEN, HEAD_DIM), lambda b, hp: (b, slot, hp, 0, 0))
    return pl.pallas_call(
        _nbr_attn_kernel,
        grid=(DEC_BATCH, N_HEADS // 2),
        in_specs=[lat(), lat(), lat(), ctx(), ctx(),
                  pl.BlockSpec((2, WIN_R, GRID_W, WIN_R * GRID_W), lambda b, hp: (hp, 0, 0, 0))],
        out_specs=pl.BlockSpec((DEC_SEQ, HEAD_PAIR_W), lambda b, hp: (b, hp)),
        out_shape=jax.ShapeDtypeStruct((T_LAT, D), BF16),
        compiler_params=_params(("arbitrary", "arbitrary")),
        name="nbr_attention",
    )(q, k, v, cache_k, cache_v, bias)


def _top2(logits):
    col = lax.broadcasted_iota(jnp.int32, logits.shape, 1)
    m1 = jnp.max(logits, axis=-1, keepdims=True)
    i1 = jnp.min(jnp.where(logits == m1, col, N_EXPERTS), axis=-1, keepdims=True)
    rest = jnp.where(col == i1, -jnp.inf, logits)
    m2 = jnp.max(rest, axis=-1, keepdims=True)
    i2 = jnp.min(jnp.where(rest == m2, col, N_EXPERTS), axis=-1, keepdims=True)
    e = jnp.exp(m2 - m1)
    return i1, i2, 1.0 / (1.0 + e), e / (1.0 + e)


CTX_TILES = T_CTX // TM


def _proj_res_norm_kernel(ac_ref, al_ref, w_ref, x_ref, g1_ref, ng_ref, sc_ref, sh_ref, *rest, router):
    if router:
        wr_ref, xo_ref, hn_ref, idx_ref, gate_ref, wb_ref = rest
    else:
        xo_ref, hn_ref, wb_ref = rest

    @pl.when(pl.program_id(0) == 0)
    def _():
        wb_ref[...] = w_ref[...].astype(BF16)

    a = jnp.where(pl.program_id(0) < CTX_TILES, ac_ref[...], al_ref[...])
    mix = jnp.dot(a, wb_ref[...], preferred_element_type=F32)
    x = x_ref[...] + g1_ref[0, 0, 0] * mix
    xo_ref[...] = x
    hn = _rms_mod(x, ng_ref[...], sc_ref[0, 0, 0], sh_ref[0, 0, 0])
    hn_ref[...] = hn.astype(hn_ref.dtype)
    if router:
        hi = hn.astype(BF16)
        lo = (hn - hi.astype(F32)).astype(BF16)
        l_hi = jnp.dot(hi, wr_ref[...], preferred_element_type=F32)
        l_lo = jnp.dot(lo, wr_ref[...], preferred_element_type=F32)
        logits = l_hi[:, :N_EXPERTS] + l_hi[:, N_EXPERTS:] + l_lo[:, :N_EXPERTS]
        i1, i2, p1, p2 = _top2(logits)
        first = lax.broadcasted_iota(jnp.int32, (logits.shape[0], 2), 1) == 0
        idx_ref[...] = jnp.where(first, i1, i2)
        gate_ref[...] = jnp.where(first, p1, p2)


def _proj_res_norm(a_ctx, a_lat, w, x, norm_g, mods, layer, w_router=None):
    router = w_router is not None
    row = lambda n: pl.BlockSpec((TM, n), lambda i: (i, 0))
    in_specs = [pl.BlockSpec((TM, D), lambda i: (jnp.minimum(i, CTX_TILES - 1), 0)),
                pl.BlockSpec((TM, D), lambda i: (jnp.maximum(i - CTX_TILES, 0), 0)),
                pl.BlockSpec((D, D), lambda i: (0, 0)), row(D),
                _mod_spec(layer, 2), pl.BlockSpec((1, D), lambda i: (0, 0)), _mod_spec(layer, 4), _mod_spec(layer, 3)]
    args = [a_ctx, a_lat, w, x, mods, norm_g.reshape(1, D), mods, mods]
    out_specs = [row(D), row(D)]
    out_shape = [jax.ShapeDtypeStruct((T, D), F32), jax.ShapeDtypeStruct((T, D), F32 if router else BF16)]
    if router:
        w_hi = w_router.astype(BF16)
        w_lo = (w_router - w_hi.astype(F32)).astype(BF16)
        in_specs.append(pl.BlockSpec((D, 2 * N_EXPERTS), lambda i: (0, 0)))
        args.append(jnp.concatenate([w_hi, w_lo], axis=1))
        out_specs += [row(2), row(2)]
        out_shape += [jax.ShapeDtypeStruct((T, 2), jnp.int32), jax.ShapeDtypeStruct((T, 2), F32)]
    return pl.pallas_call(
        functools.partial(_proj_res_norm_kernel, router=router),
        grid=(T // TM,),
        in_specs=in_specs,
        out_specs=out_specs,
        out_shape=out_shape,
        scratch_shapes=[pltpu.VMEM((D, D), BF16)],
        compiler_params=_params(("arbitrary",)),
        name="proj_res_norm",
    )(*args)


def _swiglu_chunk(hn, wg_ref, wu_ref, wd_ref, acc_ref):
    g = jnp.dot(hn, wg_ref[0].astype(BF16), preferred_element_type=F32)
    u = jnp.dot(hn, wu_ref[0].astype(BF16), preferred_element_type=F32)
    h = (g * jax.nn.sigmoid(g) * u).astype(BF16)
    acc_ref[...] += jnp.dot(h, wd_ref[0].astype(BF16), preferred_element_type=F32)


def _ffn_kernel(hn_ref, wg_ref, wu_ref, wd_ref, x_ref, g2_ref, ng_ref, sc_ref, sh_ref, xo_ref, hno_ref, acc_ref):
    @pl.when(pl.program_id(1) == 0)
    def _():
        acc_ref[...] = jnp.zeros_like(acc_ref)

    _swiglu_chunk(hn_ref[...], wg_ref, wu_ref, wd_ref, acc_ref)

    @pl.when(pl.program_id(1) == N_FCH - 1)
    def _():
        x = x_ref[...] + g2_ref[0, 0, 0] * acc_ref[...]
        xo_ref[...] = x
        hno_ref[...] = _rms_mod(x, ng_ref[...], sc_ref[0, 0, 0], sh_ref[0, 0, 0]).astype(hno_ref.dtype)


def _ffn(hn, wg, wu, wd, idx, x, mods, layer, next_g):
    row = lambda: pl.BlockSpec((TM, D), lambda i, f: (i, 0))
    return pl.pallas_call(
        _ffn_kernel,
        grid=(T // TM, N_FCH),
        in_specs=[row(),
                  pl.BlockSpec((1, D, TF), lambda i, f: (idx, 0, f)),
                  pl.BlockSpec((1, D, TF), lambda i, f: (idx, 0, f)),
                  pl.BlockSpec((1, TF, D), lambda i, f: (idx, f, 0)),
                  row(), _mod_spec(layer, 5, 2), pl.BlockSpec((1, D), lambda i, f: (0, 0)),
                  _mod_spec(layer + 1, 1, 2), _mod_spec(layer + 1, 0, 2)],
        out_specs=[row(), row()],
        out_shape=[jax.ShapeDtypeStruct((T, D), F32), jax.ShapeDtypeStruct((T, D), BF16)],
        scratch_shapes=[pltpu.VMEM((TM, D), F32)],
        compiler_params=_params(("arbitrary", "arbitrary")),
        name="ffn",
    )(hn, wg, wu, wd, x, mods, next_g.reshape(1, D), mods, mods)


ROW_UNROLL = 8
SCATTER_TM = 512
COMBINE_TM = 512


def _wait_rows(src_ref, dst_ref, sem, n_rows):
    pltpu.make_async_copy(src_ref.at[pl.ds(0, n_rows)], dst_ref.at[pl.ds(0, n_rows)], sem).wait()


def _scatter_kernel(dest_ref, hn_ref, xs_in_ref, xs_ref, sem):
    del xs_in_ref
    base = pl.program_id(0) * (2 * SCATTER_TM)

    def body(g, carry):
        for u in range(ROW_UNROLL):
            r = g * ROW_UNROLL + u
            for k in range(2):
                d = dest_ref[base + 2 * r + k]
                pltpu.make_async_copy(hn_ref.at[pl.ds(r, 1)], xs_ref.at[pl.ds(d, 1)], sem).start()
        return carry

    lax.fori_loop(0, SCATTER_TM // ROW_UNROLL, body, 0)
    for _ in range(2):
        _wait_rows(hn_ref, xs_ref, sem, SCATTER_TM)


def _moe_scatter(hn2, dest_flat):
    return pl.pallas_call(
        _scatter_kernel,
        grid_spec=pltpu.PrefetchScalarGridSpec(
            num_scalar_prefetch=1,
            grid=(T // SCATTER_TM,),
            in_specs=[pl.BlockSpec((SCATTER_TM, D), lambda j, dest: (j, 0)),
                      pl.BlockSpec(memory_space=pl.ANY)],
            out_specs=pl.BlockSpec(memory_space=pl.ANY),
            scratch_shapes=[pltpu.SemaphoreType.DMA],
        ),
        out_shape=jax.ShapeDtypeStruct((MOE_ROWS, D), F32),
        input_output_aliases={2: 0},
        compiler_params=_params(("arbitrary",)),
        name="moe_scatter",
    )(dest_flat, hn2, jnp.zeros((MOE_ROWS, D), F32))


def _moe_kernel(te_ref, nu_ref, xs_ref, wg_ref, wu_ref, wd_ref, y_ref, xsb_ref, acc_ref):
    i = pl.program_id(0)
    f = pl.program_id(1)

    @pl.when(i < nu_ref[0])
    def _():
        @pl.when(f == 0)
        def _():
            xsb_ref[...] = xs_ref[...].astype(BF16)
            acc_ref[...] = jnp.zeros_like(acc_ref)

        _swiglu_chunk(xsb_ref[...], wg_ref, wu_ref, wd_ref, acc_ref)

        @pl.when(f == N_FCH - 1)
        def _():
            y_ref[...] = acc_ref[...]

    @pl.when(jnp.logical_and(i >= nu_ref[0], f == N_FCH - 1))
    def _():
        y_ref[...] = jnp.zeros_like(y_ref)


def _moe_experts(xs, tile_expert, n_used, wg, wu, wd, idx):
    def wspec(shape, col):
        def imap(i, f, te, nu):
            fe = jnp.where(i < nu[0], f, N_FCH - 1)
            return (idx, te[i], 0, fe) if col else (idx, te[i], fe, 0)
        return pl.BlockSpec(shape, imap)

    def kern(te, nu, xs_ref, wg_ref, wu_ref, wd_ref, y_ref, xsb_ref, acc_ref):
        _moe_kernel(te, nu, xs_ref, wg_ref.at[0], wu_ref.at[0], wd_ref.at[0], y_ref, xsb_ref, acc_ref)

    return pl.pallas_call(
        kern,
        grid_spec=pltpu.PrefetchScalarGridSpec(
            num_scalar_prefetch=2,
            grid=(MOE_TILES, N_FCH),
            in_specs=[pl.BlockSpec((MOE_TM, D), lambda i, f, te, nu: (jnp.minimum(i, nu[0] - 1), 0)),
                      wspec((1, 1, D, TF), True), wspec((1, 1, D, TF), True), wspec((1, 1, TF, D), False)],
            out_specs=pl.BlockSpec((MOE_TM, D), lambda i, f, te, nu: (i, 0)),
            scratch_shapes=[pltpu.VMEM((MOE_TM, D), BF16), pltpu.VMEM((MOE_TM, D), F32)],
        ),
        out_shape=jax.ShapeDtypeStruct((MOE_ROWS, D), F32),
        compiler_params=_params(("arbitrary", "arbitrary")),
        name="moe_experts",
    )(tile_expert, n_used, xs, wg, wu, wd)


CTX_COMBINE_TILES = T_CTX // COMBINE_TM


def _combine_kernel(dest_ref, x_ref, y_hbm, gate_ref, g2_ref, ng_ref, sc_ref, sh_ref, *rest, final):
    *outs, ybuf, sem = rest
    j = pl.program_id(0)
    slot = j % 2

    def issue(tile, s):
        base = tile * (2 * COMBINE_TM)

        def body(g, carry):
            for u in range(ROW_UNROLL):
                r = g * ROW_UNROLL + u
                for k in range(2):
                    d = dest_ref[base + 2 * r + k]
                    pltpu.make_async_copy(y_hbm.at[pl.ds(d, 1)], ybuf.at[s, pl.ds(k * COMBINE_TM + r, 1)],
                                          sem.at[s]).start()
            return carry

        lax.fori_loop(0, COMBINE_TM // ROW_UNROLL, body, 0)

    @pl.when(j == 0)
    def _():
        issue(0, 0)

    @pl.when(j + 1 < pl.num_programs(0))
    def _():
        issue(j + 1, 1 - slot)

    _wait_rows(y_hbm, ybuf.at[slot], sem.at[slot], 2 * COMBINE_TM)
    gates = gate_ref[...]
    moe = gates[:, 0:1] * ybuf[slot, 0:COMBINE_TM] + gates[:, 1:2] * ybuf[slot, COMBINE_TM:2 * COMBINE_TM]
    x = x_ref[...] + g2_ref[0, 0, 0] * moe
    if final:
        yp_ref, ys_ref = outs
        y = _rms_mod(x, ng_ref[...], 0.0, 0.0)

        @pl.when(j < CTX_COMBINE_TILES)
        def _():
            yp_ref[...] = y

        @pl.when(j >= CTX_COMBINE_TILES)
        def _():
            ys_ref[...] = y
    else:
        xo_ref, hno_ref = outs
        xo_ref[...] = x
        hno_ref[...] = _rms_mod(x, ng_ref[...], sc_ref[0, 0, 0], sh_ref[0, 0, 0]).astype(hno_ref.dtype)


def _moe_combine(x, y, dest_flat, gates, mods, layer, next_g, final):
    row = lambda n=D: pl.BlockSpec((COMBINE_TM, n), lambda j, dest: (j, 0))
    vec = lambda: pl.BlockSpec((1, D), lambda j, dest: (0, 0))
    mod = lambda l, k: pl.BlockSpec((1, 1, 1, 1, D), lambda j, dest: (l, k, _cond_of_tile(j, COMBINE_TM), 0, 0))
    nl = layer if final else layer + 1
    if final:
        out_specs = [pl.BlockSpec((COMBINE_TM, D), lambda j, dest: (jnp.minimum(j, CTX_COMBINE_TILES - 1), 0)),
                     pl.BlockSpec((COMBINE_TM, D), lambda j, dest: (jnp.maximum(j - CTX_COMBINE_TILES, 0), 0))]
        out_shape = [jax.ShapeDtypeStruct((T_CTX, D), F32), jax.ShapeDtypeStruct((T_LAT, D), F32)]
    else:
        out_specs = [row(), row()]
        out_shape = [jax.ShapeDtypeStruct((T, D), F32), jax.ShapeDtypeStruct((T, D), BF16)]
    return pl.pallas_call(
        functools.partial(_combine_kernel, final=final),
        grid_spec=pltpu.PrefetchScalarGridSpec(
            num_scalar_prefetch=1,
            grid=(T // COMBINE_TM,),
            in_specs=[row(), pl.BlockSpec(memory_space=pl.ANY), row(2), mod(layer, 5), vec(), mod(nl, 1), mod(nl, 0)],
            out_specs=out_specs,
            scratch_shapes=[pltpu.VMEM((2, 2 * COMBINE_TM, D), F32), pltpu.SemaphoreType.DMA((2,))],
        ),
        out_shape=out_shape,
        compiler_params=_params(("arbitrary",)),
        name="moe_combine",
    )(dest_flat, x, y, gates, mods, next_g.reshape(1, D), mods, mods)


LRU_CHUNK = 256


def _shift_rows(v, s, row):
    n = v.shape[0]
    rolled = pltpu.roll(v, s % n, axis=0)
    keep = (row >= s) if s > 0 else (row < n + s)
    return jnp.where(keep, rolled, 0.0)


def _lru_kernel(proj_ref, h0_ref, cw_ref, cb_ref, wbd_ref, bbd_ref, lam_ref, scw_ref, scb_ref,
                y_ref, st_ref, xc_ref, a_ref, b_ref, *, seq_len):
    row = lax.broadcasted_iota(jnp.int32, (seq_len, 1), 0)
    x = proj_ref[:, 0:LRU_W]
    xc = (cb_ref[...] + _shift_rows(x, 2, row) * cw_ref[0:1] + _shift_rows(x, 1, row) * cw_ref[1:2]
          + x * cw_ref[2:3] + _shift_rows(x, -1, row) * cw_ref[3:4])
    xc_ref[...] = xc

    lam = lam_ref[...]
    neg_csp = -LRU_C * (jnp.maximum(-lam, 0.0) + jnp.log1p(jnp.exp(-jnp.abs(lam))))

    for c in range(seq_len // LRU_CHUNK):
        rows = pl.ds(c * LRU_CHUNK, LRU_CHUNK)
        xcc = xc_ref[rows, :]
        gates = jnp.dot(xcc.astype(BF16), wbd_ref[...], preferred_element_type=F32) + bbd_ref[...]
        for d in range(2):
            r = jax.nn.sigmoid(gates[:, (2 * d) * LRU_W:(2 * d + 1) * LRU_W])
            i = jax.nn.sigmoid(gates[:, (2 * d + 1) * LRU_W:(2 * d + 2) * LRU_W])
            log_a = neg_csp[d:d + 1] * r
            t = jnp.tanh(log_a)
            one_minus_a2 = -2.0 * t / (1.0 - t)
            a_ref[d, rows, :] = jnp.exp(log_a)
            b_ref[d, rows, :] = jnp.sqrt(one_minus_a2) * (i * xcc)

    n_grp = seq_len // 8

    def scan_body(g, carry):
        hf, hb = carry
        rf = pl.ds(pl.multiple_of(g * 8, 8), 8)
        rb = pl.ds(pl.multiple_of((n_grp - 1 - g) * 8, 8), 8)
        af, bf = a_ref[0, rf, :], b_ref[0, rf, :]
        ab, bb = a_ref[1, rb, :], b_ref[1, rb, :]
        outs_f, outs_b = [], []
        for j in range(8):
            hf = af[j:j + 1] * hf + bf[j:j + 1]
            outs_f.append(hf)
            hb = ab[7 - j:8 - j] * hb + bb[7 - j:8 - j]
            outs_b.append(hb)
        b_ref[0, rf, :] = jnp.concatenate(outs_f, axis=0)
        b_ref[1, rb, :] = jnp.concatenate(outs_b[::-1], axis=0)
        return hf, hb

    hf, hb = lax.fori_loop(0, n_grp, scan_body, (h0_ref[0, 0:1, :], h0_ref[0, 1:2, :]))
    st_ref[0, 0:1, :] = hf
    st_ref[0, 1:2, :] = hb

    h_sum = b_ref[0] + b_ref[1]
    y_ref[:, 0:LRU_W] = (h_sum * jax.nn.gelu(proj_ref[:, LRU_W:2 * LRU_W])).astype(y_ref.dtype)
    o = 2 * LRU_W
    cv = proj_ref[:, o + SC_W:o + 2 * SC_W] * proj_ref[:, o + 2 * SC_W:o + 3 * SC_W]
    conv = scb_ref[...] + _shift_rows(cv, 1, row) * scw_ref[0:1] + cv * scw_ref[1:2] + _shift_rows(cv, -1, row) * scw_ref[2:3]
    y_ref[:, LRU_W:LRU_W + SC_W] = (proj_ref[:, o:o + SC_W] * conv).astype(y_ref.dtype)


def _lru_mixer(proj, h0, cw, cb, wbd, bbd, lam, scw, scb, *, latent):
    if latent:
        n_seq, seq_len, blk0 = DEC_BATCH, DEC_SEQ, T_CTX // DEC_SEQ
    else:
        n_seq, seq_len, blk0 = BATCH, SEQ, 0
    full = lambda shape: pl.BlockSpec(shape, lambda b: (0,) * len(shape))
    return pl.pallas_call(
        functools.partial(_lru_kernel, seq_len=seq_len),
        grid=(n_seq,),
        in_specs=[pl.BlockSpec((seq_len, IN_W), lambda b: (blk0 + b, 0)),
                  pl.BlockSpec((1, 2, LRU_W), lambda b: (b, 0, 0)),
                  full((4, LRU_W)), full((1, LRU_W)), full((LRU_W, 4 * LRU_W)), full((1, 4 * LRU_W)),
                  full((2, LRU_W)), full((3, SC_W)), full((1, SC_W))],
        out_specs=[pl.BlockSpec((seq_len, D), lambda b: (b, 0)),
                   pl.BlockSpec((1, 2, LRU_W), lambda b: (b, 0, 0))],
        out_shape=[jax.ShapeDtypeStruct((n_seq * seq_len, D), BF16),
                   jax.ShapeDtypeStruct((n_seq, 2, LRU_W), F32)],
        scratch_shapes=[pltpu.VMEM((seq_len, LRU_W), F32),
                        pltpu.VMEM((2, seq_len, LRU_W), F32),
                        pltpu.VMEM((2, seq_len, LRU_W), F32)],
        compiler_params=_params(("arbitrary",)),
        name="lru_mixer",
    )(proj, h0, cw, cb.reshape(1, LRU_W), wbd, bbd, lam, scw, scb.reshape(1, SC_W))


def _block_diag_dense(w):
    eye = jnp.eye(LRU_HEADS, dtype=w.dtype)
    return jnp.einsum('hij,hg->higj', w, eye).reshape(LRU_W, LRU_W)


def _route(idx):
    e_flat = idx.reshape(-1)
    onehot = (e_flat[:, None] == jnp.arange(N_EXPERTS)[None, :]).astype(jnp.int32)
    csum = jnp.cumsum(onehot, axis=0)
    counts = csum[-1]
    rank = jnp.take_along_axis(csum, e_flat[:, None], axis=1)[:, 0] - 1
    padded = (counts + MOE_TM - 1) // MOE_TM * MOE_TM
    ends = jnp.cumsum(padded)
    dest = ((ends - padded)[e_flat] + rank).astype(jnp.int32)
    tile_end = ends // MOE_TM
    n_used = tile_end[-1]
    tiles = jnp.minimum(jnp.arange(MOE_TILES), n_used - 1)
    tile_expert = jnp.sum(tiles[:, None] >= tile_end[None, :], axis=1).astype(jnp.int32)
    return dest, tile_expert, n_used.reshape(1).astype(jnp.int32)


def kernel(x_prompt, x_sample, state_lru, cache_k, cache_v, c, c_ctx, norm1_g, norm2_g, w_mod, b_mod, w_in, lru_conv_w, lru_conv_b, lru_wa, lru_ba, lru_wx, lru_bx, lru_lam, sc_conv_w, sc_conv_b, w_out, ffn_w_gate, ffn_w_up, ffn_w_down, w_qkv, w_o, rpb, w_router, moe_w_gate, moe_w_up, moe_w_down, final_g):
    x = jnp.concatenate([x_prompt.reshape(T_CTX, D), x_sample.reshape(T_LAT, D)], axis=0)
    cond = jnp.concatenate([c_ctx[None, :], c, jnp.zeros((COND_ROWS - N_COND, D), F32)], axis=0)
    mods = _modulation(cond, w_mod, b_mod).reshape(DEPTH, 6, COND_ROWS, 1, D)

    hn = _first_norm(x, norm1_g[0], mods, 0)
    zero_state = jnp.zeros((BATCH, 2, LRU_W), F32)
    states = []
    new_k = jnp.zeros((BATCH, DEPTH // 2, N_HEADS, HEAD_DIM, SEQ), F32)
    new_v = jnp.zeros((BATCH, DEPTH // 2, N_HEADS, HEAD_DIM, SEQ), F32)
    for l in range(DEPTH):
        idx = l // 2
        if l % 2 == 0:
            proj = _matmul(hn, w_in[idx], IN_W // 2, F32)
            wbd = jnp.concatenate([_block_diag_dense(lru_wa[idx, 0]), _block_diag_dense(lru_wx[idx, 0]),
                                   _block_diag_dense(lru_wa[idx, 1]), _block_diag_dense(lru_wx[idx, 1])],
                                  axis=1).astype(BF16)
            bbd = jnp.concatenate([lru_ba[idx, 0], lru_bx[idx, 0], lru_ba[idx, 1], lru_bx[idx, 1]]).reshape(1, 4 * LRU_W)
            lru_args = (lru_conv_w[idx], lru_conv_b[idx], wbd, bbd, lru_lam[idx], sc_conv_w[idx], sc_conv_b[idx])
            y_ctx, st = _lru_mixer(proj, zero_state, *lru_args, latent=False)
            y_lat, _ = _lru_mixer(proj, state_lru[:, idx], *lru_args, latent=True)
            states.append(st)
            x, hn2 = _proj_res_norm(y_ctx, y_lat, w_out[idx], x, norm2_g[l], mods, l)
            x, hn = _ffn(hn2, ffn_w_gate, ffn_w_up, ffn_w_down, idx, x, mods, l, norm1_g[l + 1])
        else:
            wq = w_qkv[idx]
            q_scale = HEAD_DIM ** -0.5
            q_ctx = _matmul_heads(hn, wq, 0, latent=False, out_dtype=BF16, scale=q_scale)
            new_k = _matmul_heads(hn, wq, 1, latent=False, out_dtype=F32, into=new_k, slot=idx, n_slots=2,
                                  transposed=True)
            new_v = _matmul_heads(hn, wq, 2, latent=False, out_dtype=F32, into=new_v, slot=idx, n_slots=2,
                                  transposed=True)
            q_lat = _matmul_heads(hn, wq, 0, latent=True, out_dtype=BF16, scale=q_scale)
            k_lat = _matmul_heads(hn, wq, 1, latent=True, out_dtype=F32)
            v_lat = _matmul_heads(hn, wq, 2, latent=True, out_dtype=F32)
            o_ctx = _context_attention(q_ctx, new_k, new_v, idx)
            o_lat = _neighbourhood_attention(q_lat, k_lat, v_lat, cache_k, cache_v, _window_bias(rpb[idx]), idx)
            x, hn2, top_i, top_p = _proj_res_norm(o_ctx, o_lat,w_o[idx], x, norm2_g[l], mods, l, w_router[idx])
            dest, tile_expert, n_used = _route(top_i)
            xs = _moe_scatter(hn2, dest)
            y = _moe_experts(xs, tile_expert, n_used, moe_w_gate, moe_w_up, moe_w_down, idx)
            if l == DEPTH - 1:
                y_prompt, y_sample = _moe_combine(x, y, dest, top_p, mods, l, final_g, True)
            else:
                x, hn = _moe_combine(x, y, dest, top_p, mods, l, norm1_g[l + 1], False)

    return (y_prompt.reshape(BATCH, SEQ, D), y_sample.reshape(DEC_BATCH, DEC_SEQ, D), jnp.stack(states, axis=1),
            jnp.swapaxes(new_k, -1, -2), jnp.swapaxes(new_v, -1, -2))
```

```python
import functools

import jax
import jax.numpy as jnp
import numpy as np
from jax import lax
from jax.experimental import pallas as pl
from jax.experimental.pallas import tpu as pltpu

F32 = jnp.float32
BF16 = jnp.bfloat16

D = 1024
BATCH = 32
SEQ = 256
DEPTH = 4
DEC_BATCH = 2
DEC_SEQ = 1024
PAST_LEN = 256
GRID_W = 64
GRID_ROWS = DEC_SEQ // GRID_W
LRU_W = 512
LRU_HEADS = 8
LRU_BLOCK = LRU_W // LRU_HEADS
LRU_C = 8.0
SC_W = 512
IN_W = 2 * LRU_W + 3 * SC_W
N_HEADS = 16
HEAD_DIM = D // N_HEADS
WIN_R = 8
WIN_C = 16
D_FF = 2816
N_EXPERTS = 8
EPS = 1e-6
NEG_INF = -1e30

T_CTX = BATCH * SEQ
T_LAT = DEC_BATCH * DEC_SEQ
T = T_CTX + T_LAT
N_COND = 3
COND_ROWS = 8

TM = 1024
TF = 256
N_FCH = D_FF // TF
MOE_TM = 1024
MOE_TILES = (2 * T + N_EXPERTS * (MOE_TM - 1)) // MOE_TM
MOE_ROWS = MOE_TILES * MOE_TM
VMEM_LIMIT = 52 * 1024 * 1024


def _params(sem):
    return pltpu.CompilerParams(dimension_semantics=sem, vmem_limit_bytes=VMEM_LIMIT)


def _cond_of_tile(i, tm=TM):
    r = i * tm
    return jnp.where(r < T_CTX, 0, 1 + (r - T_CTX) // DEC_SEQ)


def _mod_spec(layer, k, grid_rank=1, tm=TM):
    if grid_rank == 1:
        return pl.BlockSpec((1, 1, 1, 1, D), lambda i: (layer, k, _cond_of_tile(i, tm), 0, 0))
    return pl.BlockSpec((1, 1, 1, 1, D), lambda i, f, *_: (layer, k, _cond_of_tile(i, tm), 0, 0))


def _rms_mod(x, g, scale, shift):
    y = x * lax.rsqrt(jnp.mean(x * x, axis=-1, keepdims=True) + EPS)
    return (y * g) * (1.0 + scale) + shift


def _mod_kernel(c_ref, w_ref, b_ref, o_ref):
    c = c_ref[...]
    s = c * jax.nn.sigmoid(c)
    o_ref[0, 0] = jnp.dot(s.astype(BF16), w_ref[0].astype(BF16), preferred_element_type=F32) + b_ref[0, 0]


def _modulation(cond, w_mod, b_mod):
    return pl.pallas_call(
        _mod_kernel,
        grid=(DEPTH, 6),
        in_specs=[
            pl.BlockSpec((COND_ROWS, D), lambda l, k: (0, 0)),
            pl.BlockSpec((1, D, D), lambda l, k: (l, 0, k)),
            pl.BlockSpec((1, 1, 1, D), lambda l, k: (l, k, 0, 0)),
        ],
        out_specs=pl.BlockSpec((1, 1, COND_ROWS, D), lambda l, k: (l, k, 0, 0)),
        out_shape=jax.ShapeDtypeStruct((DEPTH, 6, COND_ROWS, D), F32),
        compiler_params=_params(("arbitrary", "arbitrary")),
        name="adaln_mod",
    )(cond, w_mod, b_mod.reshape(DEPTH, 6, 1, D))


def _norm_kernel(xc_ref, xl_ref, g_ref, sc_ref, sh_ref, x_ref, o_ref):
    x = jnp.where(pl.program_id(0) < T_CTX // TM, xc_ref[...], xl_ref[...])
    x_ref[...] = x
    o_ref[...] = _rms_mod(x, g_ref[...], sc_ref[0, 0, 0], sh_ref[0, 0, 0]).astype(o_ref.dtype)


def _first_norm(x_ctx, x_lat, g, mods, layer):
    n_ctx = T_CTX // TM
    return pl.pallas_call(
        _norm_kernel,
        grid=(T // TM,),
        in_specs=[
            pl.BlockSpec((TM, D), lambda i: (jnp.minimum(i, n_ctx - 1), 0)),
            pl.BlockSpec((TM, D), lambda i: (jnp.maximum(i - n_ctx, 0), 0)),
            pl.BlockSpec((1, D), lambda i: (0, 0)),
            _mod_spec(layer, 1),
            _mod_spec(layer, 0),
        ],
        out_specs=[pl.BlockSpec((TM, D), lambda i: (i, 0)), pl.BlockSpec((TM, D), lambda i: (i, 0))],
        out_shape=[jax.ShapeDtypeStruct((T, D), F32), jax.ShapeDtypeStruct((T, D), BF16)],
        compiler_params=_params(("arbitrary",)),
        name="first_norm",
    )(x_ctx, x_lat, g.reshape(1, D), mods, mods)


def _mm_heads_kernel(a_ref, w_ref, o_ref, wb_ref, acc_ref, *, n_seq, seq_len, scale, transposed):
    @pl.when(pl.program_id(0) == 0)
    def _():
        wb_ref[...] = w_ref[0].astype(BF16)

    acc = jnp.dot(a_ref[...], wb_ref[...], preferred_element_type=F32)
    if scale != 1.0:
        acc = acc * scale
    if transposed:
        acc_ref[...] = acc
        acc_t = acc_ref[...].T
        for s in range(n_seq):
            for h in range(N_HEADS):
                o_ref[s, 0, h] = acc_t[h * HEAD_DIM:(h + 1) * HEAD_DIM,
                                       s * seq_len:(s + 1) * seq_len].astype(o_ref.dtype)
    else:
        for s in range(n_seq):
            for h in range(N_HEADS):
                o_ref[s, 0, h] = acc[s * seq_len:(s + 1) * seq_len,
                                     h * HEAD_DIM:(h + 1) * HEAD_DIM].astype(o_ref.dtype)


def _matmul_heads(hn, w_qkv, idx, part, *, latent, out_dtype, scale=1.0, into=None, slot=0, n_slots=1,
                  transposed=False):
    if latent:
        n_batch, seq_len, row0 = DEC_BATCH, DEC_SEQ, T_CTX // TM
    else:
        n_batch, seq_len, row0 = BATCH, SEQ, 0
    n_seq = TM // seq_len
    tail = (HEAD_DIM, seq_len) if transposed else (seq_len, HEAD_DIM)
    kern = functools.partial(_mm_heads_kernel, n_seq=n_seq, seq_len=seq_len, scale=scale, transposed=transposed)
    in_specs = [
        pl.BlockSpec((TM, D), lambda i: (row0 + i, 0)),
        pl.BlockSpec((1, D, D), lambda i: (idx, 0, part)),
    ]
    args = [hn, w_qkv]
    aliases = {}
    if into is not None:
        in_specs.append(pl.BlockSpec(memory_space=pl.ANY))
        args.append(into)
        aliases = {2: 0}
        body = lambda a, w, _, o, wb, acc: kern(a, w, o, wb, acc)
    else:
        body = kern
    return pl.pallas_call(
        body,
        grid=(n_batch // n_seq,),
        in_specs=in_specs,
        out_specs=pl.BlockSpec((n_seq, 1, N_HEADS) + tail, lambda i: (i, slot, 0, 0, 0)),
        out_shape=jax.ShapeDtypeStruct((n_batch, n_slots, N_HEADS) + tail, out_dtype),
        scratch_shapes=[pltpu.VMEM((D, D), BF16), pltpu.VMEM((TM, D), F32)],
        input_output_aliases=aliases,
        compiler_params=_params(("arbitrary",)),
        name="qkv_heads",
    )(*args)


def _dot_nt(a, b):
    return lax.dot_general(a, b, (((1,), (1,)), ((), ())), preferred_element_type=F32)


def _softmax_pv(s_list, v_list, v_transposed=False):
    m = s_list[0].max(axis=-1, keepdims=True)
    for s in s_list[1:]:
        m = jnp.maximum(m, s.max(axis=-1, keepdims=True))
    den = None
    out = None
    for s, v in zip(s_list, v_list):
        p = jnp.exp(s - m)
        d = p.sum(axis=-1, keepdims=True)
        pb = p.astype(BF16)
        o = _dot_nt(pb, v) if v_transposed else jnp.dot(pb, v, preferred_element_type=F32)
        den = d if den is None else den + d
        out = o if out is None else out + o
    return out / den


HEAD_PAIR_W = 2 * HEAD_DIM


def _ctx_attn_kernel(q_ref, kt_ref, vt_ref, o_ref):
    for hp in range(N_HEADS // 2):
        pair = []
        for h in (2 * hp, 2 * hp + 1):
            s = jnp.dot(q_ref[0, 0, h], kt_ref[0, 0, h].astype(BF16), preferred_element_type=F32)
            pair.append(_softmax_pv([s], [vt_ref[0, 0, h].astype(BF16)], v_transposed=True))
        o_ref[:, hp * HEAD_PAIR_W:(hp + 1) * HEAD_PAIR_W] = jnp.concatenate(pair, axis=-1).astype(o_ref.dtype)


def _context_attention(q, kt, vt, slot):
    kv_spec = pl.BlockSpec((1, 1, N_HEADS, HEAD_DIM, SEQ), lambda b: (b, slot, 0, 0, 0))
    return pl.pallas_call(
        _ctx_attn_kernel,
        grid=(BATCH,),
        in_specs=[pl.BlockSpec((1, 1, N_HEADS, SEQ, HEAD_DIM), lambda b: (b, 0, 0, 0, 0)), kv_spec, kv_spec],
        out_specs=pl.BlockSpec((SEQ, D), lambda b: (b, 0)),
        out_shape=jax.ShapeDtypeStruct((T_CTX, D), BF16),
        compiler_params=_params(("arbitrary",)),
        name="ctx_attention",
    )(q, kt, vt)


def _row_start(r):
    return min(max(r - WIN_R // 2, 0), GRID_ROWS - WIN_R)


def _nbr_attn_kernel(q_ref, k_ref, v_ref, kct_ref, vct_ref, bias_ref, o_ref):
    kct = [kct_ref[0, 0, hh].astype(BF16) for hh in range(2)]
    vct = [vct_ref[0, 0, hh].astype(BF16) for hh in range(2)]
    for r0, r1 in _ROW_GROUPS:
        rs = _row_start(r0)
        n_q = (r1 - r0) * GRID_W
        pair = []
        for hh in range(2):
            q = q_ref[0, 0, hh, r0 * GRID_W:r1 * GRID_W, :]
            kw = k_ref[0, 0, hh, rs * GRID_W:(rs + WIN_R) * GRID_W, :].astype(BF16)
            vw = v_ref[0, 0, hh, rs * GRID_W:(rs + WIN_R) * GRID_W, :].astype(BF16)
            bias = bias_ref[hh, r0 - rs:r1 - rs].reshape(n_q, WIN_R * GRID_W)
            s_win = _dot_nt(q, kw) + bias
            s_ctx = jnp.dot(q, kct[hh], preferred_element_type=F32)
            m = jnp.maximum(s_win.max(axis=-1, keepdims=True), s_ctx.max(axis=-1, keepdims=True))
            p_win = jnp.exp(s_win - m)
            p_ctx = jnp.exp(s_ctx - m)
            den = p_win.sum(axis=-1, keepdims=True) + p_ctx.sum(axis=-1, keepdims=True)
            out = (jnp.dot(p_win.astype(BF16), vw, preferred_element_type=F32)
                   + _dot_nt(p_ctx.astype(BF16), vct[hh]))
            pair.append(out / den)
        o_ref[r0 * GRID_W:r1 * GRID_W, :] = jnp.concatenate(pair, axis=-1).astype(o_ref.dtype)


def _row_groups():
    groups, r0 = [], 0
    for r in range(1, GRID_ROWS + 1):
        if r == GRID_ROWS or _row_start(r) != _row_start(r0):
            groups.append((r0, r))
            r0 = r
    return groups


_ROW_GROUPS = _row_groups()


def _window_bias(rpb_l):
    col = np.arange(GRID_W)
    col_start = np.clip(col - WIN_C // 2, 0, GRID_W - WIN_C)
    col_mask = (col[None, :] >= col_start[:, None]) & (col[None, :] < col_start[:, None] + WIN_C)
    dc = np.clip(col[None, :] - col[:, None], -(WIN_C - 1), WIN_C - 1) + (WIN_C - 1)
    onehot = (dc[None, :, :] == np.arange(2 * WIN_C - 1)[:, None, None]).astype(np.float32)
    rows = jnp.stack([rpb_l[:, WIN_R - 1 - v:2 * WIN_R - 1 - v, :] for v in range(WIN_R)], axis=1)
    bias = jnp.einsum('hvkd,dqc->hvqkc', rows, jnp.asarray(onehot), precision=lax.Precision.HIGHEST)
    bias = jnp.where(jnp.asarray(col_mask)[None, None, :, None, :], bias, NEG_INF)
    return bias.reshape(N_HEADS, WIN_R, GRID_W, WIN_R * GRID_W)


def _neighbourhood_attention(q, k, v, cache_kt, cache_vt, bias, slot):
    lat = lambda: pl.BlockSpec((1, 1, 2, DEC_SEQ, HEAD_DIM), lambda b, hp: (b, 0, hp, 0, 0))
    ctx = lambda: pl.BlockSpec((1, 1, 2, HEAD_DIM, PAST_LEN), lambda b, hp: (b, slot, hp, 0, 0))
    return pl.pallas_call(
        _nbr_attn_kernel,
        grid=(DEC_BATCH, N_HEADS // 2),
        in_specs=[lat(), lat(), lat(), ctx(), ctx(),
                  pl.BlockSpec((2, WIN_R, GRID_W, WIN_R * GRID_W), lambda b, hp: (hp, 0, 0, 0))],
        out_specs=pl.BlockSpec((DEC_SEQ, HEAD_PAIR_W), lambda b, hp: (b, hp)),
        out_shape=jax.ShapeDtypeStruct((T_LAT, D), BF16),
        compiler_params=_params(("arbitrary", "arbitrary")),
        name="nbr_attention",
    )(q, k, v, cache_kt, cache_vt, bias)


def _top2(logits):
    col = lax.broadcasted_iota(jnp.int32, logits.shape, 1)
    m1 = jnp.max(logits, axis=-1, keepdims=True)
    i1 = jnp.min(jnp.where(logits == m1, col, N_EXPERTS), axis=-1, keepdims=True)
    rest = jnp.where(col == i1, -jnp.inf, logits)
    m2 = jnp.max(rest, axis=-1, keepdims=True)
    i2 = jnp.min(jnp.where(rest == m2, col, N_EXPERTS), axis=-1, keepdims=True)
    e = jnp.exp(m2 - m1)
    return i1, i2, 1.0 / (1.0 + e), e / (1.0 + e)


CTX_TILES = T_CTX // TM


def _proj_res_norm_kernel(ac_ref, al_ref, w_ref, x_ref, g1_ref, ng_ref, sc_ref, sh_ref, *rest, router):
    if router:
        wr_ref, xo_ref, hn_ref, idx_ref, gate_ref, wb_ref = rest
    else:
        xo_ref, hn_ref, wb_ref = rest

    @pl.when(pl.program_id(0) == 0)
    def _():
        wb_ref[...] = w_ref[0].astype(BF16)

    a = jnp.where(pl.program_id(0) < CTX_TILES, ac_ref[...], al_ref[...])
    mix = jnp.dot(a, wb_ref[...], preferred_element_type=F32)
    x = x_ref[...] + g1_ref[0, 0, 0] * mix
    xo_ref[...] = x
    hn = _rms_mod(x, ng_ref[...], sc_ref[0, 0, 0], sh_ref[0, 0, 0])
    hn_ref[...] = hn.astype(hn_ref.dtype)
    if router:
        hi = hn.astype(BF16)
        lo = (hn - hi.astype(F32)).astype(BF16)
        l_hi = jnp.dot(hi, wr_ref[...], preferred_element_type=F32)
        l_lo = jnp.dot(lo, wr_ref[...], preferred_element_type=F32)
        logits = l_hi[:, :N_EXPERTS] + l_hi[:, N_EXPERTS:] + l_lo[:, :N_EXPERTS]
        i1, i2, p1, p2 = _top2(logits)
        first = lax.broadcasted_iota(jnp.int32, (logits.shape[0], 2), 1) == 0
        idx_ref[...] = jnp.where(first, i1, i2)
        gate_ref[...] = jnp.where(first, p1, p2)


def _proj_res_norm(a_ctx, a_lat, w, idx, x, norm_g, mods, layer, w_router=None):
    router = w_router is not None
    row = lambda n: pl.BlockSpec((TM, n), lambda i: (i, 0))
    in_specs = [pl.BlockSpec((TM, D), lambda i: (jnp.minimum(i, CTX_TILES - 1), 0)),
                pl.BlockSpec((TM, D), lambda i: (jnp.maximum(i - CTX_TILES, 0), 0)),
                pl.BlockSpec((1, D, D), lambda i: (idx, 0, 0)), row(D),
                _mod_spec(layer, 2), pl.BlockSpec((1, D), lambda i: (0, 0)), _mod_spec(layer, 4), _mod_spec(layer, 3)]
    args = [a_ctx, a_lat, w, x, mods, norm_g.reshape(1, D), mods, mods]
    out_specs = [row(D), row(D)]
    out_shape = [jax.ShapeDtypeStruct((T, D), F32), jax.ShapeDtypeStruct((T, D), F32 if router else BF16)]
    if router:
        w_hi = w_router.astype(BF16)
        w_lo = (w_router - w_hi.astype(F32)).astype(BF16)
        in_specs.append(pl.BlockSpec((D, 2 * N_EXPERTS), lambda i: (0, 0)))
        args.append(jnp.concatenate([w_hi, w_lo], axis=1))
        out_specs += [row(2), row(2)]
        out_shape += [jax.ShapeDtypeStruct((T, 2), jnp.int32), jax.ShapeDtypeStruct((T, 2), F32)]
    return pl.pallas_call(
        functools.partial(_proj_res_norm_kernel, router=router),
        grid=(T // TM,),
        in_specs=in_specs,
        out_specs=out_specs,
        out_shape=out_shape,
        scratch_shapes=[pltpu.VMEM((D, D), BF16)],
        compiler_params=_params(("arbitrary",)),
        name="proj_res_norm",
    )(*args)


def _swiglu_chunk(hn, wg_ref, wu_ref, wd_ref, acc_ref):
    g = jnp.dot(hn, wg_ref[0].astype(BF16), preferred_element_type=F32)
    u = jnp.dot(hn, wu_ref[0].astype(BF16), preferred_element_type=F32)
    h = (g * jax.nn.sigmoid(g) * u).astype(BF16)
    acc_ref[...] += jnp.dot(h, wd_ref[0].astype(BF16), preferred_element_type=F32)


def _ffn_kernel(hn_ref, wg_ref, wu_ref, wd_ref, x_ref, g2_ref, ng_ref, sc_ref, sh_ref, xo_ref, hno_ref, acc_ref):
    @pl.when(pl.program_id(1) == 0)
    def _():
        acc_ref[...] = jnp.zeros_like(acc_ref)

    _swiglu_chunk(hn_ref[...], wg_ref, wu_ref, wd_ref, acc_ref)

    @pl.when(pl.program_id(1) == N_FCH - 1)
    def _():
        x = x_ref[...] + g2_ref[0, 0, 0] * acc_ref[...]
        xo_ref[...] = x
        hno_ref[...] = _rms_mod(x, ng_ref[...], sc_ref[0, 0, 0], sh_ref[0, 0, 0]).astype(hno_ref.dtype)


def _ffn(hn, wg, wu, wd, idx, x, mods, layer, next_g):
    row = lambda: pl.BlockSpec((TM, D), lambda i, f: (i, 0))
    return pl.pallas_call(
        _ffn_kernel,
        grid=(T // TM, N_FCH),
        in_specs=[row(),
                  pl.BlockSpec((1, D, TF), lambda i, f: (idx, 0, f)),
                  pl.BlockSpec((1, D, TF), lambda i, f: (idx, 0, f)),
                  pl.BlockSpec((1, TF, D), lambda i, f: (idx, f, 0)),
                  row(), _mod_spec(layer, 5, 2), pl.BlockSpec((1, D), lambda i, f: (0, 0)),
                  _mod_spec(layer + 1, 1, 2), _mod_spec(layer + 1, 0, 2)],
        out_specs=[row(), row()],
        out_shape=[jax.ShapeDtypeStruct((T, D), F32), jax.ShapeDtypeStruct((T, D), BF16)],
        scratch_shapes=[pltpu.VMEM((TM, D), F32)],
        compiler_params=_params(("arbitrary", "arbitrary")),
        name="ffn",
    )(hn, wg, wu, wd, x, mods, next_g.reshape(1, D), mods, mods)


ROW_UNROLL = 8
SCATTER_TM = 512
COMBINE_TM = 512


def _wait_rows(src_ref, dst_ref, sem, n_rows):
    pltpu.make_async_copy(src_ref.at[pl.ds(0, n_rows)], dst_ref.at[pl.ds(0, n_rows)], sem).wait()


def _scatter_kernel(dest_ref, hn_ref, xs_in_ref, xs_ref, sem):
    del xs_in_ref
    base = pl.program_id(0) * (2 * SCATTER_TM)

    def body(g, carry):
        for u in range(ROW_UNROLL):
            r = g * ROW_UNROLL + u
            for k in range(2):
                d = dest_ref[base + 2 * r + k]
                pltpu.make_async_copy(hn_ref.at[pl.ds(r, 1)], xs_ref.at[pl.ds(d, 1)], sem).start()
        return carry

    lax.fori_loop(0, SCATTER_TM // ROW_UNROLL, body, 0)
    for _ in range(2):
        _wait_rows(hn_ref, xs_ref, sem, SCATTER_TM)


def _moe_scatter(hn2, dest_flat):
    return pl.pallas_call(
        _scatter_kernel,
        grid_spec=pltpu.PrefetchScalarGridSpec(
            num_scalar_prefetch=1,
            grid=(T // SCATTER_TM,),
            in_specs=[pl.BlockSpec((SCATTER_TM, D), lambda j, dest: (j, 0)),
                      pl.BlockSpec(memory_space=pl.ANY)],
            out_specs=pl.BlockSpec(memory_space=pl.ANY),
            scratch_shapes=[pltpu.SemaphoreType.DMA],
        ),
        out_shape=jax.ShapeDtypeStruct((MOE_ROWS, D), F32),
        input_output_aliases={2: 0},
        compiler_params=_params(("arbitrary",)),
        name="moe_scatter",
    )(dest_flat, hn2, jnp.zeros((MOE_ROWS, D), F32))


def _moe_kernel(te_ref, nu_ref, xs_ref, wg_ref, wu_ref, wd_ref, y_ref, xsb_ref, acc_ref):
    i = pl.program_id(0)
    f = pl.program_id(1)

    @pl.when(i < nu_ref[0])
    def _():
        @pl.when(f == 0)
        def _():
            xsb_ref[...] = xs_ref[...].astype(BF16)
            acc_ref[...] = jnp.zeros_like(acc_ref)

        _swiglu_chunk(xsb_ref[...], wg_ref, wu_ref, wd_ref, acc_ref)

        @pl.when(f == N_FCH - 1)
        def _():
            y_ref[...] = acc_ref[...]

    @pl.when(jnp.logical_and(i >= nu_ref[0], f == N_FCH - 1))
    def _():
        y_ref[...] = jnp.zeros_like(y_ref)


def _moe_experts(xs, tile_expert, n_used, wg, wu, wd, idx):
    def wspec(shape, col):
        def imap(i, f, te, nu):
            fe = jnp.where(i < nu[0], f, N_FCH - 1)
            return (idx, te[i], 0, fe) if col else (idx, te[i], fe, 0)
        return pl.BlockSpec(shape, imap)

    def kern(te, nu, xs_ref, wg_ref, wu_ref, wd_ref, y_ref, xsb_ref, acc_ref):
        _moe_kernel(te, nu, xs_ref, wg_ref.at[0], wu_ref.at[0], wd_ref.at[0], y_ref, xsb_ref, acc_ref)

    return pl.pallas_call(
        kern,
        grid_spec=pltpu.PrefetchScalarGridSpec(
            num_scalar_prefetch=2,
            grid=(MOE_TILES, N_FCH),
            in_specs=[pl.BlockSpec((MOE_TM, D), lambda i, f, te, nu: (jnp.minimum(i, nu[0] - 1), 0)),
                      wspec((1, 1, D, TF), True), wspec((1, 1, D, TF), True), wspec((1, 1, TF, D), False)],
            out_specs=pl.BlockSpec((MOE_TM, D), lambda i, f, te, nu: (i, 0)),
            scratch_shapes=[pltpu.VMEM((MOE_TM, D), BF16), pltpu.VMEM((MOE_TM, D), F32)],
        ),
        out_shape=jax.ShapeDtypeStruct((MOE_ROWS, D), F32),
        compiler_params=_params(("arbitrary", "arbitrary")),
        name="moe_experts",
    )(tile_expert, n_used, xs, wg, wu, wd)


CTX_COMBINE_TILES = T_CTX // COMBINE_TM


def _combine_kernel(dest_ref, x_ref, y_hbm, gate_ref, g2_ref, ng_ref, sc_ref, sh_ref, *rest, final):
    *outs, ybuf, sem = rest
    j = pl.program_id(0)
    slot = j % 2

    def issue(tile, s):
        base = tile * (2 * COMBINE_TM)

        def body(g, carry):
            for u in range(ROW_UNROLL):
                r = g * ROW_UNROLL + u
                for k in range(2):
                    d = dest_ref[base + 2 * r + k]
                    pltpu.make_async_copy(y_hbm.at[pl.ds(d, 1)], ybuf.at[s, pl.ds(k * COMBINE_TM + r, 1)],
                                          sem.at[s]).start()
            return carry

        lax.fori_loop(0, COMBINE_TM // ROW_UNROLL, body, 0)

    @pl.when(j == 0)
    def _():
        issue(0, 0)

    @pl.when(j + 1 < pl.num_programs(0))
    def _():
        issue(j + 1, 1 - slot)

    _wait_rows(y_hbm, ybuf.at[slot], sem.at[slot], 2 * COMBINE_TM)
    gates = gate_ref[...]
    moe = gates[:, 0:1] * ybuf[slot, 0:COMBINE_TM] + gates[:, 1:2] * ybuf[slot, COMBINE_TM:2 * COMBINE_TM]
    x = x_ref[...] + g2_ref[0, 0, 0] * moe
    if final:
        yp_ref, ys_ref = outs
        y = _rms_mod(x, ng_ref[...], 0.0, 0.0)

        @pl.when(j < CTX_COMBINE_TILES)
        def _():
            yp_ref[...] = y

        @pl.when(j >= CTX_COMBINE_TILES)
        def _():
            ys_ref[...] = y
    else:
        xo_ref, hno_ref = outs
        xo_ref[...] = x
        hno_ref[...] = _rms_mod(x, ng_ref[...], sc_ref[0, 0, 0], sh_ref[0, 0, 0]).astype(hno_ref.dtype)


def _moe_combine(x, y, dest_flat, gates, mods, layer, next_g, final):
    row = lambda n=D: pl.BlockSpec((COMBINE_TM, n), lambda j, dest: (j, 0))
    vec = lambda: pl.BlockSpec((1, D), lambda j, dest: (0, 0))
    mod = lambda l, k: pl.BlockSpec((1, 1, 1, 1, D), lambda j, dest: (l, k, _cond_of_tile(j, COMBINE_TM), 0, 0))
    nl = layer if final else layer + 1
    if final:
        out_specs = [pl.BlockSpec((COMBINE_TM, D), lambda j, dest: (jnp.minimum(j, CTX_COMBINE_TILES - 1), 0)),
                     pl.BlockSpec((COMBINE_TM, D), lambda j, dest: (jnp.maximum(j - CTX_COMBINE_TILES, 0), 0))]
        out_shape = [jax.ShapeDtypeStruct((T_CTX, D), F32), jax.ShapeDtypeStruct((T_LAT, D), F32)]
    else:
        out_specs = [row(), row()]
        out_shape = [jax.ShapeDtypeStruct((T, D), F32), jax.ShapeDtypeStruct((T, D), BF16)]
    return pl.pallas_call(
        functools.partial(_combine_kernel, final=final),
        grid_spec=pltpu.PrefetchScalarGridSpec(
            num_scalar_prefetch=1,
            grid=(T // COMBINE_TM,),
            in_specs=[row(), pl.BlockSpec(memory_space=pl.ANY), row(2), mod(layer, 5), vec(), mod(nl, 1), mod(nl, 0)],
            out_specs=out_specs,
            scratch_shapes=[pltpu.VMEM((2, 2 * COMBINE_TM, D), F32), pltpu.SemaphoreType.DMA((2,))],
        ),
        out_shape=out_shape,
        compiler_params=_params(("arbitrary",)),
        name="moe_combine",
    )(dest_flat, x, y, gates, mods, next_g.reshape(1, D), mods, mods)


LRU_CHUNK = 256


def _shift_rows(v, s, row):
    n = v.shape[0]
    rolled = pltpu.roll(v, s % n, axis=0)
    keep = (row >= s) if s > 0 else (row < n + s)
    return jnp.where(keep, rolled, 0.0)


W_IN_CHUNK = 512
N_W_IN_CHUNKS = IN_W // W_IN_CHUNK


def _lru_kernel(hn_ref, h0_ref, win_hbm, cw_ref, cb_ref, wbd_ref, bbd_ref, lam_ref, scw_ref, scb_ref,
                y_ref, st_ref, winb_ref, stage_ref, stage_sem, proj_ref, xc_ref, a_ref, b_ref, *, seq_len, idx):
    @pl.when(pl.program_id(0) == 0)
    def _():
        copies = [pltpu.make_async_copy(win_hbm.at[idx, :, pl.ds(j * W_IN_CHUNK, W_IN_CHUNK)],
                                        stage_ref.at[j % 2], stage_sem.at[j % 2]) for j in range(N_W_IN_CHUNKS)]
        copies[0].start()
        for j in range(N_W_IN_CHUNKS):
            if j + 1 < N_W_IN_CHUNKS:
                copies[j + 1].start()
            copies[j].wait()
            winb_ref[:, j * W_IN_CHUNK:(j + 1) * W_IN_CHUNK] = stage_ref[j % 2].astype(BF16)

    proj_ref[...] = jnp.dot(hn_ref[...], winb_ref[...], preferred_element_type=F32)
    row = lax.broadcasted_iota(jnp.int32, (seq_len, 1), 0)
    x = proj_ref[:, 0:LRU_W]
    xc =(cb_ref[...] + _shift_rows(x, 2, row) * cw_ref[0:1] + _shift_rows(x, 1, row) * cw_ref[1:2]
          + x * cw_ref[2:3] + _shift_rows(x, -1, row) * cw_ref[3:4])
    xc_ref[...] = xc

    lam = lam_ref[...]
    neg_csp = -LRU_C * (jnp.maximum(-lam, 0.0) + jnp.log1p(jnp.exp(-jnp.abs(lam))))

    for c in range(seq_len // LRU_CHUNK):
        rows = pl.ds(c * LRU_CHUNK, LRU_CHUNK)
        xcc = xc_ref[rows, :]
        gates = jnp.dot(xcc.astype(BF16), wbd_ref[...], preferred_element_type=F32) + bbd_ref[...]
        for d in range(2):
            r = jax.nn.sigmoid(gates[:, (2 * d) * LRU_W:(2 * d + 1) * LRU_W])
            i = jax.nn.sigmoid(gates[:, (2 * d + 1) * LRU_W:(2 * d + 2) * LRU_W])
            log_a = neg_csp[d:d + 1] * r
            t = jnp.tanh(log_a)
            one_minus_a2 = -2.0 * t / (1.0 - t)
            a_ref[d, rows, :] = jnp.exp(log_a)
            b_ref[d, rows, :] = jnp.sqrt(one_minus_a2) * (i * xcc)

    n_grp = seq_len // 8

    def scan_body(g, carry):
        hf, hb = carry
        rf = pl.ds(pl.multiple_of(g * 8, 8), 8)
        rb = pl.ds(pl.multiple_of((n_grp - 1 - g) * 8, 8), 8)
        af, bf = a_ref[0, rf, :], b_ref[0, rf, :]
        ab, bb = a_ref[1, rb, :], b_ref[1, rb, :]
        outs_f, outs_b = [], []
        for j in range(8):
            hf = af[j:j + 1] * hf + bf[j:j + 1]
            outs_f.append(hf)
            hb = ab[7 - j:8 - j] * hb + bb[7 - j:8 - j]
            outs_b.append(hb)
        b_ref[0, rf, :] = jnp.concatenate(outs_f, axis=0)
        b_ref[1, rb, :] = jnp.concatenate(outs_b[::-1], axis=0)
        return hf, hb

    hf, hb = lax.fori_loop(0, n_grp, scan_body, (h0_ref[0, 0:1, :], h0_ref[0, 1:2, :]))
    st_ref[0, 0:1, :] = hf
    st_ref[0, 1:2, :] = hb

    h_sum = b_ref[0] + b_ref[1]
    y_ref[:, 0:LRU_W] = (h_sum * jax.nn.gelu(proj_ref[:, LRU_W:2 * LRU_W])).astype(y_ref.dtype)
    o = 2 * LRU_W
    cv = proj_ref[:, o + SC_W:o + 2 * SC_W] * proj_ref[:, o + 2 * SC_W:o + 3 * SC_W]
    conv = scb_ref[...] + _shift_rows(cv, 1, row) * scw_ref[0:1] + cv * scw_ref[1:2] + _shift_rows(cv, -1, row) * scw_ref[2:3]
    y_ref[:, LRU_W:LRU_W + SC_W] = (proj_ref[:, o:o + SC_W] * conv).astype(y_ref.dtype)


def _lru_mixer(hn, h0, w_in, idx, cw, cb, wbd, bbd, lam, scw, scb, *, latent):
    if latent:
        n_seq, seq_len, blk0 = DEC_BATCH, DEC_SEQ, T_CTX // DEC_SEQ
    else:
        n_seq, seq_len, blk0 = BATCH, SEQ, 0
    full = lambda shape: pl.BlockSpec(shape, lambda b: (0,) * len(shape))
    return pl.pallas_call(
        functools.partial(_lru_kernel, seq_len=seq_len, idx=idx),
        grid=(n_seq,),
        in_specs=[pl.BlockSpec((seq_len, D), lambda b: (blk0 + b, 0)),
                  pl.BlockSpec((1, 2, LRU_W), lambda b: (b, 0, 0)),
                  pl.BlockSpec(memory_space=pl.ANY),
                  full((4, LRU_W)), full((1, LRU_W)), full((LRU_W, 4 * LRU_W)), full((1, 4 * LRU_W)),
                  full((2, LRU_W)), full((3, SC_W)), full((1, SC_W))],
        out_specs=[pl.BlockSpec((seq_len, D), lambda b: (b, 0)),
                   pl.BlockSpec((1, 2, LRU_W), lambda b: (b, 0, 0))],
        out_shape=[jax.ShapeDtypeStruct((n_seq * seq_len, D), BF16),
                   jax.ShapeDtypeStruct((n_seq, 2, LRU_W), F32)],
        scratch_shapes=[pltpu.VMEM((D, IN_W), BF16),
                        pltpu.VMEM((2, D, W_IN_CHUNK), F32),
                        pltpu.SemaphoreType.DMA((2,)),
                        pltpu.VMEM((seq_len, IN_W), F32),
                        pltpu.VMEM((seq_len, LRU_W), F32),
                        pltpu.VMEM((2, seq_len, LRU_W), F32),
                        pltpu.VMEM((2, seq_len, LRU_W), F32)],
        compiler_params=_params(("arbitrary",)),
        name="lru_mixer",
    )(hn, h0, w_in, cw, cb.reshape(1, LRU_W), wbd, bbd, lam, scw, scb.reshape(1, SC_W))


def _block_diag_dense(w):
    eye = jnp.eye(LRU_HEADS, dtype=w.dtype)
    return jnp.einsum('hij,hg->higj', w, eye).reshape(LRU_W, LRU_W)


def _route(idx):
    e_flat = idx.reshape(-1)
    onehot = (e_flat[:, None] == jnp.arange(N_EXPERTS)[None, :]).astype(jnp.int32)
    csum = jnp.cumsum(onehot, axis=0)
    counts = csum[-1]
    rank = jnp.take_along_axis(csum, e_flat[:, None], axis=1)[:, 0] - 1
    padded = (counts + MOE_TM - 1) // MOE_TM * MOE_TM
    ends = jnp.cumsum(padded)
    dest = ((ends - padded)[e_flat] + rank).astype(jnp.int32)
    tile_end = ends // MOE_TM
    n_used = tile_end[-1]
    tiles = jnp.minimum(jnp.arange(MOE_TILES), n_used - 1)
    tile_expert = jnp.sum(tiles[:, None] >= tile_end[None, :], axis=1).astype(jnp.int32)
    return dest, tile_expert, n_used.reshape(1).astype(jnp.int32)


def kernel(x_prompt, x_sample, state_lru, cache_k, cache_v, c, c_ctx, norm1_g, norm2_g, w_mod, b_mod, w_in, lru_conv_w, lru_conv_b, lru_wa, lru_ba, lru_wx, lru_bx, lru_lam, sc_conv_w, sc_conv_b, w_out, ffn_w_gate, ffn_w_up, ffn_w_down, w_qkv, w_o, rpb, w_router, moe_w_gate, moe_w_up, moe_w_down, final_g):
    cond = jnp.concatenate([c_ctx[None, :], c, jnp.zeros((COND_ROWS - N_COND, D), F32)], axis=0)
    mods = _modulation(cond, w_mod, b_mod).reshape(DEPTH, 6, COND_ROWS, 1, D)

    x, hn = _first_norm(x_prompt.reshape(T_CTX, D), x_sample.reshape(T_LAT, D), norm1_g[0], mods, 0)
    cache_kt = jnp.swapaxes(cache_k, -1, -2)
    cache_vt = jnp.swapaxes(cache_v, -1, -2)
    zero_state = jnp.zeros((BATCH, 2, LRU_W), F32)
    states = []
    new_k = jnp.zeros((BATCH, DEPTH // 2, N_HEADS, HEAD_DIM, SEQ), F32)
    new_v = jnp.zeros((BATCH, DEPTH // 2, N_HEADS, HEAD_DIM, SEQ), F32)
    for l in range(DEPTH):
        idx = l // 2
        if l % 2 == 0:
            wbd = jnp.concatenate([_block_diag_dense(lru_wa[idx, 0]), _block_diag_dense(lru_wx[idx, 0]),
                                   _block_diag_dense(lru_wa[idx, 1]), _block_diag_dense(lru_wx[idx, 1])],
                                  axis=1).astype(BF16)
            bbd = jnp.concatenate([lru_ba[idx, 0], lru_bx[idx, 0], lru_ba[idx, 1], lru_bx[idx, 1]]).reshape(1, 4 * LRU_W)
            lru_args = (lru_conv_w[idx], lru_conv_b[idx], wbd, bbd, lru_lam[idx], sc_conv_w[idx], sc_conv_b[idx])
            y_ctx, st = _lru_mixer(hn, zero_state, w_in, idx, *lru_args, latent=False)
            y_lat, _ = _lru_mixer(hn, state_lru[:, idx], w_in, idx, *lru_args, latent=True)
            states.append(st)
            x, hn2 = _proj_res_norm(y_ctx, y_lat, w_out, idx, x, norm2_g[l], mods, l)
            x, hn = _ffn(hn2, ffn_w_gate, ffn_w_up, ffn_w_down, idx, x, mods, l, norm1_g[l + 1])
        else:
            q_scale = HEAD_DIM ** -0.5
            q_ctx = _matmul_heads(hn, w_qkv, idx, 0, latent=False, out_dtype=BF16, scale=q_scale)
            new_k = _matmul_heads(hn, w_qkv, idx, 1, latent=False, out_dtype=F32, into=new_k, slot=idx, n_slots=2,
                                  transposed=True)
            new_v = _matmul_heads(hn, w_qkv, idx, 2, latent=False, out_dtype=F32, into=new_v, slot=idx, n_slots=2,
                                  transposed=True)
            q_lat = _matmul_heads(hn, w_qkv, idx, 0, latent=True, out_dtype=BF16, scale=q_scale)
            k_lat = _matmul_heads(hn, w_qkv, idx, 1, latent=True, out_dtype=F32)
            v_lat = _matmul_heads(hn, w_qkv, idx, 2, latent=True, out_dtype=F32)
            o_ctx = _context_attention(q_ctx, new_k, new_v, idx)
            o_lat = _neighbourhood_attention(q_lat, k_lat, v_lat, cache_kt, cache_vt, _window_bias(rpb[idx]), idx)
            x, hn2, top_i, top_p = _proj_res_norm(o_ctx, o_lat, w_o, idx, x, norm2_g[l], mods, l, w_router[idx])
            dest, tile_expert, n_used = _route(top_i)
            xs = _moe_scatter(hn2, dest)
            y = _moe_experts(xs, tile_expert, n_used, moe_w_gate, moe_w_up, moe_w_down, idx)
            if l == DEPTH - 1:
                y_prompt, y_sample = _moe_combine(x, y, dest, top_p, mods, l, final_g, True)
            else:
                x, hn = _moe_combine(x, y, dest, top_p, mods, l, norm1_g[l + 1], False)

    return (y_prompt.reshape(BATCH, SEQ, D), y_sample.reshape(DEC_BATCH, DEC_SEQ, D), jnp.stack(states, axis=1),
            jnp.swapaxes(new_k, -1, -2), jnp.swapaxes(new_v, -1, -2))
```

```python
import functools

import jax
import jax.numpy as jnp
import numpy as np
from jax import lax
from jax.experimental import pallas as pl
from jax.experimental.pallas import tpu as pltpu

F32 = jnp.float32
BF16 = jnp.bfloat16

D = 1024
BATCH = 32
SEQ = 256
DEPTH = 4
DEC_BATCH = 2
DEC_SEQ = 1024
PAST_LEN = 256
GRID_W = 64
GRID_ROWS = DEC_SEQ // GRID_W
LRU_W = 512
LRU_HEADS = 8
LRU_BLOCK = LRU_W // LRU_HEADS
LRU_C = 8.0
SC_W = 512
IN_W = 2 * LRU_W + 3 * SC_W
N_HEADS = 16
HEAD_DIM = D // N_HEADS
WIN_R = 8
WIN_C = 16
D_FF = 2816
N_EXPERTS = 8
EPS = 1e-6
NEG_INF = -1e30

T_CTX = BATCH * SEQ
T_LAT = DEC_BATCH * DEC_SEQ
T = T_CTX + T_LAT
N_COND = 3
COND_ROWS = 8

TM = 1024
TF = 256
N_FCH = D_FF // TF
MOE_TM = 1024
MOE_TILES = (2 * T + N_EXPERTS * (MOE_TM - 1)) // MOE_TM
MOE_ROWS = MOE_TILES * MOE_TM
VMEM_LIMIT = 52 * 1024 * 1024


def _params(sem):
    return pltpu.CompilerParams(dimension_semantics=sem, vmem_limit_bytes=VMEM_LIMIT)


def _cond_of_tile(i, tm=TM):
    r = i * tm
    return jnp.where(r < T_CTX, 0, 1 + (r - T_CTX) // DEC_SEQ)


def _mod_spec(layer, k, grid_rank=1, tm=TM):
    if grid_rank == 1:
        return pl.BlockSpec((1, 1, 1, 1, D), lambda i: (layer, k, _cond_of_tile(i, tm), 0, 0))
    return pl.BlockSpec((1, 1, 1, 1, D), lambda i, f, *_: (layer, k, _cond_of_tile(i, tm), 0, 0))


def _rms_mod(x, g, scale, shift):
    y = x * lax.rsqrt(jnp.mean(x * x, axis=-1, keepdims=True) + EPS)
    return (y * g) * (1.0 + scale) + shift


def _mod_kernel(c_ref, w_ref, b_ref, o_ref):
    c = c_ref[...]
    s = c * jax.nn.sigmoid(c)
    o_ref[0, 0] = jnp.dot(s.astype(BF16), w_ref[0].astype(BF16), preferred_element_type=F32) + b_ref[0, 0]


def _modulation(cond, w_mod, b_mod):
    return pl.pallas_call(
        _mod_kernel,
        grid=(DEPTH, 6),
        in_specs=[
            pl.BlockSpec((COND_ROWS, D), lambda l, k: (0, 0)),
            pl.BlockSpec((1, D, D), lambda l, k: (l, 0, k)),
            pl.BlockSpec((1, 1, 1, D), lambda l, k: (l, k, 0, 0)),
        ],
        out_specs=pl.BlockSpec((1, 1, COND_ROWS, D), lambda l, k: (l, k, 0, 0)),
        out_shape=jax.ShapeDtypeStruct((DEPTH, 6, COND_ROWS, D), F32),
        compiler_params=_params(("arbitrary", "arbitrary")),
        name="adaln_mod",
    )(cond, w_mod, b_mod.reshape(DEPTH, 6, 1, D))


def _norm_kernel(xc_ref, xl_ref, g_ref, sc_ref, sh_ref, x_ref, o_ref):
    x = jnp.where(pl.program_id(0) < T_CTX // TM, xc_ref[...], xl_ref[...])
    x_ref[...] = x
    o_ref[...] = _rms_mod(x, g_ref[...], sc_ref[0, 0, 0], sh_ref[0, 0, 0]).astype(o_ref.dtype)


def _first_norm(x_ctx, x_lat, g, mods, layer):
    n_ctx = T_CTX // TM
    return pl.pallas_call(
        _norm_kernel,
        grid=(T // TM,),
        in_specs=[
            pl.BlockSpec((TM, D), lambda i: (jnp.minimum(i, n_ctx - 1), 0)),
            pl.BlockSpec((TM, D), lambda i: (jnp.maximum(i - n_ctx, 0), 0)),
            pl.BlockSpec((1, D), lambda i: (0, 0)),
            _mod_spec(layer, 1),
            _mod_spec(layer, 0),
        ],
        out_specs=[pl.BlockSpec((TM, D), lambda i: (i, 0)), pl.BlockSpec((TM, D), lambda i: (i, 0))],
        out_shape=[jax.ShapeDtypeStruct((T, D), F32), jax.ShapeDtypeStruct((T, D), BF16)],
        compiler_params=_params(("arbitrary",)),
        name="first_norm",
    )(x_ctx, x_lat, g.reshape(1, D), mods, mods)


def _mm_heads_kernel(a_ref, w_ref, o_ref, wb_ref, acc_ref, *, n_seq, seq_len, scale, transposed):
    @pl.when(pl.program_id(0) == 0)
    def _():
        wb_ref[...] = w_ref[0].astype(BF16)

    acc = jnp.dot(a_ref[...], wb_ref[...], preferred_element_type=F32)
    if scale != 1.0:
        acc = acc * scale
    if transposed:
        acc_ref[...] = acc
        acc_t = acc_ref[...].T
        for s in range(n_seq):
            for h in range(N_HEADS):
                o_ref[s, 0, h] = acc_t[h * HEAD_DIM:(h + 1) * HEAD_DIM,
                                       s * seq_len:(s + 1) * seq_len].astype(o_ref.dtype)
    else:
        for s in range(n_seq):
            for h in range(N_HEADS):
                o_ref[s, 0, h] = acc[s * seq_len:(s + 1) * seq_len,
                                     h * HEAD_DIM:(h + 1) * HEAD_DIM].astype(o_ref.dtype)


def _matmul_heads(hn, w_qkv, idx, part, *, latent, out_dtype, scale=1.0, into=None, slot=0, n_slots=1,
                  transposed=False):
    if latent:
        n_batch, seq_len, row0 = DEC_BATCH, DEC_SEQ, T_CTX // TM
    else:
        n_batch, seq_len, row0 = BATCH, SEQ, 0
    n_seq = TM // seq_len
    tail = (HEAD_DIM, seq_len) if transposed else (seq_len, HEAD_DIM)
    kern = functools.partial(_mm_heads_kernel, n_seq=n_seq, seq_len=seq_len, scale=scale, transposed=transposed)
    in_specs = [
        pl.BlockSpec((TM, D), lambda i: (row0 + i, 0)),
        pl.BlockSpec((1, D, D), lambda i: (idx, 0, part)),
    ]
    args = [hn, w_qkv]
    aliases = {}
    if into is not None:
        in_specs.append(pl.BlockSpec(memory_space=pl.ANY))
        args.append(into)
        aliases = {2: 0}
        body = lambda a, w, _, o, wb, acc: kern(a, w, o, wb, acc)
    else:
        body = kern
    return pl.pallas_call(
        body,
        grid=(n_batch // n_seq,),
        in_specs=in_specs,
        out_specs=pl.BlockSpec((n_seq, 1, N_HEADS) + tail, lambda i: (i, slot, 0, 0, 0)),
        out_shape=jax.ShapeDtypeStruct((n_batch, n_slots, N_HEADS) + tail, out_dtype),
        scratch_shapes=[pltpu.VMEM((D, D), BF16), pltpu.VMEM((TM, D), F32)],
        input_output_aliases=aliases,
        compiler_params=_params(("arbitrary",)),
        name="qkv_heads",
    )(*args)


def _dot_nt(a, b):
    return lax.dot_general(a, b, (((1,), (1,)), ((), ())), preferred_element_type=F32)


def _softmax_pv(s_list, v_list, v_transposed=False):
    m = s_list[0].max(axis=-1, keepdims=True)
    for s in s_list[1:]:
        m = jnp.maximum(m, s.max(axis=-1, keepdims=True))
    den = None
    out = None
    for s, v in zip(s_list, v_list):
        p = jnp.exp(s - m)
        d = p.sum(axis=-1, keepdims=True)
        pb = p.astype(BF16)
        o = _dot_nt(pb, v) if v_transposed else jnp.dot(pb, v, preferred_element_type=F32)
        den = d if den is None else den + d
        out = o if out is None else out + o
    return out / den


HEAD_PAIR_W = 2 * HEAD_DIM


def _ctx_attn_kernel(q_ref, kt_ref, vt_ref, o_ref):
    for hp in range(N_HEADS // 2):
        pair = []
        for h in (2 * hp, 2 * hp + 1):
            s = jnp.dot(q_ref[0, 0, h], kt_ref[0, 0, h].astype(BF16), preferred_element_type=F32)
            pair.append(_softmax_pv([s], [vt_ref[0, 0, h].astype(BF16)], v_transposed=True))
        o_ref[:, hp * HEAD_PAIR_W:(hp + 1) * HEAD_PAIR_W] = jnp.concatenate(pair, axis=-1).astype(o_ref.dtype)


def _context_attention(q, kt, vt, slot):
    kv_spec = pl.BlockSpec((1, 1, N_HEADS, HEAD_DIM, SEQ), lambda b: (b, slot, 0, 0, 0))
    return pl.pallas_call(
        _ctx_attn_kernel,
        grid=(BATCH,),
        in_specs=[pl.BlockSpec((1, 1, N_HEADS, SEQ, HEAD_DIM), lambda b: (b, 0, 0, 0, 0)), kv_spec, kv_spec],
        out_specs=pl.BlockSpec((SEQ, D), lambda b: (b, 0)),
        out_shape=jax.ShapeDtypeStruct((T_CTX, D), BF16),
        compiler_params=_params(("arbitrary",)),
        name="ctx_attention",
    )(q, kt, vt)


def _row_start(r):
    return min(max(r - WIN_R // 2, 0), GRID_ROWS - WIN_R)


def _nbr_attn_kernel(q_ref, k_ref, v_ref, kct_ref, vct_ref, bias_ref, o_ref):
    kct = [kct_ref[0, 0, hh].astype(BF16) for hh in range(2)]
    vct = [vct_ref[0, 0, hh].astype(BF16) for hh in range(2)]
    for r0, r1 in _ROW_GROUPS:
        rs = _row_start(r0)
        n_q = (r1 - r0) * GRID_W
        pair = []
        for hh in range(2):
            q = q_ref[0, 0, hh, r0 * GRID_W:r1 * GRID_W, :]
            kw = k_ref[0, 0, hh, rs * GRID_W:(rs + WIN_R) * GRID_W, :].astype(BF16)
            vw = v_ref[0, 0, hh, rs * GRID_W:(rs + WIN_R) * GRID_W, :].astype(BF16)
            bias = bias_ref[hh, r0 - rs:r1 - rs].reshape(n_q, WIN_R * GRID_W)
            s_win = _dot_nt(q, kw) + bias
            s_ctx = jnp.dot(q, kct[hh], preferred_element_type=F32)
            m = jnp.maximum(s_win.max(axis=-1, keepdims=True), s_ctx.max(axis=-1, keepdims=True))
            p_win = jnp.exp(s_win - m)
            p_ctx = jnp.exp(s_ctx - m)
            den = p_win.sum(axis=-1, keepdims=True) + p_ctx.sum(axis=-1, keepdims=True)
            out = (jnp.dot(p_win.astype(BF16), vw, preferred_element_type=F32)
                   + _dot_nt(p_ctx.astype(BF16), vct[hh]))
            pair.append(out / den)
        o_ref[r0 * GRID_W:r1 * GRID_W, :] = jnp.concatenate(pair, axis=-1).astype(o_ref.dtype)


def _row_groups():
    groups, r0 = [], 0
    for r in range(1, GRID_ROWS + 1):
        if r == GRID_ROWS or _row_start(r) != _row_start(r0):
            groups.append((r0, r))
            r0 = r
    return groups


_ROW_GROUPS = _row_groups()


def _window_bias(rpb_l):
    col = np.arange(GRID_W)
    col_start = np.clip(col - WIN_C // 2, 0, GRID_W - WIN_C)
    col_mask = (col[None, :] >= col_start[:, None]) & (col[None, :] < col_start[:, None] + WIN_C)
    dc = np.clip(col[None, :] - col[:, None], -(WIN_C - 1), WIN_C - 1) + (WIN_C - 1)
    onehot = (dc[None, :, :] == np.arange(2 * WIN_C - 1)[:, None, None]).astype(np.float32)
    rows = jnp.stack([rpb_l[:, WIN_R - 1 - v:2 * WIN_R - 1 - v, :] for v in range(WIN_R)], axis=1)
    bias = jnp.einsum('hvkd,dqc->hvqkc', rows, jnp.asarray(onehot), precision=lax.Precision.HIGHEST)
    bias = jnp.where(jnp.asarray(col_mask)[None, None, :, None, :], bias, NEG_INF)
    return bias.reshape(N_HEADS, WIN_R, GRID_W, WIN_R * GRID_W)


def _neighbourhood_attention(q, k, v, cache_kt, cache_vt, bias, slot):
    lat = lambda: pl.BlockSpec((1, 1, 2, DEC_SEQ, HEAD_DIM), lambda b, hp: (b, 0, hp, 0, 0))
    ctx = lambda: pl.BlockSpec((1, 1, 2, HEAD_DIM, PAST_LEN), lambda b, hp: (b, slot, hp, 0, 0))
    return pl.pallas_call(
        _nbr_attn_kernel,
        grid=(DEC_BATCH, N_HEADS // 2),
        in_specs=[lat(), lat(), lat(), ctx(), ctx(),
                  pl.BlockSpec((2, WIN_R, GRID_W, WIN_R * GRID_W), lambda b, hp: (hp, 0, 0, 0))],
        out_specs=pl.BlockSpec((DEC_SEQ, HEAD_PAIR_W), lambda b, hp: (b, hp)),
        out_shape=jax.ShapeDtypeStruct((T_LAT, D), BF16),
        compiler_params=_params(("arbitrary", "arbitrary")),
        name="nbr_attention",
    )(q, k, v, cache_kt, cache_vt, bias)


def _top2(logits):
    col = lax.broadcasted_iota(jnp.int32, logits.shape, 1)
    m1 = jnp.max(logits, axis=-1, keepdims=True)
    i1 = jnp.min(jnp.where(logits == m1, col, N_EXPERTS), axis=-1, keepdims=True)
    rest = jnp.where(col == i1, -jnp.inf, logits)
    m2 = jnp.max(rest, axis=-1, keepdims=True)
    i2 = jnp.min(jnp.where(rest == m2, col, N_EXPERTS), axis=-1, keepdims=True)
    e = jnp.exp(m2 - m1)
    return i1, i2, 1.0 / (1.0 + e), e / (1.0 + e)


CTX_TILES = T_CTX // TM


def _proj_res_norm_kernel(ac_ref, al_ref, w_ref, x_ref, g1_ref, ng_ref, sc_ref, sh_ref, *rest, router):
    if router:
        wr_ref, xo_ref, hn_ref, idx_ref, gate_ref, wb_ref = rest
    else:
        xo_ref, hn_ref, wb_ref = rest

    @pl.when(pl.program_id(0) == 0)
    def _():
        wb_ref[...] = w_ref[0].astype(BF16)

    a = jnp.where(pl.program_id(0) < CTX_TILES, ac_ref[...], al_ref[...])
    mix = jnp.dot(a, wb_ref[...], preferred_element_type=F32)
    x = x_ref[...] + g1_ref[0, 0, 0] * mix
    xo_ref[...] = x
    hn = _rms_mod(x, ng_ref[...], sc_ref[0, 0, 0], sh_ref[0, 0, 0])
    hn_ref[...] = hn.astype(hn_ref.dtype)
    if router:
        hi = hn.astype(BF16)
        lo = (hn - hi.astype(F32)).astype(BF16)
        l_hi = jnp.dot(hi, wr_ref[...], preferred_element_type=F32)
        l_lo = jnp.dot(lo, wr_ref[...], preferred_element_type=F32)
        logits = l_hi[:, :N_EXPERTS] + l_hi[:, N_EXPERTS:] + l_lo[:, :N_EXPERTS]
        i1, i2, p1, p2 = _top2(logits)
        first = lax.broadcasted_iota(jnp.int32, (logits.shape[0], 2), 1) == 0
        idx_ref[...] = jnp.where(first, i1, i2)
        gate_ref[...] = jnp.where(first, p1, p2)


def _proj_res_norm(a_ctx, a_lat, w, idx, x, norm_g, mods, layer, w_router=None):
    router = w_router is not None
    row = lambda n: pl.BlockSpec((TM, n), lambda i: (i, 0))
    in_specs = [pl.BlockSpec((TM, D), lambda i: (jnp.minimum(i, CTX_TILES - 1), 0)),
                pl.BlockSpec((TM, D), lambda i: (jnp.maximum(i - CTX_TILES, 0), 0)),
                pl.BlockSpec((1, D, D), lambda i: (idx, 0, 0)), row(D),
                _mod_spec(layer, 2), pl.BlockSpec((1, D), lambda i: (0, 0)), _mod_spec(layer, 4), _mod_spec(layer, 3)]
    args = [a_ctx, a_lat, w, x, mods, norm_g.reshape(1, D), mods, mods]
    out_specs = [row(D), row(D)]
    out_shape = [jax.ShapeDtypeStruct((T, D), F32), jax.ShapeDtypeStruct((T, D), F32 if router else BF16)]
    if router:
        w_hi = w_router.astype(BF16)
        w_lo = (w_router - w_hi.astype(F32)).astype(BF16)
        in_specs.append(pl.BlockSpec((D, 2 * N_EXPERTS), lambda i: (0, 0)))
        args.append(jnp.concatenate([w_hi, w_lo], axis=1))
        out_specs += [row(2), row(2)]
        out_shape += [jax.ShapeDtypeStruct((T, 2), jnp.int32), jax.ShapeDtypeStruct((T, 2), F32)]
    return pl.pallas_call(
        functools.partial(_proj_res_norm_kernel, router=router),
        grid=(T // TM,),
        in_specs=in_specs,
        out_specs=out_specs,
        out_shape=out_shape,
        scratch_shapes=[pltpu.VMEM((D, D), BF16)],
        compiler_params=_params(("arbitrary",)),
        name="proj_res_norm",
    )(*args)


def _swiglu_rows(hn_ref, w_hbm, sel, wbuf, wsem, acc_ref):
    wg_hbm, wu_hbm, wd_hbm = w_hbm
    bg, bu, bd = wbuf

    def chunk_copies(f, slot):
        cols = pl.ds(pl.multiple_of(f * TF, TF), TF)
        return (pltpu.make_async_copy(wg_hbm.at[sel + (slice(None), cols)], bg.at[slot], wsem.at[0, slot]),
                pltpu.make_async_copy(wu_hbm.at[sel + (slice(None), cols)], bu.at[slot], wsem.at[1, slot]),
                pltpu.make_async_copy(wd_hbm.at[sel + (cols, slice(None))], bd.at[slot], wsem.at[2, slot]))

    for cp in chunk_copies(0, 0):
        cp.start()
    acc_ref[...] = jnp.zeros_like(acc_ref)

    def body(f, carry):
        slot = lax.rem(f, 2)

        @pl.when(f + 1 < N_FCH)
        def _():
            for cp in chunk_copies(f + 1, 1 - slot):
                cp.start()

        for cp in chunk_copies(f, slot):
            cp.wait()
        hn = hn_ref[...]
        g = jnp.dot(hn, bg[slot].astype(BF16), preferred_element_type=F32)
        u = jnp.dot(hn, bu[slot].astype(BF16), preferred_element_type=F32)
        h = (g * jax.nn.sigmoid(g) * u).astype(BF16)
        acc_ref[...] += jnp.dot(h, bd[slot].astype(BF16), preferred_element_type=F32)
        return carry

    lax.fori_loop(0, N_FCH, body, 0)


def _weight_scratch():
    return [pltpu.VMEM((2, D, TF), F32), pltpu.VMEM((2, D, TF), F32), pltpu.VMEM((2, TF, D), F32),
            pltpu.SemaphoreType.DMA((3, 2))]


def _ffn_kernel(hn_ref, wg_hbm, wu_hbm, wd_hbm, x_ref, g2_ref, ng_ref, sc_ref, sh_ref, xo_ref, hno_ref,
                acc_ref, bg, bu, bd, wsem, *, idx):
    _swiglu_rows(hn_ref, (wg_hbm, wu_hbm, wd_hbm), (idx,), (bg, bu, bd), wsem, acc_ref)
    x = x_ref[...] + g2_ref[0, 0, 0] * acc_ref[...]
    xo_ref[...] = x
    hno_ref[...] = _rms_mod(x, ng_ref[...], sc_ref[0, 0, 0], sh_ref[0, 0, 0]).astype(hno_ref.dtype)


def _ffn(hn, wg, wu, wd, idx, x, mods, layer, next_g):
    row = lambda: pl.BlockSpec((TM, D), lambda i: (i, 0))
    hbm = lambda: pl.BlockSpec(memory_space=pl.ANY)
    return pl.pallas_call(
        functools.partial(_ffn_kernel, idx=idx),
        grid=(T // TM,),
        in_specs=[row(), hbm(), hbm(), hbm(),
                  row(), _mod_spec(layer, 5), pl.BlockSpec((1, D), lambda i: (0, 0)),
                  _mod_spec(layer + 1, 1), _mod_spec(layer + 1, 0)],
        out_specs=[row(), row()],
        out_shape=[jax.ShapeDtypeStruct((T, D), F32), jax.ShapeDtypeStruct((T, D), BF16)],
        scratch_shapes=[pltpu.VMEM((TM, D), F32)] + _weight_scratch(),
        compiler_params=_params(("arbitrary",)),
        name="ffn",
    )(hn, wg, wu, wd, x, mods, next_g.reshape(1, D), mods, mods)


ROW_UNROLL = 8
SCATTER_TM = 512
COMBINE_TM = 512


def _wait_rows(src_ref, dst_ref, sem, n_rows):
    pltpu.make_async_copy(src_ref.at[pl.ds(0, n_rows)], dst_ref.at[pl.ds(0, n_rows)], sem).wait()


def _scatter_kernel(dest_ref, hn_ref, xs_in_ref, xs_ref, sem):
    del xs_in_ref
    base = pl.program_id(0) * (2 * SCATTER_TM)

    def body(g, carry):
        for u in range(ROW_UNROLL):
            r = g * ROW_UNROLL + u
            for k in range(2):
                d = dest_ref[base + 2 * r + k]
                pltpu.make_async_copy(hn_ref.at[pl.ds(r, 1)], xs_ref.at[pl.ds(d, 1)], sem).start()
        return carry

    lax.fori_loop(0, SCATTER_TM // ROW_UNROLL, body, 0)
    for _ in range(2):
        _wait_rows(hn_ref, xs_ref, sem, SCATTER_TM)


def _moe_scatter(hn2, dest_flat):
    return pl.pallas_call(
        _scatter_kernel,
        grid_spec=pltpu.PrefetchScalarGridSpec(
            num_scalar_prefetch=1,
            grid=(T // SCATTER_TM,),
            in_specs=[pl.BlockSpec((SCATTER_TM, D), lambda j, dest: (j, 0)),
                      pl.BlockSpec(memory_space=pl.ANY)],
            out_specs=pl.BlockSpec(memory_space=pl.ANY),
            scratch_shapes=[pltpu.SemaphoreType.DMA],
        ),
        out_shape=jax.ShapeDtypeStruct((MOE_ROWS, D), F32),
        input_output_aliases={2: 0},
        compiler_params=_params(("arbitrary",)),
        name="moe_scatter",
    )(dest_flat, hn2, jnp.zeros((MOE_ROWS, D), F32))


def _moe_kernel(te_ref, nu_ref, xs_ref, wg_hbm, wu_hbm, wd_hbm, y_ref, xsb_ref, bg, bu, bd, wsem, *, idx):
    i = pl.program_id(0)

    @pl.when(i < nu_ref[0])
    def _():
        xsb_ref[...] = xs_ref[...].astype(BF16)
        _swiglu_rows(xsb_ref, (wg_hbm, wu_hbm, wd_hbm), (idx, te_ref[i]), (bg, bu, bd), wsem, y_ref)

    @pl.when(i >= nu_ref[0])
    def _():
        y_ref[...] = jnp.zeros_like(y_ref)


def _moe_experts(xs, tile_expert, n_used, wg, wu, wd, idx):
    hbm = lambda: pl.BlockSpec(memory_space=pl.ANY)
    return pl.pallas_call(
        functools.partial(_moe_kernel, idx=idx),
        grid_spec=pltpu.PrefetchScalarGridSpec(
            num_scalar_prefetch=2,
            grid=(MOE_TILES,),
            in_specs=[pl.BlockSpec((MOE_TM, D), lambda i, te, nu: (jnp.minimum(i, nu[0] - 1), 0)),
                      hbm(), hbm(), hbm()],
            out_specs=pl.BlockSpec((MOE_TM, D), lambda i, te, nu: (i, 0)),
            scratch_shapes=[pltpu.VMEM((MOE_TM, D), BF16)] + _weight_scratch(),
        ),
        out_shape=jax.ShapeDtypeStruct((MOE_ROWS, D), F32),
        compiler_params=_params(("arbitrary",)),
        name="moe_experts",
    )(tile_expert, n_used, xs, wg, wu, wd)


CTX_COMBINE_TILES = T_CTX // COMBINE_TM


def _combine_kernel(dest_ref, x_ref, y_hbm, gate_ref, g2_ref, ng_ref, sc_ref, sh_ref, *rest, final):
    *outs, ybuf, sem = rest
    j = pl.program_id(0)
    slot = j % 2

    def issue(tile, s):
        base = tile * (2 * COMBINE_TM)

        def body(g, carry):
            for u in range(ROW_UNROLL):
                r = g * ROW_UNROLL + u
                for k in range(2):
                    d = dest_ref[base + 2 * r + k]
                    pltpu.make_async_copy(y_hbm.at[pl.ds(d, 1)], ybuf.at[s, pl.ds(k * COMBINE_TM + r, 1)],
                                          sem.at[s]).start()
            return carry

        lax.fori_loop(0, COMBINE_TM // ROW_UNROLL, body, 0)

    @pl.when(j == 0)
    def _():
        issue(0, 0)

    @pl.when(j + 1 < pl.num_programs(0))
    def _():
        issue(j + 1, 1 - slot)

    _wait_rows(y_hbm, ybuf.at[slot], sem.at[slot], 2 * COMBINE_TM)
    gates = gate_ref[...]
    moe = gates[:, 0:1] * ybuf[slot, 0:COMBINE_TM] + gates[:, 1:2] * ybuf[slot, COMBINE_TM:2 * COMBINE_TM]
    x = x_ref[...] + g2_ref[0, 0, 0] * moe
    if final:
        yp_ref, ys_ref = outs
        y = _rms_mod(x, ng_ref[...], 0.0, 0.0)

        @pl.when(j < CTX_COMBINE_TILES)
        def _():
            yp_ref[...] = y

        @pl.when(j >= CTX_COMBINE_TILES)
        def _():
            ys_ref[...] = y
    else:
        xo_ref, hno_ref = outs
        xo_ref[...] = x
        hno_ref[...] = _rms_mod(x, ng_ref[...], sc_ref[0, 0, 0], sh_ref[0, 0, 0]).astype(hno_ref.dtype)


def _moe_combine(x, y, dest_flat, gates, mods, layer, next_g, final):
    row = lambda n=D: pl.BlockSpec((COMBINE_TM, n), lambda j, dest: (j, 0))
    vec = lambda: pl.BlockSpec((1, D), lambda j, dest: (0, 0))
    mod = lambda l, k: pl.BlockSpec((1, 1, 1, 1, D), lambda j, dest: (l, k, _cond_of_tile(j, COMBINE_TM), 0, 0))
    nl = layer if final else layer + 1
    if final:
        out_specs = [pl.BlockSpec((COMBINE_TM, D), lambda j, dest: (jnp.minimum(j, CTX_COMBINE_TILES - 1), 0)),
                     pl.BlockSpec((COMBINE_TM, D), lambda j, dest: (jnp.maximum(j - CTX_COMBINE_TILES, 0), 0))]
        out_shape = [jax.ShapeDtypeStruct((T_CTX, D), F32), jax.ShapeDtypeStruct((T_LAT, D), F32)]
    else:
        out_specs = [row(), row()]
        out_shape = [jax.ShapeDtypeStruct((T, D), F32), jax.ShapeDtypeStruct((T, D), BF16)]
    return pl.pallas_call(
        functools.partial(_combine_kernel, final=final),
        grid_spec=pltpu.PrefetchScalarGridSpec(
            num_scalar_prefetch=1,
            grid=(T // COMBINE_TM,),
            in_specs=[row(), pl.BlockSpec(memory_space=pl.ANY), row(2), mod(layer, 5), vec(), mod(nl, 1), mod(nl, 0)],
            out_specs=out_specs,
            scratch_shapes=[pltpu.VMEM((2, 2 * COMBINE_TM, D), F32), pltpu.SemaphoreType.DMA((2,))],
        ),
        out_shape=out_shape,
        compiler_params=_params(("arbitrary",)),
        name="moe_combine",
    )(dest_flat, x, y, gates, mods, next_g.reshape(1, D), mods, mods)


LRU_CHUNK = 256


def _shift_rows(v, s, row):
    n = v.shape[0]
    rolled = pltpu.roll(v, s % n, axis=0)
    keep = (row >= s) if s > 0 else (row < n + s)
    return jnp.where(keep, rolled, 0.0)


W_IN_CHUNK = 512
N_W_IN_CHUNKS = IN_W // W_IN_CHUNK


def _lru_kernel(hn_ref, h0_ref, win_hbm, cw_ref, cb_ref, wbd_ref, bbd_ref, lam_ref, scw_ref, scb_ref,
                y_ref, st_ref, winb_ref, stage_ref, stage_sem, proj_ref, xc_ref, a_ref, b_ref, *, seq_len, idx):
    @pl.when(pl.program_id(0) == 0)
    def _():
        copies = [pltpu.make_async_copy(win_hbm.at[idx, :, pl.ds(j * W_IN_CHUNK, W_IN_CHUNK)],
                                        stage_ref.at[j % 2], stage_sem.at[j % 2]) for j in range(N_W_IN_CHUNKS)]
        copies[0].start()
        for j in range(N_W_IN_CHUNKS):
            if j + 1 < N_W_IN_CHUNKS:
                copies[j + 1].start()
            copies[j].wait()
            winb_ref[:, j * W_IN_CHUNK:(j + 1) * W_IN_CHUNK] = stage_ref[j % 2].astype(BF16)

    proj_ref[...] = jnp.dot(hn_ref[...], winb_ref[...], preferred_element_type=F32)
    row = lax.broadcasted_iota(jnp.int32, (seq_len, 1), 0)
    x = proj_ref[:, 0:LRU_W]
    xc =(cb_ref[...] + _shift_rows(x, 2, row) * cw_ref[0:1] + _shift_rows(x, 1, row) * cw_ref[1:2]
          + x * cw_ref[2:3] + _shift_rows(x, -1, row) * cw_ref[3:4])
    xc_ref[...] = xc

    lam = lam_ref[...]
    neg_csp = -LRU_C * (jnp.maximum(-lam, 0.0) + jnp.log1p(jnp.exp(-jnp.abs(lam))))

    for c in range(seq_len // LRU_CHUNK):
        rows = pl.ds(c * LRU_CHUNK, LRU_CHUNK)
        xcc = xc_ref[rows, :]
        gates = jnp.dot(xcc.astype(BF16), wbd_ref[...], preferred_element_type=F32) + bbd_ref[...]
        for d in range(2):
            r = jax.nn.sigmoid(gates[:, (2 * d) * LRU_W:(2 * d + 1) * LRU_W])
            i = jax.nn.sigmoid(gates[:, (2 * d + 1) * LRU_W:(2 * d + 2) * LRU_W])
            log_a = neg_csp[d:d + 1] * r
            t = jnp.tanh(log_a)
            one_minus_a2 = -2.0 * t / (1.0 - t)
            a_ref[d, rows, :] = jnp.exp(log_a)
            b_ref[d, rows, :] = jnp.sqrt(one_minus_a2) * (i * xcc)

    n_grp = seq_len // 8

    def scan_body(g, carry):
        hf, hb = carry
        rf = pl.ds(pl.multiple_of(g * 8, 8), 8)
        rb = pl.ds(pl.multiple_of((n_grp - 1 - g) * 8, 8), 8)
        af, bf = a_ref[0, rf, :], b_ref[0, rf, :]
        ab, bb = a_ref[1, rb, :], b_ref[1, rb, :]
        outs_f, outs_b = [], []
        for j in range(8):
            hf = af[j:j + 1] * hf + bf[j:j + 1]
            outs_f.append(hf)
            hb = ab[7 - j:8 - j] * hb + bb[7 - j:8 - j]
            outs_b.append(hb)
        b_ref[0, rf, :] = jnp.concatenate(outs_f, axis=0)
        b_ref[1, rb, :] = jnp.concatenate(outs_b[::-1], axis=0)
        return hf, hb

    hf, hb = lax.fori_loop(0, n_grp, scan_body, (h0_ref[0, 0:1, :], h0_ref[0, 1:2, :]))
    st_ref[0, 0:1, :] = hf
    st_ref[0, 1:2, :] = hb

    h_sum = b_ref[0] + b_ref[1]
    y_ref[:, 0:LRU_W] = (h_sum * jax.nn.gelu(proj_ref[:, LRU_W:2 * LRU_W])).astype(y_ref.dtype)
    o = 2 * LRU_W
    cv = proj_ref[:, o + SC_W:o + 2 * SC_W] * proj_ref[:, o + 2 * SC_W:o + 3 * SC_W]
    conv = scb_ref[...] + _shift_rows(cv, 1, row) * scw_ref[0:1] + cv * scw_ref[1:2] + _shift_rows(cv, -1, row) * scw_ref[2:3]
    y_ref[:, LRU_W:LRU_W + SC_W] = (proj_ref[:, o:o + SC_W] * conv).astype(y_ref.dtype)


def _lru_mixer(hn, h0, w_in, idx, cw, cb, wbd, bbd, lam, scw, scb, *, latent):
    if latent:
        n_seq, seq_len, blk0 = DEC_BATCH, DEC_SEQ, T_CTX // DEC_SEQ
    else:
        n_seq, seq_len, blk0 = BATCH, SEQ, 0
    full = lambda shape: pl.BlockSpec(shape, lambda b: (0,) * len(shape))
    return pl.pallas_call(
        functools.partial(_lru_kernel, seq_len=seq_len, idx=idx),
        grid=(n_seq,),
        in_specs=[pl.BlockSpec((seq_len, D), lambda b: (blk0 + b, 0)),
                  pl.BlockSpec((1, 2, LRU_W), lambda b: (b, 0, 0)),
                  pl.BlockSpec(memory_space=pl.ANY),
                  full((4, LRU_W)), full((1, LRU_W)), full((LRU_W, 4 * LRU_W)), full((1, 4 * LRU_W)),
                  full((2, LRU_W)), full((3, SC_W)), full((1, SC_W))],
        out_specs=[pl.BlockSpec((seq_len, D), lambda b: (b, 0)),
                   pl.BlockSpec((1, 2, LRU_W), lambda b: (b, 0, 0))],
        out_shape=[jax.ShapeDtypeStruct((n_seq * seq_len, D), BF16),
                   jax.ShapeDtypeStruct((n_seq, 2, LRU_W), F32)],
        scratch_shapes=[pltpu.VMEM((D, IN_W), BF16),
                        pltpu.VMEM((2, D, W_IN_CHUNK), F32),
                        pltpu.SemaphoreType.DMA((2,)),
                        pltpu.VMEM((seq_len, IN_W), F32),
                        pltpu.VMEM((seq_len, LRU_W), F32),
                        pltpu.VMEM((2, seq_len, LRU_W), F32),
                        pltpu.VMEM((2, seq_len, LRU_W), F32)],
        compiler_params=_params(("arbitrary",)),
        name="lru_mixer",
    )(hn, h0, w_in, cw, cb.reshape(1, LRU_W), wbd, bbd, lam, scw, scb.reshape(1, SC_W))


def _block_diag_dense(w):
    eye = jnp.eye(LRU_HEADS, dtype=w.dtype)
    return jnp.einsum('hij,hg->higj', w, eye).reshape(LRU_W, LRU_W)


def _route(idx):
    e_flat = idx.reshape(-1)
    onehot = (e_flat[:, None] == jnp.arange(N_EXPERTS)[None, :]).astype(jnp.int32)
    csum = jnp.cumsum(onehot, axis=0)
    counts = csum[-1]
    rank = jnp.take_along_axis(csum, e_flat[:, None], axis=1)[:, 0] - 1
    padded = (counts + MOE_TM - 1) // MOE_TM * MOE_TM
    ends = jnp.cumsum(padded)
    dest = ((ends - padded)[e_flat] + rank).astype(jnp.int32)
    tile_end = ends // MOE_TM
    n_used = tile_end[-1]
    tiles = jnp.minimum(jnp.arange(MOE_TILES), n_used - 1)
    tile_expert = jnp.sum(tiles[:, None] >= tile_end[None, :], axis=1).astype(jnp.int32)
    return dest, tile_expert, n_used.reshape(1).astype(jnp.int32)


def kernel(x_prompt, x_sample, state_lru, cache_k, cache_v, c, c_ctx, norm1_g, norm2_g, w_mod, b_mod, w_in, lru_conv_w, lru_conv_b, lru_wa, lru_ba, lru_wx, lru_bx, lru_lam, sc_conv_w, sc_conv_b, w_out, ffn_w_gate, ffn_w_up, ffn_w_down, w_qkv, w_o, rpb, w_router, moe_w_gate, moe_w_up, moe_w_down, final_g):
    cond = jnp.concatenate([c_ctx[None, :], c, jnp.zeros((COND_ROWS - N_COND, D), F32)], axis=0)
    mods = _modulation(cond, w_mod, b_mod).reshape(DEPTH, 6, COND_ROWS, 1, D)

    x, hn = _first_norm(x_prompt.reshape(T_CTX, D), x_sample.reshape(T_LAT, D), norm1_g[0], mods, 0)
    cache_kt = jnp.swapaxes(cache_k, -1, -2)
    cache_vt = jnp.swapaxes(cache_v, -1, -2)
    zero_state = jnp.zeros((BATCH, 2, LRU_W), F32)
    states = []
    new_k = jnp.zeros((BATCH, DEPTH // 2, N_HEADS, HEAD_DIM, SEQ), F32)
    new_v = jnp.zeros((BATCH, DEPTH // 2, N_HEADS, HEAD_DIM, SEQ), F32)
    for l in range(DEPTH):
        idx = l // 2
        if l % 2 == 0:
            wbd = jnp.concatenate([_block_diag_dense(lru_wa[idx, 0]), _block_diag_dense(lru_wx[idx, 0]),
                                   _block_diag_dense(lru_wa[idx, 1]), _block_diag_dense(lru_wx[idx, 1])],
                                  axis=1).astype(BF16)
            bbd = jnp.concatenate([lru_ba[idx, 0], lru_bx[idx, 0], lru_ba[idx, 1], lru_bx[idx, 1]]).reshape(1, 4 * LRU_W)
            lru_args = (lru_conv_w[idx], lru_conv_b[idx], wbd, bbd, lru_lam[idx], sc_conv_w[idx], sc_conv_b[idx])
            y_ctx, st = _lru_mixer(hn, zero_state, w_in, idx, *lru_args, latent=False)
            y_lat, _ = _lru_mixer(hn, state_lru[:, idx], w_in, idx, *lru_args, latent=True)
            states.append(st)
            x, hn2 = _proj_res_norm(y_ctx, y_lat, w_out, idx, x, norm2_g[l], mods, l)
            x, hn = _ffn(hn2, ffn_w_gate, ffn_w_up, ffn_w_down, idx, x, mods, l, norm1_g[l + 1])
        else:
            q_scale = HEAD_DIM ** -0.5
            q_ctx = _matmul_heads(hn, w_qkv, idx, 0, latent=False, out_dtype=BF16, scale=q_scale)
            new_k = _matmul_heads(hn, w_qkv, idx, 1, latent=False, out_dtype=F32, into=new_k, slot=idx, n_slots=2,
                                  transposed=True)
            new_v = _matmul_heads(hn, w_qkv, idx, 2, latent=False, out_dtype=F32, into=new_v, slot=idx, n_slots=2,
                                  transposed=True)
            q_lat = _matmul_heads(hn, w_qkv, idx, 0, latent=True, out_dtype=BF16, scale=q_scale)
            k_lat = _matmul_heads(hn, w_qkv, idx, 1, latent=True, out_dtype=F32)
            v_lat = _matmul_heads(hn, w_qkv, idx, 2, latent=True, out_dtype=F32)
            o_ctx = _context_attention(q_ctx, new_k, new_v, idx)
            o_lat = _neighbourhood_attention(q_lat, k_lat, v_lat, cache_kt, cache_vt, _window_bias(rpb[idx]), idx)
            x, hn2, top_i, top_p = _proj_res_norm(o_ctx, o_lat, w_o, idx, x, norm2_g[l], mods, l, w_router[idx])
            dest, tile_expert, n_used = _route(top_i)
            xs = _moe_scatter(hn2, dest)
            y = _moe_experts(xs, tile_expert, n_used, moe_w_gate, moe_w_up, moe_w_down, idx)
            if l == DEPTH - 1:
                y_prompt, y_sample = _moe_combine(x, y, dest, top_p, mods, l, final_g, True)
            else:
                x, hn = _moe_combine(x, y, dest, top_p, mods, l, norm1_g[l + 1], False)

    return (y_prompt.reshape(BATCH, SEQ, D), y_sample.reshape(DEC_BATCH, DEC_SEQ, D), jnp.stack(states, axis=1),
            jnp.swapaxes(new_k, -1, -2), jnp.swapaxes(new_v, -1, -2))
```

```python
import functools

import jax
import jax.numpy as jnp
import numpy as np
from jax import lax
from jax.experimental import pallas as pl
from jax.experimental.pallas import tpu as pltpu

F32 = jnp.float32
BF16 = jnp.bfloat16

D = 1024
BATCH = 32
SEQ = 256
DEPTH = 4
DEC_BATCH = 2
DEC_SEQ = 1024
PAST_LEN = 256
GRID_W = 64
GRID_ROWS = DEC_SEQ // GRID_W
LRU_W = 512
LRU_HEADS = 8
LRU_BLOCK = LRU_W // LRU_HEADS
LRU_C = 8.0
SC_W = 512
IN_W = 2 * LRU_W + 3 * SC_W
N_HEADS = 16
HEAD_DIM = D // N_HEADS
WIN_R = 8
WIN_C = 16
D_FF = 2816
N_EXPERTS = 8
EPS = 1e-6
NEG_INF = -1e30

T_CTX = BATCH * SEQ
T_LAT = DEC_BATCH * DEC_SEQ
T = T_CTX + T_LAT
N_COND = 3
COND_ROWS = 8

TM = 1024
TF = 256
N_FCH = D_FF // TF
MOE_TM = 1024
MOE_TILES = (2 * T + N_EXPERTS * (MOE_TM - 1)) // MOE_TM
MOE_ROWS = MOE_TILES * MOE_TM
VMEM_LIMIT = 52 * 1024 * 1024


def _params(sem):
    return pltpu.CompilerParams(dimension_semantics=sem, vmem_limit_bytes=VMEM_LIMIT)


def _cond_of_tile(i, tm=TM):
    r = i * tm
    return jnp.where(r < T_CTX, 0, 1 + (r - T_CTX) // DEC_SEQ)


def _mod_spec(layer, k, grid_rank=1, tm=TM):
    if grid_rank == 1:
        return pl.BlockSpec((1, 1, 1, 1, D), lambda i: (layer, k, _cond_of_tile(i, tm), 0, 0))
    return pl.BlockSpec((1, 1, 1, 1, D), lambda i, f, *_: (layer, k, _cond_of_tile(i, tm), 0, 0))


def _rms_mod(x, g, scale, shift):
    y = x * lax.rsqrt(jnp.mean(x * x, axis=-1, keepdims=True) + EPS)
    return (y * g) * (1.0 + scale) + shift


def _mod_kernel(c_ref, w_ref, b_ref, o_ref):
    c = c_ref[...]
    s = c * jax.nn.sigmoid(c)
    o_ref[0, 0] = jnp.dot(s.astype(BF16), w_ref[0].astype(BF16), preferred_element_type=F32) + b_ref[0, 0]


def _modulation(cond, w_mod, b_mod):
    return pl.pallas_call(
        _mod_kernel,
        grid=(DEPTH, 6),
        in_specs=[
            pl.BlockSpec((COND_ROWS, D), lambda l, k: (0, 0)),
            pl.BlockSpec((1, D, D), lambda l, k: (l, 0, k)),
            pl.BlockSpec((1, 1, 1, D), lambda l, k: (l, k, 0, 0)),
        ],
        out_specs=pl.BlockSpec((1, 1, COND_ROWS, D), lambda l, k: (l, k, 0, 0)),
        out_shape=jax.ShapeDtypeStruct((DEPTH, 6, COND_ROWS, D), F32),
        compiler_params=_params(("arbitrary", "arbitrary")),
        name="adaln_mod",
    )(cond, w_mod, b_mod.reshape(DEPTH, 6, 1, D))


def _norm_kernel(xc_ref, xl_ref, g_ref, sc_ref, sh_ref, x_ref, o_ref):
    x = jnp.where(pl.program_id(0) < T_CTX // TM, xc_ref[...], xl_ref[...])
    x_ref[...] = x
    o_ref[...] = _rms_mod(x, g_ref[...], sc_ref[0, 0, 0], sh_ref[0, 0, 0]).astype(o_ref.dtype)


def _first_norm(x_ctx, x_lat, g, mods, layer):
    n_ctx = T_CTX // TM
    return pl.pallas_call(
        _norm_kernel,
        grid=(T // TM,),
        in_specs=[
            pl.BlockSpec((TM, D), lambda i: (jnp.minimum(i, n_ctx - 1), 0)),
            pl.BlockSpec((TM, D), lambda i: (jnp.maximum(i - n_ctx, 0), 0)),
            pl.BlockSpec((1, D), lambda i: (0, 0)),
            _mod_spec(layer, 1),
            _mod_spec(layer, 0),
        ],
        out_specs=[pl.BlockSpec((TM, D), lambda i: (i, 0)), pl.BlockSpec((TM, D), lambda i: (i, 0))],
        out_shape=[jax.ShapeDtypeStruct((T, D), F32), jax.ShapeDtypeStruct((T, D), BF16)],
        compiler_params=_params(("arbitrary",)),
        name="first_norm",
    )(x_ctx, x_lat, g.reshape(1, D), mods, mods)


def _mm_heads_kernel(a_ref, w_ref, o_ref, wb_ref, acc_ref, *, n_seq, seq_len, scale, transposed):
    @pl.when(pl.program_id(0) == 0)
    def _():
        wb_ref[...] = w_ref[0].astype(BF16)

    acc = jnp.dot(a_ref[...], wb_ref[...], preferred_element_type=F32)
    if scale != 1.0:
        acc = acc * scale
    if transposed:
        acc_ref[...] = acc
        acc_t = acc_ref[...].T
        for s in range(n_seq):
            for h in range(N_HEADS):
                o_ref[s, 0, h] = acc_t[h * HEAD_DIM:(h + 1) * HEAD_DIM,
                                       s * seq_len:(s + 1) * seq_len].astype(o_ref.dtype)
    else:
        for s in range(n_seq):
            for h in range(N_HEADS):
                o_ref[s, 0, h] = acc[s * seq_len:(s + 1) * seq_len,
                                     h * HEAD_DIM:(h + 1) * HEAD_DIM].astype(o_ref.dtype)


def _matmul_heads(hn, w_qkv, idx, part, *, latent, out_dtype, scale=1.0, into=None, slot=0, n_slots=1,
                  transposed=False):
    if latent:
        n_batch, seq_len, row0 = DEC_BATCH, DEC_SEQ, T_CTX // TM
    else:
        n_batch, seq_len, row0 = BATCH, SEQ, 0
    n_seq = TM // seq_len
    tail = (HEAD_DIM, seq_len) if transposed else (seq_len, HEAD_DIM)
    kern = functools.partial(_mm_heads_kernel, n_seq=n_seq, seq_len=seq_len, scale=scale, transposed=transposed)
    in_specs = [
        pl.BlockSpec((TM, D), lambda i: (row0 + i, 0)),
        pl.BlockSpec((1, D, D), lambda i: (idx, 0, part)),
    ]
    args = [hn, w_qkv]
    aliases = {}
    if into is not None:
        in_specs.append(pl.BlockSpec(memory_space=pl.ANY))
        args.append(into)
        aliases = {2: 0}
        body = lambda a, w, _, o, wb, acc: kern(a, w, o, wb, acc)
    else:
        body = kern
    return pl.pallas_call(
        body,
        grid=(n_batch // n_seq,),
        in_specs=in_specs,
        out_specs=pl.BlockSpec((n_seq, 1, N_HEADS) + tail, lambda i: (i, slot, 0, 0, 0)),
        out_shape=jax.ShapeDtypeStruct((n_batch, n_slots, N_HEADS) + tail, out_dtype),
        scratch_shapes=[pltpu.VMEM((D, D), BF16), pltpu.VMEM((TM, D), F32)],
        input_output_aliases=aliases,
        compiler_params=_params(("arbitrary",)),
        name="qkv_heads",
    )(*args)


def _dot_nt(a, b):
    return lax.dot_general(a, b, (((1,), (1,)), ((), ())), preferred_element_type=F32)


def _softmax_pv(s_list, v_list, v_transposed=False):
    m = s_list[0].max(axis=-1, keepdims=True)
    for s in s_list[1:]:
        m = jnp.maximum(m, s.max(axis=-1, keepdims=True))
    den = None
    out = None
    for s, v in zip(s_list, v_list):
        p = jnp.exp(s - m)
        d = p.sum(axis=-1, keepdims=True)
        pb = p.astype(BF16)
        o = _dot_nt(pb, v) if v_transposed else jnp.dot(pb, v, preferred_element_type=F32)
        den = d if den is None else den + d
        out = o if out is None else out + o
    return out / den


HEAD_PAIR_W = 2 * HEAD_DIM


def _ctx_attn_kernel(q_ref, kt_ref, vt_ref, o_ref):
    for hp in range(N_HEADS // 2):
        pair = []
        for h in (2 * hp, 2 * hp + 1):
            s = jnp.dot(q_ref[0, 0, h], kt_ref[0, 0, h].astype(BF16), preferred_element_type=F32)
            pair.append(_softmax_pv([s], [vt_ref[0, 0, h].astype(BF16)], v_transposed=True))
        o_ref[:, hp * HEAD_PAIR_W:(hp + 1) * HEAD_PAIR_W] = jnp.concatenate(pair, axis=-1).astype(o_ref.dtype)


def _context_attention(q, kt, vt, slot):
    kv_spec = pl.BlockSpec((1, 1, N_HEADS, HEAD_DIM, SEQ), lambda b: (b, slot, 0, 0, 0))
    return pl.pallas_call(
        _ctx_attn_kernel,
        grid=(BATCH,),
        in_specs=[pl.BlockSpec((1, 1, N_HEADS, SEQ, HEAD_DIM), lambda b: (b, 0, 0, 0, 0)), kv_spec, kv_spec],
        out_specs=pl.BlockSpec((SEQ, D), lambda b: (b, 0)),
        out_shape=jax.ShapeDtypeStruct((T_CTX, D), BF16),
        compiler_params=_params(("arbitrary",)),
        name="ctx_attention",
    )(q, kt, vt)


def _row_start(r):
    return min(max(r - WIN_R // 2, 0), GRID_ROWS - WIN_R)


def _nbr_attn_kernel(q_ref, k_ref, v_ref, kct_ref, vct_ref, bias_ref, o_ref):
    kct = [kct_ref[0, 0, hh].astype(BF16) for hh in range(2)]
    vct = [vct_ref[0, 0, hh].astype(BF16) for hh in range(2)]
    for r0, r1 in _ROW_GROUPS:
        rs = _row_start(r0)
        n_q = (r1 - r0) * GRID_W
        pair = []
        for hh in range(2):
            q = q_ref[0, 0, hh, r0 * GRID_W:r1 * GRID_W, :]
            kw = k_ref[0, 0, hh, rs * GRID_W:(rs + WIN_R) * GRID_W, :].astype(BF16)
            vw = v_ref[0, 0, hh, rs * GRID_W:(rs + WIN_R) * GRID_W, :].astype(BF16)
            bias = bias_ref[hh, r0 - rs:r1 - rs].reshape(n_q, WIN_R * GRID_W)
            s_win = _dot_nt(q, kw) + bias
            s_ctx = jnp.dot(q, kct[hh], preferred_element_type=F32)
            m = jnp.maximum(s_win.max(axis=-1, keepdims=True), s_ctx.max(axis=-1, keepdims=True))
            p_win = jnp.exp(s_win - m)
            p_ctx = jnp.exp(s_ctx - m)
            den = p_win.sum(axis=-1, keepdims=True) + p_ctx.sum(axis=-1, keepdims=True)
            out = (jnp.dot(p_win.astype(BF16), vw, preferred_element_type=F32)
                   + _dot_nt(p_ctx.astype(BF16), vct[hh]))
            pair.append(out / den)
        o_ref[r0 * GRID_W:r1 * GRID_W, :] = jnp.concatenate(pair, axis=-1).astype(o_ref.dtype)


def _row_groups():
    groups, r0 = [], 0
    for r in range(1, GRID_ROWS + 1):
        if r == GRID_ROWS or _row_start(r) != _row_start(r0):
            groups.append((r0, r))
            r0 = r
    return groups


_ROW_GROUPS = _row_groups()


def _window_bias(rpb_l):
    col = np.arange(GRID_W)
    col_start = np.clip(col - WIN_C // 2, 0, GRID_W - WIN_C)
    col_mask = (col[None, :] >= col_start[:, None]) & (col[None, :] < col_start[:, None] + WIN_C)
    dc = np.clip(col[None, :] - col[:, None], -(WIN_C - 1), WIN_C - 1) + (WIN_C - 1)
    onehot = (dc[None, :, :] == np.arange(2 * WIN_C - 1)[:, None, None]).astype(np.float32)
    rows = jnp.stack([rpb_l[:, WIN_R - 1 - v:2 * WIN_R - 1 - v, :] for v in range(WIN_R)], axis=1)
    bias = jnp.einsum('hvkd,dqc->hvqkc', rows, jnp.asarray(onehot), precision=lax.Precision.HIGHEST)
    bias = jnp.where(jnp.asarray(col_mask)[None, None, :, None, :], bias, NEG_INF)
    return bias.reshape(N_HEADS, WIN_R, GRID_W, WIN_R * GRID_W)


def _neighbourhood_attention(q, k, v, cache_kt, cache_vt, bias, slot):
    lat = lambda: pl.BlockSpec((1, 1, 2, DEC_SEQ, HEAD_DIM), lambda b, hp: (b, 0, hp, 0, 0))
    ctx = lambda: pl.BlockSpec((1, 1, 2, HEAD_DIM, PAST_LEN), lambda b, hp: (b, slot, hp, 0, 0))
    return pl.pallas_call(
        _nbr_attn_kernel,
        grid=(DEC_BATCH, N_HEADS // 2),
        in_specs=[lat(), lat(), lat(), ctx(), ctx(),
                  pl.BlockSpec((2, WIN_R, GRID_W, WIN_R * GRID_W), lambda b, hp: (hp, 0, 0, 0))],
        out_specs=pl.BlockSpec((DEC_SEQ, HEAD_PAIR_W), lambda b, hp: (b, hp)),
        out_shape=jax.ShapeDtypeStruct((T_LAT, D), BF16),
        compiler_params=_params(("arbitrary", "arbitrary")),
        name="nbr_attention",
    )(q, k, v, cache_kt, cache_vt, bias)


def _top2(logits):
    col = lax.broadcasted_iota(jnp.int32, logits.shape, 1)
    m1 = jnp.max(logits, axis=-1, keepdims=True)
    i1 = jnp.min(jnp.where(logits == m1, col, N_EXPERTS), axis=-1, keepdims=True)
    rest = jnp.where(col == i1, -jnp.inf, logits)
    m2 = jnp.max(rest, axis=-1, keepdims=True)
    i2 = jnp.min(jnp.where(rest == m2, col, N_EXPERTS), axis=-1, keepdims=True)
    e = jnp.exp(m2 - m1)
    return i1, i2, 1.0 / (1.0 + e), e / (1.0 + e)


CTX_TILES = T_CTX // TM


def _proj_res_norm_kernel(ac_ref, al_ref, w_ref, x_ref, g1_ref, ng_ref, sc_ref, sh_ref, *rest, router):
    if router:
        wr_ref, xo_ref, hn_ref, idx_ref, gate_ref, wb_ref = rest
    else:
        xo_ref, hn_ref, wb_ref = rest

    @pl.when(pl.program_id(0) == 0)
    def _():
        wb_ref[...] = w_ref[0].astype(BF16)

    a = jnp.where(pl.program_id(0) < CTX_TILES, ac_ref[...], al_ref[...])
    mix = jnp.dot(a, wb_ref[...], preferred_element_type=F32)
    x = x_ref[...] + g1_ref[0, 0, 0] * mix
    xo_ref[...] = x
    hn = _rms_mod(x, ng_ref[...], sc_ref[0, 0, 0], sh_ref[0, 0, 0])
    hn_ref[...] = hn.astype(hn_ref.dtype)
    if router:
        hi = hn.astype(BF16)
        lo = (hn - hi.astype(F32)).astype(BF16)
        l_hi = jnp.dot(hi, wr_ref[...], preferred_element_type=F32)
        l_lo = jnp.dot(lo, wr_ref[...], preferred_element_type=F32)
        logits = l_hi[:, :N_EXPERTS] + l_hi[:, N_EXPERTS:] + l_lo[:, :N_EXPERTS]
        i1, i2, p1, p2 = _top2(logits)
        first = lax.broadcasted_iota(jnp.int32, (logits.shape[0], 2), 1) == 0
        idx_ref[...] = jnp.where(first, i1, i2)
        gate_ref[...] = jnp.where(first, p1, p2)


def _proj_res_norm(a_ctx, a_lat, w, idx, x, norm_g, mods, layer, w_router=None):
    router = w_router is not None
    row = lambda n: pl.BlockSpec((TM, n), lambda i: (i, 0))
    in_specs = [pl.BlockSpec((TM, D), lambda i: (jnp.minimum(i, CTX_TILES - 1), 0)),
                pl.BlockSpec((TM, D), lambda i: (jnp.maximum(i - CTX_TILES, 0), 0)),
                pl.BlockSpec((1, D, D), lambda i: (idx, 0, 0)), row(D),
                _mod_spec(layer, 2), pl.BlockSpec((1, D), lambda i: (0, 0)), _mod_spec(layer, 4), _mod_spec(layer, 3)]
    args = [a_ctx, a_lat, w, x, mods, norm_g.reshape(1, D), mods, mods]
    out_specs = [row(D), row(D)]
    out_shape = [jax.ShapeDtypeStruct((T, D), F32), jax.ShapeDtypeStruct((T, D), F32 if router else BF16)]
    if router:
        w_hi = w_router.astype(BF16)
        w_lo = (w_router - w_hi.astype(F32)).astype(BF16)
        in_specs.append(pl.BlockSpec((D, 2 * N_EXPERTS), lambda i: (0, 0)))
        args.append(jnp.concatenate([w_hi, w_lo], axis=1))
        out_specs += [row(2), row(2)]
        out_shape += [jax.ShapeDtypeStruct((T, 2), jnp.int32), jax.ShapeDtypeStruct((T, 2), F32)]
    return pl.pallas_call(
        functools.partial(_proj_res_norm_kernel, router=router),
        grid=(T // TM,),
        in_specs=in_specs,
        out_specs=out_specs,
        out_shape=out_shape,
        scratch_shapes=[pltpu.VMEM((D, D), BF16)],
        compiler_params=_params(("arbitrary",)),
        name="proj_res_norm",
    )(*args)


def _swiglu_rows(hn_ref, w_hbm, sel, next_sel, has_next, wbuf, wsem, acc_ref):
    assert N_FCH % 2 == 1
    wg_hbm, wu_hbm, wd_hbm = w_hbm
    bg, bu, bd = wbuf
    tile = pl.program_id(0)

    def chunk_copies(f, slot, sel=sel):
        cols = pl.ds(pl.multiple_of(f * TF, TF), TF)
        copies = []
        for h in range(2):
            rows_d = pl.ds(h * (D // 2), D // 2)
            rows_f = pl.ds(pl.multiple_of(f * TF + h * (TF // 2), TF // 2), TF // 2)
            copies.append((h, pltpu.make_async_copy(wg_hbm.at[sel + (rows_d, cols)], bg.at[slot, rows_d],
                                                    wsem.at[0, slot])))
            copies.append((h, pltpu.make_async_copy(wu_hbm.at[sel + (rows_d, cols)], bu.at[slot, rows_d],
                                                    wsem.at[1, slot])))
            copies.append((h, pltpu.make_async_copy(wd_hbm.at[sel + (rows_f, slice(None))],
                                                    bd.at[slot, pl.ds(h * (TF // 2), TF // 2)], wsem.at[2, slot])))
        return copies

    def start_chunk(f, slot, sel=sel):
        for h, cp in chunk_copies(f, slot, sel):
            cp.start(priority=h)

    @pl.when(tile == 0)
    def _():
        start_chunk(0, 0)

    acc_ref[...] = jnp.zeros_like(acc_ref)

    def body(f, carry):
        slot = lax.rem(tile + f, 2)

        @pl.when(f + 1 < N_FCH)
        def _():
            start_chunk(f + 1, 1 - slot)

        @pl.when(jnp.logical_and(f + 1 == N_FCH, has_next))
        def _():
            start_chunk(0, 1 - slot, next_sel)

        for _, cp in chunk_copies(f, slot):
            cp.wait()
        hn = hn_ref[...]
        g = jnp.dot(hn, bg[slot].astype(BF16), preferred_element_type=F32)
        u = jnp.dot(hn, bu[slot].astype(BF16), preferred_element_type=F32)
        h = (g * jax.nn.sigmoid(g) * u).astype(BF16)
        acc_ref[...] += jnp.dot(h, bd[slot].astype(BF16), preferred_element_type=F32)
        return carry

    lax.fori_loop(0, N_FCH, body, 0)


def _weight_scratch():
    return [pltpu.VMEM((2, D, TF), F32), pltpu.VMEM((2, D, TF), F32), pltpu.VMEM((2, TF, D), F32),
            pltpu.SemaphoreType.DMA((3, 2))]


def _ffn_kernel(hn_ref, wg_hbm, wu_hbm, wd_hbm, x_ref, g2_ref, ng_ref, sc_ref, sh_ref, xo_ref, hno_ref,
                acc_ref, bg, bu, bd, wsem, *, idx):
    has_next = pl.program_id(0) + 1 < pl.num_programs(0)
    _swiglu_rows(hn_ref, (wg_hbm, wu_hbm, wd_hbm), (idx,), (idx,), has_next, (bg, bu, bd), wsem, acc_ref)
    x = x_ref[...] + g2_ref[0, 0, 0] * acc_ref[...]
    xo_ref[...] = x
    hno_ref[...] = _rms_mod(x, ng_ref[...], sc_ref[0, 0, 0], sh_ref[0, 0, 0]).astype(hno_ref.dtype)


def _ffn(hn, wg, wu, wd, idx, x, mods, layer, next_g):
    row = lambda: pl.BlockSpec((TM, D), lambda i: (i, 0))
    hbm = lambda: pl.BlockSpec(memory_space=pl.ANY)
    return pl.pallas_call(
        functools.partial(_ffn_kernel, idx=idx),
        grid=(T // TM,),
        in_specs=[row(), hbm(), hbm(), hbm(),
                  row(), _mod_spec(layer, 5), pl.BlockSpec((1, D), lambda i: (0, 0)),
                  _mod_spec(layer + 1, 1), _mod_spec(layer + 1, 0)],
        out_specs=[row(), row()],
        out_shape=[jax.ShapeDtypeStruct((T, D), F32), jax.ShapeDtypeStruct((T, D), BF16)],
        scratch_shapes=[pltpu.VMEM((TM, D), F32)] + _weight_scratch(),
        compiler_params=_params(("arbitrary",)),
        name="ffn",
    )(hn, wg, wu, wd, x, mods, next_g.reshape(1, D), mods, mods)


ROW_UNROLL = 8
SCATTER_TM = 512
COMBINE_TM = 512


def _wait_rows(src_ref, dst_ref, sem, n_rows):
    pltpu.make_async_copy(src_ref.at[pl.ds(0, n_rows)], dst_ref.at[pl.ds(0, n_rows)], sem).wait()


def _scatter_kernel(dest_ref, hn_ref, xs_in_ref, xs_ref, sem):
    del xs_in_ref
    base = pl.program_id(0) * (2 * SCATTER_TM)

    def body(g, carry):
        for u in range(ROW_UNROLL):
            r = g * ROW_UNROLL + u
            for k in range(2):
                d = dest_ref[base + 2 * r + k]
                pltpu.make_async_copy(hn_ref.at[pl.ds(r, 1)], xs_ref.at[pl.ds(d, 1)], sem).start()
        return carry

    lax.fori_loop(0, SCATTER_TM // ROW_UNROLL, body, 0)
    for _ in range(2):
        _wait_rows(hn_ref, xs_ref, sem, SCATTER_TM)


def _moe_scatter(hn2, dest_flat):
    return pl.pallas_call(
        _scatter_kernel,
        grid_spec=pltpu.PrefetchScalarGridSpec(
            num_scalar_prefetch=1,
            grid=(T // SCATTER_TM,),
            in_specs=[pl.BlockSpec((SCATTER_TM, D), lambda j, dest: (j, 0)),
                      pl.BlockSpec(memory_space=pl.ANY)],
            out_specs=pl.BlockSpec(memory_space=pl.ANY),
            scratch_shapes=[pltpu.SemaphoreType.DMA],
        ),
        out_shape=jax.ShapeDtypeStruct((MOE_ROWS, D), F32),
        input_output_aliases={2: 0},
        compiler_params=_params(("arbitrary",)),
        name="moe_scatter",
    )(dest_flat, hn2, jnp.zeros((MOE_ROWS, D), F32))


def _moe_kernel(te_ref, nu_ref, xs_ref, wg_hbm, wu_hbm, wd_hbm, y_ref, xsb_ref, bg, bu, bd, wsem, *, idx):
    i = pl.program_id(0)

    @pl.when(i < nu_ref[0])
    def _():
        xsb_ref[...] = xs_ref[...].astype(BF16)
        next_expert = te_ref[jnp.minimum(i + 1, MOE_TILES - 1)]
        _swiglu_rows(xsb_ref, (wg_hbm, wu_hbm, wd_hbm), (idx, te_ref[i]), (idx, next_expert), i + 1 < nu_ref[0],
                     (bg, bu, bd), wsem, y_ref)

    @pl.when(i >= nu_ref[0])
    def _():
        y_ref[...] = jnp.zeros_like(y_ref)


def _moe_experts(xs, tile_expert, n_used, wg, wu, wd, idx):
    hbm = lambda: pl.BlockSpec(memory_space=pl.ANY)
    return pl.pallas_call(
        functools.partial(_moe_kernel, idx=idx),
        grid_spec=pltpu.PrefetchScalarGridSpec(
            num_scalar_prefetch=2,
            grid=(MOE_TILES,),
            in_specs=[pl.BlockSpec((MOE_TM, D), lambda i, te, nu: (jnp.minimum(i, nu[0] - 1), 0)),
                      hbm(), hbm(), hbm()],
            out_specs=pl.BlockSpec((MOE_TM, D), lambda i, te, nu: (i, 0)),
            scratch_shapes=[pltpu.VMEM((MOE_TM, D), BF16)] + _weight_scratch(),
        ),
        out_shape=jax.ShapeDtypeStruct((MOE_ROWS, D), F32),
        compiler_params=_params(("arbitrary",)),
        name="moe_experts",
    )(tile_expert, n_used, xs, wg, wu, wd)


CTX_COMBINE_TILES = T_CTX // COMBINE_TM


def _combine_kernel(dest_ref, x_ref, y_hbm, gate_ref, g2_ref, ng_ref, sc_ref, sh_ref, *rest, final):
    *outs, ybuf, sem = rest
    j = pl.program_id(0)
    slot = j % 2

    def issue(tile, s):
        base = tile * (2 * COMBINE_TM)

        def body(g, carry):
            for u in range(ROW_UNROLL):
                r = g * ROW_UNROLL + u
                for k in range(2):
                    d = dest_ref[base + 2 * r + k]
                    pltpu.make_async_copy(y_hbm.at[pl.ds(d, 1)], ybuf.at[s, pl.ds(k * COMBINE_TM + r, 1)],
                                          sem.at[s]).start()
            return carry

        lax.fori_loop(0, COMBINE_TM // ROW_UNROLL, body, 0)

    @pl.when(j == 0)
    def _():
        issue(0, 0)

    @pl.when(j + 1 < pl.num_programs(0))
    def _():
        issue(j + 1, 1 - slot)

    _wait_rows(y_hbm, ybuf.at[slot], sem.at[slot], 2 * COMBINE_TM)
    gates = gate_ref[...]
    moe = gates[:, 0:1] * ybuf[slot, 0:COMBINE_TM] + gates[:, 1:2] * ybuf[slot, COMBINE_TM:2 * COMBINE_TM]
    x = x_ref[...] + g2_ref[0, 0, 0] * moe
    if final:
        yp_ref, ys_ref = outs
        y = _rms_mod(x, ng_ref[...], 0.0, 0.0)

        @pl.when(j < CTX_COMBINE_TILES)
        def _():
            yp_ref[...] = y

        @pl.when(j >= CTX_COMBINE_TILES)
        def _():
            ys_ref[...] = y
    else:
        xo_ref, hno_ref = outs
        xo_ref[...] = x
        hno_ref[...] = _rms_mod(x, ng_ref[...], sc_ref[0, 0, 0], sh_ref[0, 0, 0]).astype(hno_ref.dtype)


def _moe_combine(x, y, dest_flat, gates, mods, layer, next_g, final):
    row = lambda n=D: pl.BlockSpec((COMBINE_TM, n), lambda j, dest: (j, 0))
    vec = lambda: pl.BlockSpec((1, D), lambda j, dest: (0, 0))
    mod = lambda l, k: pl.BlockSpec((1, 1, 1, 1, D), lambda j, dest: (l, k, _cond_of_tile(j, COMBINE_TM), 0, 0))
    nl = layer if final else layer + 1
    if final:
        out_specs = [pl.BlockSpec((COMBINE_TM, D), lambda j, dest: (jnp.minimum(j, CTX_COMBINE_TILES - 1), 0)),
                     pl.BlockSpec((COMBINE_TM, D), lambda j, dest: (jnp.maximum(j - CTX_COMBINE_TILES, 0), 0))]
        out_shape = [jax.ShapeDtypeStruct((T_CTX, D), F32), jax.ShapeDtypeStruct((T_LAT, D), F32)]
    else:
        out_specs = [row(), row()]
        out_shape = [jax.ShapeDtypeStruct((T, D), F32), jax.ShapeDtypeStruct((T, D), BF16)]
    return pl.pallas_call(
        functools.partial(_combine_kernel, final=final),
        grid_spec=pltpu.PrefetchScalarGridSpec(
            num_scalar_prefetch=1,
            grid=(T // COMBINE_TM,),
            in_specs=[row(), pl.BlockSpec(memory_space=pl.ANY), row(2), mod(layer, 5), vec(), mod(nl, 1), mod(nl, 0)],
            out_specs=out_specs,
            scratch_shapes=[pltpu.VMEM((2, 2 * COMBINE_TM, D), F32), pltpu.SemaphoreType.DMA((2,))],
        ),
        out_shape=out_shape,
        compiler_params=_params(("arbitrary",)),
        name="moe_combine",
    )(dest_flat, x, y, gates, mods, next_g.reshape(1, D), mods, mods)


LRU_CHUNK = 256


def _shift_rows(v, s, row):
    n = v.shape[0]
    rolled = pltpu.roll(v, s % n, axis=0)
    keep = (row >= s) if s > 0 else (row < n + s)
    return jnp.where(keep, rolled, 0.0)


W_IN_CHUNK = 512
N_W_IN_CHUNKS = IN_W // W_IN_CHUNK


def _lru_kernel(hn_ref, h0_ref, win_hbm, cw_ref, cb_ref, wbd_ref, bbd_ref, lam_ref, scw_ref, scb_ref,
                y_ref, st_ref, winb_ref, stage_ref, stage_sem, proj_ref, xc_ref, a_ref, b_ref, *, seq_len, idx):
    @pl.when(pl.program_id(0) == 0)
    def _():
        copies = [pltpu.make_async_copy(win_hbm.at[idx, :, pl.ds(j * W_IN_CHUNK, W_IN_CHUNK)],
                                        stage_ref.at[j % 2], stage_sem.at[j % 2]) for j in range(N_W_IN_CHUNKS)]
        copies[0].start()
        for j in range(N_W_IN_CHUNKS):
            if j + 1 < N_W_IN_CHUNKS:
                copies[j + 1].start()
            copies[j].wait()
            winb_ref[:, j * W_IN_CHUNK:(j + 1) * W_IN_CHUNK] = stage_ref[j % 2].astype(BF16)

    proj_ref[...] = jnp.dot(hn_ref[...], winb_ref[...], preferred_element_type=F32)
    row = lax.broadcasted_iota(jnp.int32, (seq_len, 1), 0)
    x = proj_ref[:, 0:LRU_W]
    xc =(cb_ref[...] + _shift_rows(x, 2, row) * cw_ref[0:1] + _shift_rows(x, 1, row) * cw_ref[1:2]
          + x * cw_ref[2:3] + _shift_rows(x, -1, row) * cw_ref[3:4])
    xc_ref[...] = xc

    lam = lam_ref[...]
    neg_csp = -LRU_C * (jnp.maximum(-lam, 0.0) + jnp.log1p(jnp.exp(-jnp.abs(lam))))

    for c in range(seq_len // LRU_CHUNK):
        rows = pl.ds(c * LRU_CHUNK, LRU_CHUNK)
        xcc = xc_ref[rows, :]
        gates = jnp.dot(xcc.astype(BF16), wbd_ref[...], preferred_element_type=F32) + bbd_ref[...]
        for d in range(2):
            r = jax.nn.sigmoid(gates[:, (2 * d) * LRU_W:(2 * d + 1) * LRU_W])
            i = jax.nn.sigmoid(gates[:, (2 * d + 1) * LRU_W:(2 * d + 2) * LRU_W])
            log_a = neg_csp[d:d + 1] * r
            t = jnp.tanh(log_a)
            one_minus_a2 = -2.0 * t / (1.0 - t)
            a_ref[d, rows, :] = jnp.exp(log_a)
            b_ref[d, rows, :] = jnp.sqrt(one_minus_a2) * (i * xcc)

    n_grp = seq_len // 8

    def scan_body(g, carry):
        hf, hb = carry
        rf = pl.ds(pl.multiple_of(g * 8, 8), 8)
        rb = pl.ds(pl.multiple_of((n_grp - 1 - g) * 8, 8), 8)
        af, bf = a_ref[0, rf, :], b_ref[0, rf, :]
        ab, bb = a_ref[1, rb, :], b_ref[1, rb, :]
        outs_f, outs_b = [], []
        for j in range(8):
            hf = af[j:j + 1] * hf + bf[j:j + 1]
            outs_f.append(hf)
            hb = ab[7 - j:8 - j] * hb + bb[7 - j:8 - j]
            outs_b.append(hb)
        b_ref[0, rf, :] = jnp.concatenate(outs_f, axis=0)
        b_ref[1, rb, :] = jnp.concatenate(outs_b[::-1], axis=0)
        return hf, hb

    hf, hb = lax.fori_loop(0, n_grp, scan_body, (h0_ref[0, 0:1, :], h0_ref[0, 1:2, :]))
    st_ref[0, 0:1, :] = hf
    st_ref[0, 1:2, :] = hb

    h_sum = b_ref[0] + b_ref[1]
    y_ref[:, 0:LRU_W] = (h_sum * jax.nn.gelu(proj_ref[:, LRU_W:2 * LRU_W])).astype(y_ref.dtype)
    o = 2 * LRU_W
    cv = proj_ref[:, o + SC_W:o + 2 * SC_W] * proj_ref[:, o + 2 * SC_W:o + 3 * SC_W]
    conv = scb_ref[...] + _shift_rows(cv, 1, row) * scw_ref[0:1] + cv * scw_ref[1:2] + _shift_rows(cv, -1, row) * scw_ref[2:3]
    y_ref[:, LRU_W:LRU_W + SC_W] = (proj_ref[:, o:o + SC_W] * conv).astype(y_ref.dtype)


def _lru_mixer(hn, h0, w_in, idx, cw, cb, wbd, bbd, lam, scw, scb, *, latent):
    if latent:
        n_seq, seq_len, blk0 = DEC_BATCH, DEC_SEQ, T_CTX // DEC_SEQ
    else:
        n_seq, seq_len, blk0 = BATCH, SEQ, 0
    full = lambda shape: pl.BlockSpec(shape, lambda b: (0,) * len(shape))
    return pl.pallas_call(
        functools.partial(_lru_kernel, seq_len=seq_len, idx=idx),
        grid=(n_seq,),
        in_specs=[pl.BlockSpec((seq_len, D), lambda b: (blk0 + b, 0)),
                  pl.BlockSpec((1, 2, LRU_W), lambda b: (b, 0, 0)),
                  pl.BlockSpec(memory_space=pl.ANY),
                  full((4, LRU_W)), full((1, LRU_W)), full((LRU_W, 4 * LRU_W)), full((1, 4 * LRU_W)),
                  full((2, LRU_W)), full((3, SC_W)), full((1, SC_W))],
        out_specs=[pl.BlockSpec((seq_len, D), lambda b: (b, 0)),
                   pl.BlockSpec((1, 2, LRU_W), lambda b: (b, 0, 0))],
        out_shape=[jax.ShapeDtypeStruct((n_seq * seq_len, D), BF16),
                   jax.ShapeDtypeStruct((n_seq, 2, LRU_W), F32)],
        scratch_shapes=[pltpu.VMEM((D, IN_W), BF16),
                        pltpu.VMEM((2, D, W_IN_CHUNK), F32),
                        pltpu.SemaphoreType.DMA((2,)),
                        pltpu.VMEM((seq_len, IN_W), F32),
                        pltpu.VMEM((seq_len, LRU_W), F32),
                        pltpu.VMEM((2, seq_len, LRU_W), F32),
                        pltpu.VMEM((2, seq_len, LRU_W), F32)],
        compiler_params=_params(("arbitrary",)),
        name="lru_mixer",
    )(hn, h0, w_in, cw, cb.reshape(1, LRU_W), wbd, bbd, lam, scw, scb.reshape(1, SC_W))


def _block_diag_dense(w):
    eye = jnp.eye(LRU_HEADS, dtype=w.dtype)
    return jnp.einsum('hij,hg->higj', w, eye).reshape(LRU_W, LRU_W)


def _route(idx):
    e_flat = idx.reshape(-1)
    onehot = (e_flat[:, None] == jnp.arange(N_EXPERTS)[None, :]).astype(jnp.int32)
    csum = jnp.cumsum(onehot, axis=0)
    counts = csum[-1]
    rank = jnp.take_along_axis(csum, e_flat[:, None], axis=1)[:, 0] - 1
    padded = (counts + MOE_TM - 1) // MOE_TM * MOE_TM
    ends = jnp.cumsum(padded)
    dest = ((ends - padded)[e_flat] + rank).astype(jnp.int32)
    tile_end = ends // MOE_TM
    n_used = tile_end[-1]
    tiles = jnp.minimum(jnp.arange(MOE_TILES), n_used - 1)
    tile_expert = jnp.sum(tiles[:, None] >= tile_end[None, :], axis=1).astype(jnp.int32)
    return dest, tile_expert, n_used.reshape(1).astype(jnp.int32)


def kernel(x_prompt, x_sample, state_lru, cache_k, cache_v, c, c_ctx, norm1_g, norm2_g, w_mod, b_mod, w_in, lru_conv_w, lru_conv_b, lru_wa, lru_ba, lru_wx, lru_bx, lru_lam, sc_conv_w, sc_conv_b, w_out, ffn_w_gate, ffn_w_up, ffn_w_down, w_qkv, w_o, rpb, w_router, moe_w_gate, moe_w_up, moe_w_down, final_g):
    cond = jnp.concatenate([c_ctx[None, :], c, jnp.zeros((COND_ROWS - N_COND, D), F32)], axis=0)
    mods = _modulation(cond, w_mod, b_mod).reshape(DEPTH, 6, COND_ROWS, 1, D)

    x, hn = _first_norm(x_prompt.reshape(T_CTX, D), x_sample.reshape(T_LAT, D), norm1_g[0], mods, 0)
    cache_kt = jnp.swapaxes(cache_k, -1, -2)
    cache_vt = jnp.swapaxes(cache_v, -1, -2)
    zero_state = jnp.zeros((BATCH, 2, LRU_W), F32)
    states = []
    new_k = jnp.zeros((BATCH, DEPTH // 2, N_HEADS, HEAD_DIM, SEQ), F32)
    new_v = jnp.zeros((BATCH, DEPTH // 2, N_HEADS, HEAD_DIM, SEQ), F32)
    for l in range(DEPTH):
        idx = l // 2
        if l % 2 == 0:
            wbd = jnp.concatenate([_block_diag_dense(lru_wa[idx, 0]), _block_diag_dense(lru_wx[idx, 0]),
                                   _block_diag_dense(lru_wa[idx, 1]), _block_diag_dense(lru_wx[idx, 1])],
                                  axis=1).astype(BF16)
            bbd = jnp.concatenate([lru_ba[idx, 0], lru_bx[idx, 0], lru_ba[idx, 1], lru_bx[idx, 1]]).reshape(1, 4 * LRU_W)
            lru_args = (lru_conv_w[idx], lru_conv_b[idx], wbd, bbd, lru_lam[idx], sc_conv_w[idx], sc_conv_b[idx])
            y_ctx, st = _lru_mixer(hn, zero_state, w_in, idx, *lru_args, latent=False)
            y_lat, _ = _lru_mixer(hn, state_lru[:, idx], w_in, idx, *lru_args, latent=True)
            states.append(st)
            x, hn2 = _proj_res_norm(y_ctx, y_lat, w_out, idx, x, norm2_g[l], mods, l)
            x, hn = _ffn(hn2, ffn_w_gate, ffn_w_up, ffn_w_down, idx, x, mods, l, norm1_g[l + 1])
        else:
            q_scale = HEAD_DIM ** -0.5
            q_ctx = _matmul_heads(hn, w_qkv, idx, 0, latent=False, out_dtype=BF16, scale=q_scale)
            new_k = _matmul_heads(hn, w_qkv, idx, 1, latent=False, out_dtype=F32, into=new_k, slot=idx, n_slots=2,
                                  transposed=True)
            new_v = _matmul_heads(hn, w_qkv, idx, 2, latent=False, out_dtype=F32, into=new_v, slot=idx, n_slots=2,
                                  transposed=True)
            q_lat = _matmul_heads(hn, w_qkv, idx, 0, latent=True, out_dtype=BF16, scale=q_scale)
            k_lat = _matmul_heads(hn, w_qkv, idx, 1, latent=True, out_dtype=F32)
            v_lat = _matmul_heads(hn, w_qkv, idx, 2, latent=True, out_dtype=F32)
            o_ctx = _context_attention(q_ctx, new_k, new_v, idx)
            o_lat = _neighbourhood_attention(q_lat, k_lat, v_lat, cache_kt, cache_vt, _window_bias(rpb[idx]), idx)
            x, hn2, top_i, top_p = _proj_res_norm(o_ctx, o_lat, w_o, idx, x, norm2_g[l], mods, l, w_router[idx])
            dest, tile_expert, n_used = _route(top_i)
            xs = _moe_scatter(hn2, dest)
            y = _moe_experts(xs, tile_expert, n_used, moe_w_gate, moe_w_up, moe_w_down, idx)
            if l == DEPTH - 1:
                y_prompt, y_sample = _moe_combine(x, y, dest, top_p, mods, l, final_g, True)
            else:
                x, hn = _moe_combine(x, y, dest, top_p, mods, l, norm1_g[l + 1], False)

    return (y_prompt.reshape(BATCH, SEQ, D), y_sample.reshape(DEC_BATCH, DEC_SEQ, D), jnp.stack(states, axis=1),
            jnp.swapaxes(new_k, -1, -2), jnp.swapaxes(new_v, -1, -2))
```

```python
import functools

import jax
import jax.numpy as jnp
import numpy as np
from jax import lax
from jax.experimental import pallas as pl
from jax.experimental.pallas import tpu as pltpu

F32 = jnp.float32
BF16 = jnp.bfloat16

D = 1024
BATCH = 32
SEQ = 256
DEPTH = 4
DEC_BATCH = 2
DEC_SEQ = 1024
PAST_LEN = 256
GRID_W = 64
GRID_ROWS = DEC_SEQ // GRID_W
LRU_W = 512
LRU_HEADS = 8
LRU_BLOCK = LRU_W // LRU_HEADS
LRU_C = 8.0
SC_W = 512
IN_W = 2 * LRU_W + 3 * SC_W
N_HEADS = 16
HEAD_DIM = D // N_HEADS
WIN_R = 8
WIN_C = 16
D_FF = 2816
N_EXPERTS = 8
EPS = 1e-6
NEG_INF = -1e30

T_CTX = BATCH * SEQ
T_LAT = DEC_BATCH * DEC_SEQ
T = T_CTX + T_LAT
N_COND = 3
COND_ROWS = 8

TM = 1024
TF = 256
N_FCH = D_FF // TF
MOE_TM = 1024
MOE_TILES = (2 * T + N_EXPERTS * (MOE_TM - 1)) // MOE_TM
MOE_ROWS = MOE_TILES * MOE_TM
VMEM_LIMIT = 52 * 1024 * 1024


def _params(sem):
    return pltpu.CompilerParams(dimension_semantics=sem, vmem_limit_bytes=VMEM_LIMIT)


def _cond_of_tile(i, tm=TM):
    r = i * tm
    return jnp.where(r < T_CTX, 0, 1 + (r - T_CTX) // DEC_SEQ)


def _mod_spec(layer, k, grid_rank=1, tm=TM):
    if grid_rank == 1:
        return pl.BlockSpec((1, 1, 1, 1, D), lambda i: (layer, k, _cond_of_tile(i, tm), 0, 0))
    return pl.BlockSpec((1, 1, 1, 1, D), lambda i, f, *_: (layer, k, _cond_of_tile(i, tm), 0, 0))


def _rms_mod(x, g, scale, shift):
    y = x * lax.rsqrt(jnp.mean(x * x, axis=-1, keepdims=True) + EPS)
    return (y * g) * (1.0 + scale) + shift


def _mod_kernel(c_ref, w_ref, b_ref, o_ref):
    c = c_ref[...]
    s = c * jax.nn.sigmoid(c)
    o_ref[0, 0] = jnp.dot(s.astype(BF16), w_ref[0].astype(BF16), preferred_element_type=F32) + b_ref[0, 0]


def _modulation(cond, w_mod, b_mod):
    return pl.pallas_call(
        _mod_kernel,
        grid=(DEPTH, 6),
        in_specs=[
            pl.BlockSpec((COND_ROWS, D), lambda l, k: (0, 0)),
            pl.BlockSpec((1, D, D), lambda l, k: (l, 0, k)),
            pl.BlockSpec((1, 1, 1, D), lambda l, k: (l, k, 0, 0)),
        ],
        out_specs=pl.BlockSpec((1, 1, COND_ROWS, D), lambda l, k: (l, k, 0, 0)),
        out_shape=jax.ShapeDtypeStruct((DEPTH, 6, COND_ROWS, D), F32),
        compiler_params=_params(("arbitrary", "arbitrary")),
        name="adaln_mod",
    )(cond, w_mod, b_mod.reshape(DEPTH, 6, 1, D))


def _norm_kernel(xc_ref, xl_ref, g_ref, sc_ref, sh_ref, x_ref, o_ref):
    x = jnp.where(pl.program_id(0) < T_CTX // TM, xc_ref[...], xl_ref[...])
    x_ref[...] = x
    o_ref[...] = _rms_mod(x, g_ref[...], sc_ref[0, 0, 0], sh_ref[0, 0, 0]).astype(o_ref.dtype)


def _first_norm(x_ctx, x_lat, g, mods, layer):
    n_ctx = T_CTX // TM
    return pl.pallas_call(
        _norm_kernel,
        grid=(T // TM,),
        in_specs=[
            pl.BlockSpec((TM, D), lambda i: (jnp.minimum(i, n_ctx - 1), 0)),
            pl.BlockSpec((TM, D), lambda i: (jnp.maximum(i - n_ctx, 0), 0)),
            pl.BlockSpec((1, D), lambda i: (0, 0)),
            _mod_spec(layer, 1),
            _mod_spec(layer, 0),
        ],
        out_specs=[pl.BlockSpec((TM, D), lambda i: (i, 0)), pl.BlockSpec((TM, D), lambda i: (i, 0))],
        out_shape=[jax.ShapeDtypeStruct((T, D), F32), jax.ShapeDtypeStruct((T, D), BF16)],
        compiler_params=_params(("arbitrary",)),
        name="first_norm",
    )(x_ctx, x_lat, g.reshape(1, D), mods, mods)


def _mm_heads_kernel(a_ref, w_ref, o_ref, wb_ref, acc_ref, *, n_seq, seq_len, scale, transposed):
    @pl.when(pl.program_id(0) == 0)
    def _():
        wb_ref[...] = w_ref[0].astype(BF16)

    acc = jnp.dot(a_ref[...], wb_ref[...], preferred_element_type=F32)
    if scale != 1.0:
        acc = acc * scale
    if transposed:
        acc_ref[...] = acc
        acc_t = acc_ref[...].T
        for s in range(n_seq):
            for h in range(N_HEADS):
                o_ref[s, 0, h] = acc_t[h * HEAD_DIM:(h + 1) * HEAD_DIM,
                                       s * seq_len:(s + 1) * seq_len].astype(o_ref.dtype)
    else:
        for s in range(n_seq):
            for h in range(N_HEADS):
                o_ref[s, 0, h] = acc[s * seq_len:(s + 1) * seq_len,
                                     h * HEAD_DIM:(h + 1) * HEAD_DIM].astype(o_ref.dtype)


def _matmul_heads(hn, w_qkv, idx, part, *, latent, out_dtype, scale=1.0, into=None, slot=0, n_slots=1,
                  transposed=False):
    if latent:
        n_batch, seq_len, row0 = DEC_BATCH, DEC_SEQ, T_CTX // TM
    else:
        n_batch, seq_len, row0 = BATCH, SEQ, 0
    n_seq = TM // seq_len
    tail = (HEAD_DIM, seq_len) if transposed else (seq_len, HEAD_DIM)
    kern = functools.partial(_mm_heads_kernel, n_seq=n_seq, seq_len=seq_len, scale=scale, transposed=transposed)
    in_specs = [
        pl.BlockSpec((TM, D), lambda i: (row0 + i, 0)),
        pl.BlockSpec((1, D, D), lambda i: (idx, 0, part)),
    ]
    args = [hn, w_qkv]
    aliases = {}
    if into is not None:
        in_specs.append(pl.BlockSpec(memory_space=pl.ANY))
        args.append(into)
        aliases = {2: 0}
        body = lambda a, w, _, o, wb, acc: kern(a, w, o, wb, acc)
    else:
        body = kern
    return pl.pallas_call(
        body,
        grid=(n_batch // n_seq,),
        in_specs=in_specs,
        out_specs=pl.BlockSpec((n_seq, 1, N_HEADS) + tail, lambda i: (i, slot, 0, 0, 0)),
        out_shape=jax.ShapeDtypeStruct((n_batch, n_slots, N_HEADS) + tail, out_dtype),
        scratch_shapes=[pltpu.VMEM((D, D), BF16), pltpu.VMEM((TM, D), F32)],
        input_output_aliases=aliases,
        compiler_params=_params(("arbitrary",)),
        name="qkv_heads",
    )(*args)


def _dot_nt(a, b):
    return lax.dot_general(a, b, (((1,), (1,)), ((), ())), preferred_element_type=F32)


def _softmax_pv(s_list, v_list, v_transposed=False):
    m = s_list[0].max(axis=-1, keepdims=True)
    for s in s_list[1:]:
        m = jnp.maximum(m, s.max(axis=-1, keepdims=True))
    den = None
    out = None
    for s, v in zip(s_list, v_list):
        p = jnp.exp(s - m)
        d = p.sum(axis=-1, keepdims=True)
        pb = p.astype(BF16)
        o = _dot_nt(pb, v) if v_transposed else jnp.dot(pb, v, preferred_element_type=F32)
        den = d if den is None else den + d
        out = o if out is None else out + o
    return out / den


HEAD_PAIR_W = 2 * HEAD_DIM


def _ctx_attn_kernel(q_ref, kt_ref, vt_ref, o_ref):
    for hp in range(N_HEADS // 2):
        pair = []
        for h in (2 * hp, 2 * hp + 1):
            s = jnp.dot(q_ref[0, 0, h], kt_ref[0, 0, h].astype(BF16), preferred_element_type=F32)
            pair.append(_softmax_pv([s], [vt_ref[0, 0, h].astype(BF16)], v_transposed=True))
        o_ref[:, hp * HEAD_PAIR_W:(hp + 1) * HEAD_PAIR_W] = jnp.concatenate(pair, axis=-1).astype(o_ref.dtype)


def _context_attention(q, kt, vt, slot):
    kv_spec = pl.BlockSpec((1, 1, N_HEADS, HEAD_DIM, SEQ), lambda b: (b, slot, 0, 0, 0))
    return pl.pallas_call(
        _ctx_attn_kernel,
        grid=(BATCH,),
        in_specs=[pl.BlockSpec((1, 1, N_HEADS, SEQ, HEAD_DIM), lambda b: (b, 0, 0, 0, 0)), kv_spec, kv_spec],
        out_specs=pl.BlockSpec((SEQ, D), lambda b: (b, 0)),
        out_shape=jax.ShapeDtypeStruct((T_CTX, D), BF16),
        compiler_params=_params(("arbitrary",)),
        name="ctx_attention",
    )(q, kt, vt)


def _row_start(r):
    return min(max(r - WIN_R // 2, 0), GRID_ROWS - WIN_R)


def _nbr_attn_kernel(q_ref, k_ref, v_ref, kct_ref, vct_ref, bias_ref, o_ref):
    kct = [kct_ref[0, 0, hh].astype(BF16) for hh in range(2)]
    vct = [vct_ref[0, 0, hh].astype(BF16) for hh in range(2)]
    for r0, r1 in _ROW_GROUPS:
        rs = _row_start(r0)
        n_q = (r1 - r0) * GRID_W
        pair = []
        for hh in range(2):
            q = q_ref[0, 0, hh, r0 * GRID_W:r1 * GRID_W, :]
            kw = k_ref[0, 0, hh, rs * GRID_W:(rs + WIN_R) * GRID_W, :].astype(BF16)
            vw = v_ref[0, 0, hh, rs * GRID_W:(rs + WIN_R) * GRID_W, :].astype(BF16)
            bias = bias_ref[hh, r0 - rs:r1 - rs].reshape(n_q, WIN_R * GRID_W)
            s_win = _dot_nt(q, kw) + bias
            s_ctx = jnp.dot(q, kct[hh], preferred_element_type=F32)
            m = jnp.maximum(s_win.max(axis=-1, keepdims=True), s_ctx.max(axis=-1, keepdims=True))
            p_win = jnp.exp(s_win - m)
            p_ctx = jnp.exp(s_ctx - m)
            den = p_win.sum(axis=-1, keepdims=True) + p_ctx.sum(axis=-1, keepdims=True)
            out = (jnp.dot(p_win.astype(BF16), vw, preferred_element_type=F32)
                   + _dot_nt(p_ctx.astype(BF16), vct[hh]))
            pair.append(out / den)
        o_ref[r0 * GRID_W:r1 * GRID_W, :] = jnp.concatenate(pair, axis=-1).astype(o_ref.dtype)


def _row_groups():
    groups, r0 = [], 0
    for r in range(1, GRID_ROWS + 1):
        if r == GRID_ROWS or _row_start(r) != _row_start(r0):
            groups.append((r0, r))
            r0 = r
    return groups


_ROW_GROUPS = _row_groups()


def _window_bias(rpb_l):
    col = np.arange(GRID_W)
    col_start = np.clip(col - WIN_C // 2, 0, GRID_W - WIN_C)
    col_mask = (col[None, :] >= col_start[:, None]) & (col[None, :] < col_start[:, None] + WIN_C)
    dc = np.clip(col[None, :] - col[:, None], -(WIN_C - 1), WIN_C - 1) + (WIN_C - 1)
    onehot = (dc[None, :, :] == np.arange(2 * WIN_C - 1)[:, None, None]).astype(np.float32)
    rows = jnp.stack([rpb_l[:, WIN_R - 1 - v:2 * WIN_R - 1 - v, :] for v in range(WIN_R)], axis=1)
    bias = jnp.einsum('hvkd,dqc->hvqkc', rows, jnp.asarray(onehot), precision=lax.Precision.HIGHEST)
    bias = jnp.where(jnp.asarray(col_mask)[None, None, :, None, :], bias, NEG_INF)
    return bias.reshape(N_HEADS, WIN_R, GRID_W, WIN_R * GRID_W)


def _neighbourhood_attention(q, k, v, cache_kt, cache_vt, bias, slot):
    lat = lambda: pl.BlockSpec((1, 1, 2, DEC_SEQ, HEAD_DIM), lambda b, hp: (b, 0, hp, 0, 0))
    ctx = lambda: pl.BlockSpec((1, 1, 2, HEAD_DIM, PAST_LEN), lambda b, hp: (b, slot, hp, 0, 0))
    return pl.pallas_call(
        _nbr_attn_kernel,
        grid=(DEC_BATCH, N_HEADS // 2),
        in_specs=[lat(), lat(), lat(), ctx(), ctx(),
                  pl.BlockSpec((2, WIN_R, GRID_W, WIN_R * GRID_W), lambda b, hp: (hp, 0, 0, 0))],
        out_specs=pl.BlockSpec((DEC_SEQ, HEAD_PAIR_W), lambda b, hp: (b, hp)),
        out_shape=jax.ShapeDtypeStruct((T_LAT, D), BF16),
        compiler_params=_params(("arbitrary", "arbitrary")),
        name="nbr_attention",
    )(q, k, v, cache_kt, cache_vt, bias)


def _top2(logits):
    col = lax.broadcasted_iota(jnp.int32, logits.shape, 1)
    m1 = jnp.max(logits, axis=-1, keepdims=True)
    i1 = jnp.min(jnp.where(logits == m1, col, N_EXPERTS), axis=-1, keepdims=True)
    rest = jnp.where(col == i1, -jnp.inf, logits)
    m2 = jnp.max(rest, axis=-1, keepdims=True)
    i2 = jnp.min(jnp.where(rest == m2, col, N_EXPERTS), axis=-1, keepdims=True)
    e = jnp.exp(m2 - m1)
    return i1, i2, 1.0 / (1.0 + e), e / (1.0 + e)


CTX_TILES = T_CTX // TM


def _proj_res_norm_kernel(ac_ref, al_ref, w_ref, x_ref, g1_ref, ng_ref, sc_ref, sh_ref, *rest, router):
    if router:
        wr_ref, xo_ref, hn_ref, idx_ref, gate_ref, wb_ref = rest
    else:
        xo_ref, hn_ref, wb_ref = rest

    @pl.when(pl.program_id(0) == 0)
    def _():
        wb_ref[...] = w_ref[0].astype(BF16)

    a = jnp.where(pl.program_id(0) < CTX_TILES, ac_ref[...], al_ref[...])
    mix = jnp.dot(a, wb_ref[...], preferred_element_type=F32)
    x = x_ref[...] + g1_ref[0, 0, 0] * mix
    xo_ref[...] = x
    hn = _rms_mod(x, ng_ref[...], sc_ref[0, 0, 0], sh_ref[0, 0, 0])
    hn_ref[...] = hn.astype(hn_ref.dtype)
    if router:
        hi = hn.astype(BF16)
        lo = (hn - hi.astype(F32)).astype(BF16)
        l_hi = jnp.dot(hi, wr_ref[...], preferred_element_type=F32)
        l_lo = jnp.dot(lo, wr_ref[...], preferred_element_type=F32)
        logits = l_hi[:, :N_EXPERTS] + l_hi[:, N_EXPERTS:] + l_lo[:, :N_EXPERTS]
        i1, i2, p1, p2 = _top2(logits)
        first = lax.broadcasted_iota(jnp.int32, (logits.shape[0], 2), 1) == 0
        idx_ref[...] = jnp.where(first, i1, i2)
        gate_ref[...] = jnp.where(first, p1, p2)


def _proj_res_norm(a_ctx, a_lat, w, idx, x, norm_g, mods, layer, w_router=None):
    router = w_router is not None
    row = lambda n: pl.BlockSpec((TM, n), lambda i: (i, 0))
    in_specs = [pl.BlockSpec((TM, D), lambda i: (jnp.minimum(i, CTX_TILES - 1), 0)),
                pl.BlockSpec((TM, D), lambda i: (jnp.maximum(i - CTX_TILES, 0), 0)),
                pl.BlockSpec((1, D, D), lambda i: (idx, 0, 0)), row(D),
                _mod_spec(layer, 2), pl.BlockSpec((1, D), lambda i: (0, 0)), _mod_spec(layer, 4), _mod_spec(layer, 3)]
    args = [a_ctx, a_lat, w, x, mods, norm_g.reshape(1, D), mods, mods]
    out_specs = [row(D), row(D)]
    out_shape = [jax.ShapeDtypeStruct((T, D), F32), jax.ShapeDtypeStruct((T, D), F32 if router else BF16)]
    if router:
        w_hi = w_router.astype(BF16)
        w_lo = (w_router - w_hi.astype(F32)).astype(BF16)
        in_specs.append(pl.BlockSpec((D, 2 * N_EXPERTS), lambda i: (0, 0)))
        args.append(jnp.concatenate([w_hi, w_lo], axis=1))
        out_specs += [row(2), row(2)]
        out_shape += [jax.ShapeDtypeStruct((T, 2), jnp.int32), jax.ShapeDtypeStruct((T, 2), F32)]
    return pl.pallas_call(
        functools.partial(_proj_res_norm_kernel, router=router),
        grid=(T // TM,),
        in_specs=in_specs,
        out_specs=out_specs,
        out_shape=out_shape,
        scratch_shapes=[pltpu.VMEM((D, D), BF16)],
        compiler_params=_params(("arbitrary",)),
        name="proj_res_norm",
    )(*args)


def _swiglu_chunk(hn, wg_ref, wu_ref, wd_ref, acc_ref):
    g = jnp.dot(hn, wg_ref[0].astype(BF16), preferred_element_type=F32)
    u = jnp.dot(hn, wu_ref[0].astype(BF16), preferred_element_type=F32)
    h = (g * jax.nn.sigmoid(g) * u).astype(BF16)
    acc_ref[...] += jnp.dot(h, wd_ref[0].astype(BF16), preferred_element_type=F32)


def _ffn_kernel(hn_ref, wg_ref, wu_ref, wd_ref, x_ref, g2_ref, ng_ref, sc_ref, sh_ref, xo_ref, hno_ref, acc_ref):
    @pl.when(pl.program_id(1) == 0)
    def _():
        acc_ref[...] = jnp.zeros_like(acc_ref)

    _swiglu_chunk(hn_ref[...], wg_ref, wu_ref, wd_ref, acc_ref)

    @pl.when(pl.program_id(1) == N_FCH - 1)
    def _():
        x = x_ref[...] + g2_ref[0, 0, 0] * acc_ref[...]
        xo_ref[...] = x
        hno_ref[...] = _rms_mod(x, ng_ref[...], sc_ref[0, 0, 0], sh_ref[0, 0, 0]).astype(hno_ref.dtype)


def _ffn(hn, wg, wu, wd, idx, x, mods, layer, next_g):
    row = lambda: pl.BlockSpec((TM, D), lambda i, f: (i, 0))
    return pl.pallas_call(
        _ffn_kernel,
        grid=(T // TM, N_FCH),
        in_specs=[row(),
                  pl.BlockSpec((1, D, TF), lambda i, f: (idx, 0, f)),
                  pl.BlockSpec((1, D, TF), lambda i, f: (idx, 0, f)),
                  pl.BlockSpec((1, TF, D), lambda i, f: (idx, f, 0)),
                  row(), _mod_spec(layer, 5, 2), pl.BlockSpec((1, D), lambda i, f: (0, 0)),
                  _mod_spec(layer + 1, 1, 2), _mod_spec(layer + 1, 0, 2)],
        out_specs=[row(), row()],
        out_shape=[jax.ShapeDtypeStruct((T, D), F32), jax.ShapeDtypeStruct((T, D), BF16)],
        scratch_shapes=[pltpu.VMEM((TM, D), F32)],
        compiler_params=_params(("arbitrary", "arbitrary")),
        name="ffn",
    )(hn, wg, wu, wd, x, mods, next_g.reshape(1, D), mods, mods)


ROW_UNROLL = 8
SCATTER_TM = 1024
COMBINE_TM = 1024


def _wait_rows(src_ref, dst_ref, sem, n_rows):
    pltpu.make_async_copy(src_ref.at[pl.ds(0, n_rows)], dst_ref.at[pl.ds(0, n_rows)], sem).wait()


def _scatter_kernel(dest_ref, hn_ref, xs_in_ref, xs_ref, sem):
    del xs_in_ref
    base = pl.program_id(0) * (2 * SCATTER_TM)

    def body(g, carry):
        r0 = pl.multiple_of(g * ROW_UNROLL, ROW_UNROLL)
        for u in range(ROW_UNROLL):
            for k in range(2):
                d = dest_ref[base + 2 * (r0 + u) + k]
                pltpu.make_async_copy(hn_ref.at[pl.ds(r0, ROW_UNROLL)].at[pl.ds(u, 1)], xs_ref.at[pl.ds(d, 1)],
                                      sem).start()
        return carry

    lax.fori_loop(0, SCATTER_TM // ROW_UNROLL, body, 0)
    for _ in range(2):
        _wait_rows(hn_ref, xs_ref, sem, SCATTER_TM)


def _moe_scatter(hn2, dest_flat):
    return pl.pallas_call(
        _scatter_kernel,
        grid_spec=pltpu.PrefetchScalarGridSpec(
            num_scalar_prefetch=1,
            grid=(T // SCATTER_TM,),
            in_specs=[pl.BlockSpec((SCATTER_TM, D), lambda j, dest: (j, 0)),
                      pl.BlockSpec(memory_space=pl.ANY)],
            out_specs=pl.BlockSpec(memory_space=pl.ANY),
            scratch_shapes=[pltpu.SemaphoreType.DMA],
        ),
        out_shape=jax.ShapeDtypeStruct((MOE_ROWS, D), F32),
        input_output_aliases={2: 0},
        compiler_params=_params(("arbitrary",)),
        name="moe_scatter",
    )(dest_flat, hn2, jnp.zeros((MOE_ROWS, D), F32))


def _moe_kernel(te_ref, nu_ref, rows_ref, xs_ref, wg_ref, wu_ref, wd_ref, y_ref, xsb_ref, acc_ref):
    i = pl.program_id(0)
    f = pl.program_id(1)
    half = MOE_TM // 2

    @pl.when(i < nu_ref[0])
    def _():
        @pl.when(f == 0)
        def _():
            xsb_ref[...] = xs_ref[...].astype(BF16)
            acc_ref[...] = jnp.zeros_like(acc_ref)

        @pl.when(rows_ref[i] > half)
        def _():
            _swiglu_chunk(xsb_ref[...], wg_ref, wu_ref, wd_ref, acc_ref)

        @pl.when(rows_ref[i] <= half)
        def _():
            _swiglu_chunk(xsb_ref[0:half], wg_ref, wu_ref, wd_ref, acc_ref.at[0:half])

        @pl.when(f == N_FCH - 1)
        def _():
            y_ref[...] = acc_ref[...]

    @pl.when(jnp.logical_and(i >= nu_ref[0], f == N_FCH - 1))
    def _():
        y_ref[...] = jnp.zeros_like(y_ref)


def _moe_experts(xs, tile_expert, n_used, tile_rows, wg, wu, wd, idx):
    def wspec(shape, col):
        def imap(i, f, te, nu, rows):
            fe = jnp.where(i < nu[0], f, N_FCH - 1)
            return (idx, te[i], 0, fe) if col else (idx, te[i], fe, 0)
        return pl.BlockSpec(shape, imap)

    def kern(te, nu, rows, xs_ref, wg_ref, wu_ref, wd_ref, y_ref, xsb_ref, acc_ref):
        _moe_kernel(te, nu, rows, xs_ref, wg_ref.at[0], wu_ref.at[0], wd_ref.at[0], y_ref, xsb_ref, acc_ref)

    return pl.pallas_call(
        kern,
        grid_spec=pltpu.PrefetchScalarGridSpec(
            num_scalar_prefetch=3,
            grid=(MOE_TILES, N_FCH),
            in_specs=[pl.BlockSpec((MOE_TM, D), lambda i, f, te, nu, rows: (jnp.minimum(i, nu[0] - 1), 0)),
                      wspec((1, 1, D, TF), True), wspec((1, 1, D, TF), True), wspec((1, 1, TF, D), False)],
            out_specs=pl.BlockSpec((MOE_TM, D), lambda i, f, te, nu, rows: (i, 0)),
            scratch_shapes=[pltpu.VMEM((MOE_TM, D), BF16), pltpu.VMEM((MOE_TM, D), F32)],
        ),
        out_shape=jax.ShapeDtypeStruct((MOE_ROWS, D), F32),
        compiler_params=_params(("arbitrary", "arbitrary")),
        name="moe_experts",
    )(tile_expert, n_used, tile_rows, xs, wg, wu, wd)


CTX_COMBINE_TILES = T_CTX // COMBINE_TM


def _combine_kernel(dest_ref, x_ref, y_hbm, gate_ref, g2_ref, ng_ref, sc_ref, sh_ref, *rest, final):
    *outs, ybuf, sem = rest
    j = pl.program_id(0)
    slot = j % 2

    def issue(tile, s):
        base = tile * (2 * COMBINE_TM)

        def body(g, carry):
            r0 = pl.multiple_of(g * ROW_UNROLL, ROW_UNROLL)
            for u in range(ROW_UNROLL):
                for k in range(2):
                    d = dest_ref[base + 2 * (r0 + u) + k]
                    rows = ybuf.at[s, pl.ds(k * COMBINE_TM + r0, ROW_UNROLL)]
                    pltpu.make_async_copy(y_hbm.at[pl.ds(d, 1)], rows.at[pl.ds(u, 1)], sem.at[s]).start()
            return carry

        lax.fori_loop(0, COMBINE_TM // ROW_UNROLL, body, 0)

    @pl.when(j == 0)
    def _():
        issue(0, 0)

    @pl.when(j + 1 < pl.num_programs(0))
    def _():
        issue(j + 1, 1 - slot)

    _wait_rows(y_hbm, ybuf.at[slot], sem.at[slot], 2 * COMBINE_TM)
    gates = gate_ref[...]
    moe = gates[:, 0:1] * ybuf[slot, 0:COMBINE_TM] + gates[:, 1:2] * ybuf[slot, COMBINE_TM:2 * COMBINE_TM]
    x = x_ref[...] + g2_ref[0, 0, 0] * moe
    if final:
        yp_ref, ys_ref = outs
        y = _rms_mod(x, ng_ref[...], 0.0, 0.0)

        @pl.when(j < CTX_COMBINE_TILES)
        def _():
            yp_ref[...] = y

        @pl.when(j >= CTX_COMBINE_TILES)
        def _():
            ys_ref[...] = y
    else:
        xo_ref, hno_ref = outs
        xo_ref[...] = x
        hno_ref[...] = _rms_mod(x, ng_ref[...], sc_ref[0, 0, 0], sh_ref[0, 0, 0]).astype(hno_ref.dtype)


def _moe_combine(x, y, dest_flat, gates, mods, layer, next_g, final):
    row = lambda n=D: pl.BlockSpec((COMBINE_TM, n), lambda j, dest: (j, 0))
    vec = lambda: pl.BlockSpec((1, D), lambda j, dest: (0, 0))
    mod = lambda l, k: pl.BlockSpec((1, 1, 1, 1, D), lambda j, dest: (l, k, _cond_of_tile(j, COMBINE_TM), 0, 0))
    nl = layer if final else layer + 1
    if final:
        out_specs = [pl.BlockSpec((COMBINE_TM, D), lambda j, dest: (jnp.minimum(j, CTX_COMBINE_TILES - 1), 0)),
                     pl.BlockSpec((COMBINE_TM, D), lambda j, dest: (jnp.maximum(j - CTX_COMBINE_TILES, 0), 0))]
        out_shape = [jax.ShapeDtypeStruct((T_CTX, D), F32), jax.ShapeDtypeStruct((T_LAT, D), F32)]
    else:
        out_specs = [row(), row()]
        out_shape = [jax.ShapeDtypeStruct((T, D), F32), jax.ShapeDtypeStruct((T, D), BF16)]
    return pl.pallas_call(
        functools.partial(_combine_kernel, final=final),
        grid_spec=pltpu.PrefetchScalarGridSpec(
            num_scalar_prefetch=1,
            grid=(T // COMBINE_TM,),
            in_specs=[row(), pl.BlockSpec(memory_space=pl.ANY), row(2), mod(layer, 5), vec(), mod(nl, 1), mod(nl, 0)],
            out_specs=out_specs,
            scratch_shapes=[pltpu.VMEM((2, 2 * COMBINE_TM, D), F32), pltpu.SemaphoreType.DMA((2,))],
        ),
        out_shape=out_shape,
        compiler_params=_params(("arbitrary",)),
        name="moe_combine",
    )(dest_flat, x, y, gates, mods, next_g.reshape(1, D), mods, mods)


LRU_CHUNK = 256


def _shift_rows(v, s, row):
    n = v.shape[0]
    rolled = pltpu.roll(v, s % n, axis=0)
    keep = (row >= s) if s > 0 else (row < n + s)
    return jnp.where(keep, rolled, 0.0)


W_IN_CHUNK = 512
N_W_IN_CHUNKS = IN_W // W_IN_CHUNK


def _lru_kernel(hn_ref, h0_ref, win_hbm, cw_ref, cb_ref, wbd_ref, bbd_ref, lam_ref, scw_ref, scb_ref,
                y_ref, st_ref, winb_ref, stage_ref, stage_sem, proj_ref, xc_ref, a_ref, b_ref, *, seq_len, idx):
    @pl.when(pl.program_id(0) == 0)
    def _():
        copies = [pltpu.make_async_copy(win_hbm.at[idx, :, pl.ds(j * W_IN_CHUNK, W_IN_CHUNK)],
                                        stage_ref.at[j % 2], stage_sem.at[j % 2]) for j in range(N_W_IN_CHUNKS)]
        copies[0].start()
        for j in range(N_W_IN_CHUNKS):
            if j + 1 < N_W_IN_CHUNKS:
                copies[j + 1].start()
            copies[j].wait()
            winb_ref[:, j * W_IN_CHUNK:(j + 1) * W_IN_CHUNK] = stage_ref[j % 2].astype(BF16)

    proj_ref[...] = jnp.dot(hn_ref[...], winb_ref[...], preferred_element_type=F32)
    row = lax.broadcasted_iota(jnp.int32, (seq_len, 1), 0)
    x = proj_ref[:, 0:LRU_W]
    xc =(cb_ref[...] + _shift_rows(x, 2, row) * cw_ref[0:1] + _shift_rows(x, 1, row) * cw_ref[1:2]
          + x * cw_ref[2:3] + _shift_rows(x, -1, row) * cw_ref[3:4])
    xc_ref[...] = xc

    lam = lam_ref[...]
    neg_csp = -LRU_C * (jnp.maximum(-lam, 0.0) + jnp.log1p(jnp.exp(-jnp.abs(lam))))

    for c in range(seq_len // LRU_CHUNK):
        rows = pl.ds(c * LRU_CHUNK, LRU_CHUNK)
        xcc = xc_ref[rows, :]
        gates = jnp.dot(xcc.astype(BF16), wbd_ref[...], preferred_element_type=F32) + bbd_ref[...]
        for d in range(2):
            r = jax.nn.sigmoid(gates[:, (2 * d) * LRU_W:(2 * d + 1) * LRU_W])
            i = jax.nn.sigmoid(gates[:, (2 * d + 1) * LRU_W:(2 * d + 2) * LRU_W])
            log_a = neg_csp[d:d + 1] * r
            t = jnp.tanh(log_a)
            one_minus_a2 = -2.0 * t / (1.0 - t)
            a_ref[d, rows, :] = jnp.exp(log_a)
            b_ref[d, rows, :] = jnp.sqrt(one_minus_a2) * (i * xcc)

    n_grp = seq_len // 8

    def scan_body(g, carry):
        hf, hb = carry
        rf = pl.ds(pl.multiple_of(g * 8, 8), 8)
        rb = pl.ds(pl.multiple_of((n_grp - 1 - g) * 8, 8), 8)
        af, bf = a_ref[0, rf, :], b_ref[0, rf, :]
        ab, bb = a_ref[1, rb, :], b_ref[1, rb, :]
        outs_f, outs_b = [], []
        for j in range(8):
            hf = af[j:j + 1] * hf + bf[j:j + 1]
            outs_f.append(hf)
            hb = ab[7 - j:8 - j] * hb + bb[7 - j:8 - j]
            outs_b.append(hb)
        b_ref[0, rf, :] = jnp.concatenate(outs_f, axis=0)
        b_ref[1, rb, :] = jnp.concatenate(outs_b[::-1], axis=0)
        return hf, hb

    hf, hb = lax.fori_loop(0, n_grp, scan_body, (h0_ref[0, 0:1, :], h0_ref[0, 1:2, :]))
    st_ref[0, 0:1, :] = hf
    st_ref[0, 1:2, :] = hb

    h_sum = b_ref[0] + b_ref[1]
    y_ref[:, 0:LRU_W] = (h_sum * jax.nn.gelu(proj_ref[:, LRU_W:2 * LRU_W])).astype(y_ref.dtype)
    o = 2 * LRU_W
    cv = proj_ref[:, o + SC_W:o + 2 * SC_W] * proj_ref[:, o + 2 * SC_W:o + 3 * SC_W]
    conv = scb_ref[...] + _shift_rows(cv, 1, row) * scw_ref[0:1] + cv * scw_ref[1:2] + _shift_rows(cv, -1, row) * scw_ref[2:3]
    y_ref[:, LRU_W:LRU_W + SC_W] = (proj_ref[:, o:o + SC_W] * conv).astype(y_ref.dtype)


def _lru_mixer(hn, h0, w_in, idx, cw, cb, wbd, bbd, lam, scw, scb, *, latent):
    if latent:
        n_seq, seq_len, blk0 = DEC_BATCH, DEC_SEQ, T_CTX // DEC_SEQ
    else:
        n_seq, seq_len, blk0 = BATCH, SEQ, 0
    full = lambda shape: pl.BlockSpec(shape, lambda b: (0,) * len(shape))
    return pl.pallas_call(
        functools.partial(_lru_kernel, seq_len=seq_len, idx=idx),
        grid=(n_seq,),
        in_specs=[pl.BlockSpec((seq_len, D), lambda b: (blk0 + b, 0)),
                  pl.BlockSpec((1, 2, LRU_W), lambda b: (b, 0, 0)),
                  pl.BlockSpec(memory_space=pl.ANY),
                  full((4, LRU_W)), full((1, LRU_W)), full((LRU_W, 4 * LRU_W)), full((1, 4 * LRU_W)),
                  full((2, LRU_W)), full((3, SC_W)), full((1, SC_W))],
        out_specs=[pl.BlockSpec((seq_len, D), lambda b: (b, 0)),
                   pl.BlockSpec((1, 2, LRU_W), lambda b: (b, 0, 0))],
        out_shape=[jax.ShapeDtypeStruct((n_seq * seq_len, D), BF16),
                   jax.ShapeDtypeStruct((n_seq, 2, LRU_W), F32)],
        scratch_shapes=[pltpu.VMEM((D, IN_W), BF16),
                        pltpu.VMEM((2, D, W_IN_CHUNK), F32),
                        pltpu.SemaphoreType.DMA((2,)),
                        pltpu.VMEM((seq_len, IN_W), F32),
                        pltpu.VMEM((seq_len, LRU_W), F32),
                        pltpu.VMEM((2, seq_len, LRU_W), F32),
                        pltpu.VMEM((2, seq_len, LRU_W), F32)],
        compiler_params=_params(("arbitrary",)),
        name="lru_mixer",
    )(hn, h0, w_in, cw, cb.reshape(1, LRU_W), wbd, bbd, lam, scw, scb.reshape(1, SC_W))


def _block_diag_dense(w):
    eye = jnp.eye(LRU_HEADS, dtype=w.dtype)
    return jnp.einsum('hij,hg->higj', w, eye).reshape(LRU_W, LRU_W)


def _route(idx):
    e_flat = idx.reshape(-1)
    onehot = (e_flat[:, None] == jnp.arange(N_EXPERTS)[None, :]).astype(jnp.int32)
    csum = jnp.cumsum(onehot, axis=0)
    counts = csum[-1]
    rank = jnp.take_along_axis(csum, e_flat[:, None], axis=1)[:, 0] - 1
    padded = (counts + MOE_TM - 1) // MOE_TM * MOE_TM
    ends = jnp.cumsum(padded)
    dest = ((ends - padded)[e_flat] + rank).astype(jnp.int32)
    tile_end = ends // MOE_TM
    n_used = tile_end[-1]
    tiles = jnp.minimum(jnp.arange(MOE_TILES), n_used - 1)
    tile_expert = jnp.sum(tiles[:, None] >= tile_end[None, :], axis=1).astype(jnp.int32)
    group_tile = jnp.arange(MOE_TILES) - ((ends - padded) // MOE_TM)[tile_expert]
    tile_rows = jnp.clip(counts[tile_expert] - group_tile * MOE_TM, 0, MOE_TM).astype(jnp.int32)
    return dest, tile_expert, n_used.reshape(1).astype(jnp.int32), tile_rows


def kernel(x_prompt, x_sample, state_lru, cache_k, cache_v, c, c_ctx, norm1_g, norm2_g, w_mod, b_mod, w_in, lru_conv_w, lru_conv_b, lru_wa, lru_ba, lru_wx, lru_bx, lru_lam, sc_conv_w, sc_conv_b, w_out, ffn_w_gate, ffn_w_up, ffn_w_down, w_qkv, w_o, rpb, w_router, moe_w_gate, moe_w_up, moe_w_down, final_g):
    cond = jnp.concatenate([c_ctx[None, :], c, jnp.zeros((COND_ROWS - N_COND, D), F32)], axis=0)
    mods = _modulation(cond, w_mod, b_mod).reshape(DEPTH, 6, COND_ROWS, 1, D)

    x, hn = _first_norm(x_prompt.reshape(T_CTX, D), x_sample.reshape(T_LAT, D), norm1_g[0], mods, 0)
    cache_kt = jnp.swapaxes(cache_k, -1, -2)
    cache_vt = jnp.swapaxes(cache_v, -1, -2)
    zero_state = jnp.zeros((BATCH, 2, LRU_W), F32)
    states = []
    new_k = jnp.zeros((BATCH, DEPTH // 2, N_HEADS, HEAD_DIM, SEQ), F32)
    new_v = jnp.zeros((BATCH, DEPTH // 2, N_HEADS, HEAD_DIM, SEQ), F32)
    for l in range(DEPTH):
        idx = l // 2
        if l % 2 == 0:
            wbd = jnp.concatenate([_block_diag_dense(lru_wa[idx, 0]), _block_diag_dense(lru_wx[idx, 0]),
                                   _block_diag_dense(lru_wa[idx, 1]), _block_diag_dense(lru_wx[idx, 1])],
                                  axis=1).astype(BF16)
            bbd = jnp.concatenate([lru_ba[idx, 0], lru_bx[idx, 0], lru_ba[idx, 1], lru_bx[idx, 1]]).reshape(1, 4 * LRU_W)
            lru_args = (lru_conv_w[idx], lru_conv_b[idx], wbd, bbd, lru_lam[idx], sc_conv_w[idx], sc_conv_b[idx])
            y_ctx, st = _lru_mixer(hn, zero_state, w_in, idx, *lru_args, latent=False)
            y_lat, _ = _lru_mixer(hn, state_lru[:, idx], w_in, idx, *lru_args, latent=True)
            states.append(st)
            x, hn2 = _proj_res_norm(y_ctx, y_lat, w_out, idx, x, norm2_g[l], mods, l)
            x, hn = _ffn(hn2, ffn_w_gate, ffn_w_up, ffn_w_down, idx, x, mods, l, norm1_g[l + 1])
        else:
            q_scale = HEAD_DIM ** -0.5
            q_ctx = _matmul_heads(hn, w_qkv, idx, 0, latent=False, out_dtype=BF16, scale=q_scale)
            new_k = _matmul_heads(hn, w_qkv, idx, 1, latent=False, out_dtype=F32, into=new_k, slot=idx, n_slots=2,
                                  transposed=True)
            new_v = _matmul_heads(hn, w_qkv, idx, 2, latent=False, out_dtype=F32, into=new_v, slot=idx, n_slots=2,
                                  transposed=True)
            q_lat = _matmul_heads(hn, w_qkv, idx, 0, latent=True, out_dtype=BF16, scale=q_scale)
            k_lat = _matmul_heads(hn, w_qkv, idx, 1, latent=True, out_dtype=F32)
            v_lat = _matmul_heads(hn, w_qkv, idx, 2, latent=True, out_dtype=F32)
            o_ctx = _context_attention(q_ctx, new_k, new_v, idx)
            o_lat = _neighbourhood_attention(q_lat, k_lat, v_lat, cache_kt, cache_vt, _window_bias(rpb[idx]), idx)
            x, hn2, top_i, top_p = _proj_res_norm(o_ctx, o_lat, w_o, idx, x, norm2_g[l], mods, l, w_router[idx])
            dest, tile_expert, n_used, tile_rows = _route(top_i)
            xs = _moe_scatter(hn2, dest)
            y = _moe_experts(xs, tile_expert, n_used, tile_rows, moe_w_gate, moe_w_up, moe_w_down, idx)
            if l == DEPTH - 1:
                y_prompt, y_sample = _moe_combine(x, y, dest, top_p, mods, l, final_g, True)
            else:
                x, hn = _moe_combine(x, y, dest, top_p, mods, l, norm1_g[l + 1], False)

    return (y_prompt.reshape(BATCH, SEQ, D), y_sample.reshape(DEC_BATCH, DEC_SEQ, D), jnp.stack(states, axis=1),
            jnp.swapaxes(new_k, -1, -2), jnp.swapaxes(new_v, -1, -2))
```

```python
import functools

import jax
import jax.numpy as jnp
import numpy as np
from jax import lax
from jax.experimental import pallas as pl
from jax.experimental.pallas import tpu as pltpu

F32 = jnp.float32
BF16 = jnp.bfloat16

D = 1024
BATCH = 32
SEQ = 256
DEPTH = 4
DEC_BATCH = 2
DEC_SEQ = 1024
PAST_LEN = 256
GRID_W = 64
GRID_ROWS = DEC_SEQ // GRID_W
LRU_W = 512
LRU_HEADS = 8
LRU_BLOCK = LRU_W // LRU_HEADS
LRU_C = 8.0
SC_W = 512
IN_W = 2 * LRU_W + 3 * SC_W
N_HEADS = 16
HEAD_DIM = D // N_HEADS
WIN_R = 8
WIN_C = 16
D_FF = 2816
N_EXPERTS = 8
EPS = 1e-6
NEG_INF = -1e30

T_CTX = BATCH * SEQ
T_LAT = DEC_BATCH * DEC_SEQ
T = T_CTX + T_LAT
N_COND = 3
COND_ROWS = 8

TM = 1024
TF = 256
N_FCH = D_FF // TF
MOE_TM = 1024
MOE_TILES = (2 * T + N_EXPERTS * (MOE_TM - 1)) // MOE_TM
MOE_ROWS = MOE_TILES * MOE_TM
VMEM_LIMIT = 52 * 1024 * 1024


def _params(sem):
    return pltpu.CompilerParams(dimension_semantics=sem, vmem_limit_bytes=VMEM_LIMIT)


def _cond_of_tile(i, tm=TM):
    r = i * tm
    return jnp.where(r < T_CTX, 0, 1 + (r - T_CTX) // DEC_SEQ)


def _mod_spec(layer, k, grid_rank=1, tm=TM):
    if grid_rank == 1:
        return pl.BlockSpec((1, 1, 1, 1, D), lambda i: (layer, k, _cond_of_tile(i, tm), 0, 0))
    return pl.BlockSpec((1, 1, 1, 1, D), lambda i, f, *_: (layer, k, _cond_of_tile(i, tm), 0, 0))


def _rms_mod(x, g, scale, shift):
    y = x * lax.rsqrt(jnp.mean(x * x, axis=-1, keepdims=True) + EPS)
    return (y * g) * (1.0 + scale) + shift


def _mod_kernel(c_ref, w_ref, b_ref, o_ref):
    c = c_ref[...]
    s = c * jax.nn.sigmoid(c)
    o_ref[0, 0] = jnp.dot(s.astype(BF16), w_ref[0].astype(BF16), preferred_element_type=F32) + b_ref[0, 0]


def _modulation(cond, w_mod, b_mod):
    return pl.pallas_call(
        _mod_kernel,
        grid=(DEPTH, 6),
        in_specs=[
            pl.BlockSpec((COND_ROWS, D), lambda l, k: (0, 0)),
            pl.BlockSpec((1, D, D), lambda l, k: (l, 0, k)),
            pl.BlockSpec((1, 1, 1, D), lambda l, k: (l, k, 0, 0)),
        ],
        out_specs=pl.BlockSpec((1, 1, COND_ROWS, D), lambda l, k: (l, k, 0, 0)),
        out_shape=jax.ShapeDtypeStruct((DEPTH, 6, COND_ROWS, D), F32),
        compiler_params=_params(("arbitrary", "arbitrary")),
        name="adaln_mod",
    )(cond, w_mod, b_mod.reshape(DEPTH, 6, 1, D))


def _norm_kernel(xc_ref, xl_ref, g_ref, sc_ref, sh_ref, x_ref, o_ref):
    x = jnp.where(pl.program_id(0) < T_CTX // TM, xc_ref[...], xl_ref[...])
    x_ref[...] = x
    o_ref[...] = _rms_mod(x, g_ref[...], sc_ref[0, 0, 0], sh_ref[0, 0, 0]).astype(o_ref.dtype)


def _first_norm(x_ctx, x_lat, g, mods, layer):
    n_ctx = T_CTX // TM
    return pl.pallas_call(
        _norm_kernel,
        grid=(T // TM,),
        in_specs=[
            pl.BlockSpec((TM, D), lambda i: (jnp.minimum(i, n_ctx - 1), 0)),
            pl.BlockSpec((TM, D), lambda i: (jnp.maximum(i - n_ctx, 0), 0)),
            pl.BlockSpec((1, D), lambda i: (0, 0)),
            _mod_spec(layer, 1),
            _mod_spec(layer, 0),
        ],
        out_specs=[pl.BlockSpec((TM, D), lambda i: (i, 0)), pl.BlockSpec((TM, D), lambda i: (i, 0))],
        out_shape=[jax.ShapeDtypeStruct((T, D), F32), jax.ShapeDtypeStruct((T, D), BF16)],
        compiler_params=_params(("arbitrary",)),
        name="first_norm",
    )(x_ctx, x_lat, g.reshape(1, D), mods, mods)


QKV_TM = 512


def _qkv_kernel(a_ref, wq_ref, wk_ref, wv_ref, *rest, n_piece, piece_len, kv_transposed, aliased):
    if aliased:
        rest = rest[2:]
    q_ref, k_ref, v_ref, wb_ref, acc_ref = rest

    @pl.when(pl.program_id(0) == 0)
    def _():
        for part, w_ref in enumerate((wq_ref, wk_ref, wv_ref)):
            wb_ref[part] = w_ref[0].astype(BF16)

    a = a_ref[...]
    q = jnp.dot(a, wb_ref[0], preferred_element_type=F32) * (HEAD_DIM ** -0.5)
    for s in range(n_piece):
        for h in range(N_HEADS):
            q_ref[s, 0, h] = q[s * piece_len:(s + 1) * piece_len, h * HEAD_DIM:(h + 1) * HEAD_DIM].astype(q_ref.dtype)
    for part, o_ref in ((1, k_ref), (2, v_ref)):
        acc = jnp.dot(a, wb_ref[part], preferred_element_type=F32)
        if kv_transposed:
            acc_ref[...] = acc
            acc_t = acc_ref[...].T
            for s in range(n_piece):
                for h in range(N_HEADS):
                    o_ref[s, 0, h] = acc_t[h * HEAD_DIM:(h + 1) * HEAD_DIM, s * piece_len:(s + 1) * piece_len]
        else:
            for s in range(n_piece):
                for h in range(N_HEADS):
                    o_ref[s, 0, h] = acc[s * piece_len:(s + 1) * piece_len, h * HEAD_DIM:(h + 1) * HEAD_DIM]


def _qkv_heads(hn, w_qkv, idx, *, latent, k_into=None, v_into=None):
    if latent:
        n_batch, seq_len, row0, n_slots, slot = DEC_BATCH, DEC_SEQ, T_CTX // QKV_TM, 1, 0
    else:
        n_batch, seq_len, row0, n_slots, slot = BATCH, SEQ, 0, DEPTH // 2, idx
    piece_len = min(seq_len, QKV_TM)
    n_piece = QKV_TM // piece_len
    per_seq = seq_len // piece_len
    aliased = k_into is not None

    def head_spec(tail_t, s):
        if tail_t:
            shape = (n_piece, 1, N_HEADS, HEAD_DIM, piece_len)
            return pl.BlockSpec(shape, lambda i: (i, s, 0, 0, 0))
        shape = (n_piece, 1, N_HEADS, piece_len, HEAD_DIM)
        return pl.BlockSpec(shape, lambda i: (i // per_seq, s, 0, i % per_seq, 0))

    wspec = lambda part: pl.BlockSpec((1, D, D), lambda i: (idx, 0, part), pipeline_mode=pl.Buffered(1))
    in_specs = [pl.BlockSpec((QKV_TM, D), lambda i: (row0 + i, 0)), wspec(0), wspec(1), wspec(2)]
    args = [hn, w_qkv, w_qkv, w_qkv]
    aliases = {}
    kv_tail = (HEAD_DIM, seq_len) if aliased else (seq_len, HEAD_DIM)
    if aliased:
        in_specs += [pl.BlockSpec(memory_space=pl.ANY), pl.BlockSpec(memory_space=pl.ANY)]
        args += [k_into, v_into]
        aliases = {4: 1, 5: 2}
    kv_shape = jax.ShapeDtypeStruct((n_batch, n_slots, N_HEADS) + kv_tail, F32)
    return pl.pallas_call(
        functools.partial(_qkv_kernel, n_piece=n_piece, piece_len=piece_len, kv_transposed=aliased, aliased=aliased),
        grid=(n_batch * seq_len // QKV_TM,),
        in_specs=in_specs,
        out_specs=[head_spec(False, 0), head_spec(aliased, slot), head_spec(aliased, slot)],
        out_shape=[jax.ShapeDtypeStruct((n_batch, 1, N_HEADS, seq_len, HEAD_DIM), BF16), kv_shape, kv_shape],
        scratch_shapes=[pltpu.VMEM((3, D, D), BF16), pltpu.VMEM((QKV_TM, D), F32)],
        input_output_aliases=aliases,
        compiler_params=_params(("arbitrary",)),
        name="qkv_heads",
    )(*args)


def _dot_nt(a, b):
    return lax.dot_general(a, b, (((1,), (1,)), ((), ())), preferred_element_type=F32)


def _softmax_pv(s_list, v_list, v_transposed=False):
    m = s_list[0].max(axis=-1, keepdims=True)
    for s in s_list[1:]:
        m = jnp.maximum(m, s.max(axis=-1, keepdims=True))
    den = None
    out = None
    for s, v in zip(s_list, v_list):
        p = jnp.exp(s - m)
        d = p.sum(axis=-1, keepdims=True)
        pb = p.astype(BF16)
        o = _dot_nt(pb, v) if v_transposed else jnp.dot(pb, v, preferred_element_type=F32)
        den = d if den is None else den + d
        out = o if out is None else out + o
    return out / den


HEAD_PAIR_W = 2 * HEAD_DIM


def _ctx_attn_kernel(q_ref, kt_ref, vt_ref, o_ref):
    for hp in range(N_HEADS // 2):
        pair = []
        for h in (2 * hp, 2 * hp + 1):
            s = jnp.dot(q_ref[0, 0, h], kt_ref[0, 0, h].astype(BF16), preferred_element_type=F32)
            pair.append(_softmax_pv([s], [vt_ref[0, 0, h].astype(BF16)], v_transposed=True))
        o_ref[:, hp * HEAD_PAIR_W:(hp + 1) * HEAD_PAIR_W] = jnp.concatenate(pair, axis=-1).astype(o_ref.dtype)


def _context_attention(q, kt, vt, slot):
    kv_spec = pl.BlockSpec((1, 1, N_HEADS, HEAD_DIM, SEQ), lambda b: (b, slot, 0, 0, 0))
    return pl.pallas_call(
        _ctx_attn_kernel,
        grid=(BATCH,),
        in_specs=[pl.BlockSpec((1, 1, N_HEADS, SEQ, HEAD_DIM), lambda b: (b, 0, 0, 0, 0)), kv_spec, kv_spec],
        out_specs=pl.BlockSpec((SEQ, D), lambda b: (b, 0)),
        out_shape=jax.ShapeDtypeStruct((T_CTX, D), BF16),
        compiler_params=_params(("arbitrary",)),
        name="ctx_attention",
    )(q, kt, vt)


def _row_start(r):
    return min(max(r - WIN_R // 2, 0), GRID_ROWS - WIN_R)


def _nbr_attn_kernel(q_ref, k_ref, v_ref, kct_ref, vct_ref, bias_ref, o_ref):
    kct = [kct_ref[0, 0, hh].astype(BF16) for hh in range(2)]
    vct = [vct_ref[0, 0, hh].astype(BF16) for hh in range(2)]
    for r0, r1 in _ROW_GROUPS:
        rs = _row_start(r0)
        n_q = (r1 - r0) * GRID_W
        pair = []
        for hh in range(2):
            q = q_ref[0, 0, hh, r0 * GRID_W:r1 * GRID_W, :]
            kw = k_ref[0, 0, hh, rs * GRID_W:(rs + WIN_R) * GRID_W, :].astype(BF16)
            vw = v_ref[0, 0, hh, rs * GRID_W:(rs + WIN_R) * GRID_W, :].astype(BF16)
            bias = bias_ref[hh, r0 - rs:r1 - rs].reshape(n_q, WIN_R * GRID_W)
            s_win = _dot_nt(q, kw) + bias
            s_ctx = jnp.dot(q, kct[hh], preferred_element_type=F32)
            m = jnp.maximum(s_win.max(axis=-1, keepdims=True), s_ctx.max(axis=-1, keepdims=True))
            p_win = jnp.exp(s_win - m)
            p_ctx = jnp.exp(s_ctx - m)
            den = p_win.sum(axis=-1, keepdims=True) + p_ctx.sum(axis=-1, keepdims=True)
            out = (jnp.dot(p_win.astype(BF16), vw, preferred_element_type=F32)
                   + _dot_nt(p_ctx.astype(BF16), vct[hh]))
            pair.append(out / den)
        o_ref[r0 * GRID_W:r1 * GRID_W, :] = jnp.concatenate(pair, axis=-1).astype(o_ref.dtype)


def _row_groups():
    groups, r0 = [], 0
    for r in range(1, GRID_ROWS + 1):
        if r == GRID_ROWS or _row_start(r) != _row_start(r0):
            groups.append((r0, r))
            r0 = r
    return groups


_ROW_GROUPS = _row_groups()


def _window_bias(rpb_l):
    col = np.arange(GRID_W)
    col_start = np.clip(col - WIN_C // 2, 0, GRID_W - WIN_C)
    col_mask = (col[None, :] >= col_start[:, None]) & (col[None, :] < col_start[:, None] + WIN_C)
    dc = np.clip(col[None, :] - col[:, None], -(WIN_C - 1), WIN_C - 1) + (WIN_C - 1)
    onehot = (dc[None, :, :] == np.arange(2 * WIN_C - 1)[:, None, None]).astype(np.float32)
    rows = jnp.stack([rpb_l[:, WIN_R - 1 - v:2 * WIN_R - 1 - v, :] for v in range(WIN_R)], axis=1)
    bias = jnp.einsum('hvkd,dqc->hvqkc', rows, jnp.asarray(onehot), precision=lax.Precision.HIGHEST)
    bias = jnp.where(jnp.asarray(col_mask)[None, None, :, None, :], bias, NEG_INF)
    return bias.reshape(N_HEADS, WIN_R, GRID_W, WIN_R * GRID_W)


def _neighbourhood_attention(q, k, v, cache_kt, cache_vt, bias, slot):
    lat = lambda: pl.BlockSpec((1, 1, 2, DEC_SEQ, HEAD_DIM), lambda b, hp: (b, 0, hp, 0, 0))
    ctx = lambda: pl.BlockSpec((1, 1, 2, HEAD_DIM, PAST_LEN), lambda b, hp: (b, slot, hp, 0, 0))
    return pl.pallas_call(
        _nbr_attn_kernel,
        grid=(DEC_BATCH, N_HEADS // 2),
        in_specs=[lat(), lat(), lat(), ctx(), ctx(),
                  pl.BlockSpec((2, WIN_R, GRID_W, WIN_R * GRID_W), lambda b, hp: (hp, 0, 0, 0))],
        out_specs=pl.BlockSpec((DEC_SEQ, HEAD_PAIR_W), lambda b, hp: (b, hp)),
        out_shape=jax.ShapeDtypeStruct((T_LAT, D), BF16),
        compiler_params=_params(("arbitrary", "arbitrary")),
        name="nbr_attention",
    )(q, k, v, cache_kt, cache_vt, bias)


def _top2(logits):
    col = lax.broadcasted_iota(jnp.int32, logits.shape, 1)
    m1 = jnp.max(logits, axis=-1, keepdims=True)
    i1 = jnp.min(jnp.where(logits == m1, col, N_EXPERTS), axis=-1, keepdims=True)
    rest = jnp.where(col == i1, -jnp.inf, logits)
    m2 = jnp.max(rest, axis=-1, keepdims=True)
    i2 = jnp.min(jnp.where(rest == m2, col, N_EXPERTS), axis=-1, keepdims=True)
    e = jnp.exp(m2 - m1)
    return i1, i2, 1.0 / (1.0 + e), e / (1.0 + e)


CTX_TILES = T_CTX // TM


def _proj_res_norm_kernel(ac_ref, al_ref, w_ref, x_ref, g1_ref, ng_ref, sc_ref, sh_ref, *rest, router):
    if router:
        wr_ref, xo_ref, hn_ref, idx_ref, gate_ref, wb_ref = rest
    else:
        xo_ref, hn_ref, wb_ref = rest

    @pl.when(pl.program_id(0) == 0)
    def _():
        wb_ref[...] = w_ref[0].astype(BF16)

    a = jnp.where(pl.program_id(0) < CTX_TILES, ac_ref[...], al_ref[...])
    mix = jnp.dot(a, wb_ref[...], preferred_element_type=F32)
    x = x_ref[...] + g1_ref[0, 0, 0] * mix
    xo_ref[...] = x
    hn = _rms_mod(x, ng_ref[...], sc_ref[0, 0, 0], sh_ref[0, 0, 0])
    hn_ref[...] = hn.astype(hn_ref.dtype)
    if router:
        hi = hn.astype(BF16)
        lo = (hn - hi.astype(F32)).astype(BF16)
        l_hi = jnp.dot(hi, wr_ref[...], preferred_element_type=F32)
        l_lo = jnp.dot(lo, wr_ref[...], preferred_element_type=F32)
        logits = l_hi[:, :N_EXPERTS] + l_hi[:, N_EXPERTS:] + l_lo[:, :N_EXPERTS]
        i1, i2, p1, p2 = _top2(logits)
        first = lax.broadcasted_iota(jnp.int32, (logits.shape[0], 2), 1) == 0
        idx_ref[...] = jnp.where(first, i1, i2)
        gate_ref[...] = jnp.where(first, p1, p2)


def _proj_res_norm(a_ctx, a_lat, w, idx, x, norm_g, mods, layer, w_router=None):
    router = w_router is not None
    row = lambda n: pl.BlockSpec((TM, n), lambda i: (i, 0))
    in_specs = [pl.BlockSpec((TM, D), lambda i: (jnp.minimum(i, CTX_TILES - 1), 0)),
                pl.BlockSpec((TM, D), lambda i: (jnp.maximum(i - CTX_TILES, 0), 0)),
                pl.BlockSpec((1, D, D), lambda i: (idx, 0, 0)), row(D),
                _mod_spec(layer, 2), pl.BlockSpec((1, D), lambda i: (0, 0)), _mod_spec(layer, 4), _mod_spec(layer, 3)]
    args = [a_ctx, a_lat, w, x, mods, norm_g.reshape(1, D), mods, mods]
    out_specs = [row(D), row(D)]
    out_shape = [jax.ShapeDtypeStruct((T, D), F32), jax.ShapeDtypeStruct((T, D), F32 if router else BF16)]
    if router:
        w_hi = w_router.astype(BF16)
        w_lo = (w_router - w_hi.astype(F32)).astype(BF16)
        in_specs.append(pl.BlockSpec((D, 2 * N_EXPERTS), lambda i: (0, 0)))
        args.append(jnp.concatenate([w_hi, w_lo], axis=1))
        out_specs += [row(2), row(2)]
        out_shape += [jax.ShapeDtypeStruct((T, 2), jnp.int32), jax.ShapeDtypeStruct((T, 2), F32)]
    return pl.pallas_call(
        functools.partial(_proj_res_norm_kernel, router=router),
        grid=(T // TM,),
        in_specs=in_specs,
        out_specs=out_specs,
        out_shape=out_shape,
        scratch_shapes=[pltpu.VMEM((D, D), BF16)],
        compiler_params=_params(("arbitrary",)),
        name="proj_res_norm",
    )(*args)


def _swiglu_chunk(hn, wg_ref, wu_ref, wd_ref, acc_ref):
    g = jnp.dot(hn, wg_ref[0].astype(BF16), preferred_element_type=F32)
    u = jnp.dot(hn, wu_ref[0].astype(BF16), preferred_element_type=F32)
    h = (g * jax.nn.sigmoid(g) * u).astype(BF16)
    acc_ref[...] += jnp.dot(h, wd_ref[0].astype(BF16), preferred_element_type=F32)


def _ffn_kernel(hn_ref, wg_ref, wu_ref, wd_ref, x_ref, g2_ref, ng_ref, sc_ref, sh_ref, xo_ref, hno_ref, acc_ref):
    @pl.when(pl.program_id(1) == 0)
    def _():
        acc_ref[...] = jnp.zeros_like(acc_ref)

    _swiglu_chunk(hn_ref[...], wg_ref, wu_ref, wd_ref, acc_ref)

    @pl.when(pl.program_id(1) == N_FCH - 1)
    def _():
        x = x_ref[...] + g2_ref[0, 0, 0] * acc_ref[...]
        xo_ref[...] = x
        hno_ref[...] = _rms_mod(x, ng_ref[...], sc_ref[0, 0, 0], sh_ref[0, 0, 0]).astype(hno_ref.dtype)


def _ffn(hn, wg, wu, wd, idx, x, mods, layer, next_g):
    row = lambda: pl.BlockSpec((TM, D), lambda i, f: (i, 0))
    return pl.pallas_call(
        _ffn_kernel,
        grid=(T // TM, N_FCH),
        in_specs=[row(),
                  pl.BlockSpec((1, D, TF), lambda i, f: (idx, 0, f)),
                  pl.BlockSpec((1, D, TF), lambda i, f: (idx, 0, f)),
                  pl.BlockSpec((1, TF, D), lambda i, f: (idx, f, 0)),
                  row(), _mod_spec(layer, 5, 2), pl.BlockSpec((1, D), lambda i, f: (0, 0)),
                  _mod_spec(layer + 1, 1, 2), _mod_spec(layer + 1, 0, 2)],
        out_specs=[row(), row()],
        out_shape=[jax.ShapeDtypeStruct((T, D), F32), jax.ShapeDtypeStruct((T, D), BF16)],
        scratch_shapes=[pltpu.VMEM((TM, D), F32)],
        compiler_params=_params(("arbitrary", "arbitrary")),
        name="ffn",
    )(hn, wg, wu, wd, x, mods, next_g.reshape(1, D), mods, mods)


ROW_UNROLL = 8
SCATTER_TM = 1024
COMBINE_TM = 1024


def _wait_rows(src_ref, dst_ref, sem, n_rows):
    pltpu.make_async_copy(src_ref.at[pl.ds(0, n_rows)], dst_ref.at[pl.ds(0, n_rows)], sem).wait()


def _scatter_kernel(dest_ref, hn_ref, xs_in_ref, xs_ref, sem):
    del xs_in_ref
    base = pl.program_id(0) * (2 * SCATTER_TM)

    def body(g, carry):
        r0 = pl.multiple_of(g * ROW_UNROLL, ROW_UNROLL)
        for u in range(ROW_UNROLL):
            for k in range(2):
                d = dest_ref[base + 2 * (r0 + u) + k]
                pltpu.make_async_copy(hn_ref.at[pl.ds(r0, ROW_UNROLL)].at[pl.ds(u, 1)], xs_ref.at[pl.ds(d, 1)],
                                      sem).start(priority=k)
        return carry

    lax.fori_loop(0, SCATTER_TM // ROW_UNROLL, body, 0)
    for _ in range(2):
        _wait_rows(hn_ref, xs_ref, sem, SCATTER_TM)


def _moe_scatter(hn2, dest_flat):
    return pl.pallas_call(
        _scatter_kernel,
        grid_spec=pltpu.PrefetchScalarGridSpec(
            num_scalar_prefetch=1,
            grid=(T // SCATTER_TM,),
            in_specs=[pl.BlockSpec((SCATTER_TM, D), lambda j, dest: (j, 0)),
                      pl.BlockSpec(memory_space=pl.ANY)],
            out_specs=pl.BlockSpec(memory_space=pl.ANY),
            scratch_shapes=[pltpu.SemaphoreType.DMA],
        ),
        out_shape=jax.ShapeDtypeStruct((MOE_ROWS, D), F32),
        input_output_aliases={2: 0},
        compiler_params=_params(("arbitrary",)),
        name="moe_scatter",
    )(dest_flat, hn2, jnp.zeros((MOE_ROWS, D), F32))


def _moe_kernel(te_ref, nu_ref, rows_ref, xs_ref, wg_ref, wu_ref, wd_ref, y_ref, xsb_ref, acc_ref):
    i = pl.program_id(0)
    f = pl.program_id(1)
    half = MOE_TM // 2

    @pl.when(i < nu_ref[0])
    def _():
        @pl.when(f == 0)
        def _():
            xsb_ref[...] = xs_ref[...].astype(BF16)
            acc_ref[...] = jnp.zeros_like(acc_ref)

        @pl.when(rows_ref[i] > half)
        def _():
            _swiglu_chunk(xsb_ref[...], wg_ref, wu_ref, wd_ref, acc_ref)

        @pl.when(rows_ref[i] <= half)
        def _():
            _swiglu_chunk(xsb_ref[0:half], wg_ref, wu_ref, wd_ref, acc_ref.at[0:half])

        @pl.when(f == N_FCH - 1)
        def _():
            y_ref[...] = acc_ref[...]

    @pl.when(jnp.logical_and(i >= nu_ref[0], f == N_FCH - 1))
    def _():
        y_ref[...] = jnp.zeros_like(y_ref)


def _moe_experts(xs, tile_expert, n_used, tile_rows, wg, wu, wd, idx):
    def wspec(shape, col):
        def imap(i, f, te, nu, rows):
            fe = jnp.where(i < nu[0], f, N_FCH - 1)
            return (idx, te[i], 0, fe) if col else (idx, te[i], fe, 0)
        return pl.BlockSpec(shape, imap)

    def kern(te, nu, rows, xs_ref, wg_ref, wu_ref, wd_ref, y_ref, xsb_ref, acc_ref):
        _moe_kernel(te, nu, rows, xs_ref, wg_ref.at[0], wu_ref.at[0], wd_ref.at[0], y_ref, xsb_ref, acc_ref)

    return pl.pallas_call(
        kern,
        grid_spec=pltpu.PrefetchScalarGridSpec(
            num_scalar_prefetch=3,
            grid=(MOE_TILES, N_FCH),
            in_specs=[pl.BlockSpec((MOE_TM, D), lambda i, f, te, nu, rows: (jnp.minimum(i, nu[0] - 1), 0)),
                      wspec((1, 1, D, TF), True), wspec((1, 1, D, TF), True), wspec((1, 1, TF, D), False)],
            out_specs=pl.BlockSpec((MOE_TM, D), lambda i, f, te, nu, rows: (i, 0)),
            scratch_shapes=[pltpu.VMEM((MOE_TM, D), BF16), pltpu.VMEM((MOE_TM, D), F32)],
        ),
        out_shape=jax.ShapeDtypeStruct((MOE_ROWS, D), F32),
        compiler_params=_params(("arbitrary", "arbitrary")),
        name="moe_experts",
    )(tile_expert, n_used, tile_rows, xs, wg, wu, wd)


CTX_COMBINE_TILES = T_CTX // COMBINE_TM


def _combine_kernel(dest_ref, x_ref, y_hbm, gate_ref, g2_ref, ng_ref, sc_ref, sh_ref, *rest, final):
    *outs, ybuf, sem = rest
    j = pl.program_id(0)
    slot = j % 2

    def issue(tile, s):
        base = tile * (2 * COMBINE_TM)

        def body(g, carry):
            r0 = pl.multiple_of(g * ROW_UNROLL, ROW_UNROLL)
            for u in range(ROW_UNROLL):
                for k in range(2):
                    d = dest_ref[base + 2 * (r0 + u) + k]
                    rows = ybuf.at[s, pl.ds(k * COMBINE_TM + r0, ROW_UNROLL)]
                    pltpu.make_async_copy(y_hbm.at[pl.ds(d, 1)], rows.at[pl.ds(u, 1)], sem.at[s]).start(priority=k)
            return carry

        lax.fori_loop(0, COMBINE_TM // ROW_UNROLL, body, 0)

    @pl.when(j == 0)
    def _():
        issue(0, 0)

    @pl.when(j + 1 < pl.num_programs(0))
    def _():
        issue(j + 1, 1 - slot)

    _wait_rows(y_hbm, ybuf.at[slot], sem.at[slot], 2 * COMBINE_TM)
    gates = gate_ref[...]
    moe = gates[:, 0:1] * ybuf[slot, 0:COMBINE_TM] + gates[:, 1:2] * ybuf[slot, COMBINE_TM:2 * COMBINE_TM]
    x = x_ref[...] + g2_ref[0, 0, 0] * moe
    if final:
        yp_ref, ys_ref = outs
        y = _rms_mod(x, ng_ref[...], 0.0, 0.0)

        @pl.when(j < CTX_COMBINE_TILES)
        def _():
            yp_ref[...] = y

        @pl.when(j >= CTX_COMBINE_TILES)
        def _():
            ys_ref[...] = y
    else:
        xo_ref, hno_ref = outs
        xo_ref[...] = x
        hno_ref[...] = _rms_mod(x, ng_ref[...], sc_ref[0, 0, 0], sh_ref[0, 0, 0]).astype(hno_ref.dtype)


def _moe_combine(x, y, dest_flat, gates, mods, layer, next_g, final):
    row = lambda n=D: pl.BlockSpec((COMBINE_TM, n), lambda j, dest: (j, 0))
    vec = lambda: pl.BlockSpec((1, D), lambda j, dest: (0, 0))
    mod = lambda l, k: pl.BlockSpec((1, 1, 1, 1, D), lambda j, dest: (l, k, _cond_of_tile(j, COMBINE_TM), 0, 0))
    nl = layer if final else layer + 1
    if final:
        out_specs = [pl.BlockSpec((COMBINE_TM, D), lambda j, dest: (jnp.minimum(j, CTX_COMBINE_TILES - 1), 0)),
                     pl.BlockSpec((COMBINE_TM, D), lambda j, dest: (jnp.maximum(j - CTX_COMBINE_TILES, 0), 0))]
        out_shape = [jax.ShapeDtypeStruct((T_CTX, D), F32), jax.ShapeDtypeStruct((T_LAT, D), F32)]
    else:
        out_specs = [row(), row()]
        out_shape = [jax.ShapeDtypeStruct((T, D), F32), jax.ShapeDtypeStruct((T, D), BF16)]
    return pl.pallas_call(
        functools.partial(_combine_kernel, final=final),
        grid_spec=pltpu.PrefetchScalarGridSpec(
            num_scalar_prefetch=1,
            grid=(T // COMBINE_TM,),
            in_specs=[row(), pl.BlockSpec(memory_space=pl.ANY), row(2), mod(layer, 5), vec(), mod(nl, 1), mod(nl, 0)],
            out_specs=out_specs,
            scratch_shapes=[pltpu.VMEM((2, 2 * COMBINE_TM, D), F32), pltpu.SemaphoreType.DMA((2,))],
        ),
        out_shape=out_shape,
        compiler_params=_params(("arbitrary",)),
        name="moe_combine",
    )(dest_flat, x, y, gates, mods, next_g.reshape(1, D), mods, mods)


LRU_CHUNK = 256


def _shift_rows(v, s, row):
    n = v.shape[0]
    rolled = pltpu.roll(v, s % n, axis=0)
    keep = (row >= s) if s > 0 else (row < n + s)
    return jnp.where(keep, rolled, 0.0)


W_IN_CHUNK = 512
N_W_IN_CHUNKS = IN_W // W_IN_CHUNK


def _lru_kernel(hn_ref, h0_ref, win_hbm, cw_ref, cb_ref, wbd_ref, bbd_ref, lam_ref, scw_ref, scb_ref,
                y_ref, st_ref, winb_ref, stage_ref, stage_sem, proj_ref, xc_ref, a_ref, b_ref, *, seq_len, idx):
    @pl.when(pl.program_id(0) == 0)
    def _():
        copies = [pltpu.make_async_copy(win_hbm.at[idx, :, pl.ds(j * W_IN_CHUNK, W_IN_CHUNK)],
                                        stage_ref.at[j % 2], stage_sem.at[j % 2]) for j in range(N_W_IN_CHUNKS)]
        copies[0].start()
        for j in range(N_W_IN_CHUNKS):
            if j + 1 < N_W_IN_CHUNKS:
                copies[j + 1].start()
            copies[j].wait()
            winb_ref[:, j * W_IN_CHUNK:(j + 1) * W_IN_CHUNK] = stage_ref[j % 2].astype(BF16)

    proj_ref[...] = jnp.dot(hn_ref[...], winb_ref[...], preferred_element_type=F32)
    row = lax.broadcasted_iota(jnp.int32, (seq_len, 1), 0)
    x = proj_ref[:, 0:LRU_W]
    xc =(cb_ref[...] + _shift_rows(x, 2, row) * cw_ref[0:1] + _shift_rows(x, 1, row) * cw_ref[1:2]
          + x * cw_ref[2:3] + _shift_rows(x, -1, row) * cw_ref[3:4])
    xc_ref[...] = xc

    lam = lam_ref[...]
    neg_csp = -LRU_C * (jnp.maximum(-lam, 0.0) + jnp.log1p(jnp.exp(-jnp.abs(lam))))

    for c in range(seq_len // LRU_CHUNK):
        rows = pl.ds(c * LRU_CHUNK, LRU_CHUNK)
        xcc = xc_ref[rows, :]
        gates = jnp.dot(xcc.astype(BF16), wbd_ref[...], preferred_element_type=F32) + bbd_ref[...]
        for d in range(2):
            r = jax.nn.sigmoid(gates[:, (2 * d) * LRU_W:(2 * d + 1) * LRU_W])
            i = jax.nn.sigmoid(gates[:, (2 * d + 1) * LRU_W:(2 * d + 2) * LRU_W])
            log_a = neg_csp[d:d + 1] * r
            t = jnp.tanh(log_a)
            one_minus_a2 = -2.0 * t / (1.0 - t)
            a_ref[d, rows, :] = jnp.exp(log_a)
            b_ref[d, rows, :] = jnp.sqrt(one_minus_a2) * (i * xcc)

    n_grp = seq_len // 8

    def scan_body(g, carry):
        hf, hb = carry
        rf = pl.ds(pl.multiple_of(g * 8, 8), 8)
        rb = pl.ds(pl.multiple_of((n_grp - 1 - g) * 8, 8), 8)
        af, bf = a_ref[0, rf, :], b_ref[0, rf, :]
        ab, bb = a_ref[1, rb, :], b_ref[1, rb, :]
        outs_f, outs_b = [], []
        for j in range(8):
            hf = af[j:j + 1] * hf + bf[j:j + 1]
            outs_f.append(hf)
            hb = ab[7 - j:8 - j] * hb + bb[7 - j:8 - j]
            outs_b.append(hb)
        b_ref[0, rf, :] = jnp.concatenate(outs_f, axis=0)
        b_ref[1, rb, :] = jnp.concatenate(outs_b[::-1], axis=0)
        return hf, hb

    hf, hb = lax.fori_loop(0, n_grp, scan_body, (h0_ref[0, 0:1, :], h0_ref[0, 1:2, :]))
    st_ref[0, 0:1, :] = hf
    st_ref[0, 1:2, :] = hb

    h_sum = b_ref[0] + b_ref[1]
    y_ref[:, 0:LRU_W] = (h_sum * jax.nn.gelu(proj_ref[:, LRU_W:2 * LRU_W])).astype(y_ref.dtype)
    o = 2 * LRU_W
    cv = proj_ref[:, o + SC_W:o + 2 * SC_W] * proj_ref[:, o + 2 * SC_W:o + 3 * SC_W]
    conv = scb_ref[...] + _shift_rows(cv, 1, row) * scw_ref[0:1] + cv * scw_ref[1:2] + _shift_rows(cv, -1, row) * scw_ref[2:3]
    y_ref[:, LRU_W:LRU_W + SC_W] = (proj_ref[:, o:o + SC_W] * conv).astype(y_ref.dtype)


def _lru_mixer(hn, h0, w_in, idx, cw, cb, wbd, bbd, lam, scw, scb, *, latent):
    if latent:
        n_seq, seq_len, blk0 = DEC_BATCH, DEC_SEQ, T_CTX // DEC_SEQ
    else:
        n_seq, seq_len, blk0 = BATCH, SEQ, 0
    full = lambda shape: pl.BlockSpec(shape, lambda b: (0,) * len(shape))
    return pl.pallas_call(
        functools.partial(_lru_kernel, seq_len=seq_len, idx=idx),
        grid=(n_seq,),
        in_specs=[pl.BlockSpec((seq_len, D), lambda b: (blk0 + b, 0)),
                  pl.BlockSpec((1, 2, LRU_W), lambda b: (b, 0, 0)),
                  pl.BlockSpec(memory_space=pl.ANY),
                  full((4, LRU_W)), full((1, LRU_W)), full((LRU_W, 4 * LRU_W)), full((1, 4 * LRU_W)),
                  full((2, LRU_W)), full((3, SC_W)), full((1, SC_W))],
        out_specs=[pl.BlockSpec((seq_len, D), lambda b: (b, 0)),
                   pl.BlockSpec((1, 2, LRU_W), lambda b: (b, 0, 0))],
        out_shape=[jax.ShapeDtypeStruct((n_seq * seq_len, D), BF16),
                   jax.ShapeDtypeStruct((n_seq, 2, LRU_W), F32)],
        scratch_shapes=[pltpu.VMEM((D, IN_W), BF16),
                        pltpu.VMEM((2, D, W_IN_CHUNK), F32),
                        pltpu.SemaphoreType.DMA((2,)),
                        pltpu.VMEM((seq_len, IN_W), F32),
                        pltpu.VMEM((seq_len, LRU_W), F32),
                        pltpu.VMEM((2, seq_len, LRU_W), F32),
                        pltpu.VMEM((2, seq_len, LRU_W), F32)],
        compiler_params=_params(("arbitrary",)),
        name="lru_mixer",
    )(hn, h0, w_in, cw, cb.reshape(1, LRU_W), wbd, bbd, lam, scw, scb.reshape(1, SC_W))


def _block_diag_dense(w):
    eye = jnp.eye(LRU_HEADS, dtype=w.dtype)
    return jnp.einsum('hij,hg->higj', w, eye).reshape(LRU_W, LRU_W)


def _route(idx):
    e_flat = idx.reshape(-1)
    onehot = (e_flat[:, None] == jnp.arange(N_EXPERTS)[None, :]).astype(jnp.int32)
    csum = jnp.cumsum(onehot, axis=0)
    counts = csum[-1]
    rank = jnp.take_along_axis(csum, e_flat[:, None], axis=1)[:, 0] - 1
    padded = (counts + MOE_TM - 1) // MOE_TM * MOE_TM
    ends = jnp.cumsum(padded)
    dest = ((ends - padded)[e_flat] + rank).astype(jnp.int32)
    tile_end = ends // MOE_TM
    n_used = tile_end[-1]
    tiles = jnp.minimum(jnp.arange(MOE_TILES), n_used - 1)
    tile_expert = jnp.sum(tiles[:, None] >= tile_end[None, :], axis=1).astype(jnp.int32)
    group_tile = jnp.arange(MOE_TILES) - ((ends - padded) // MOE_TM)[tile_expert]
    tile_rows = jnp.clip(counts[tile_expert] - group_tile * MOE_TM, 0, MOE_TM).astype(jnp.int32)
    return dest, tile_expert, n_used.reshape(1).astype(jnp.int32), tile_rows


def kernel(x_prompt, x_sample, state_lru, cache_k, cache_v, c, c_ctx, norm1_g, norm2_g, w_mod, b_mod, w_in, lru_conv_w, lru_conv_b, lru_wa, lru_ba, lru_wx, lru_bx, lru_lam, sc_conv_w, sc_conv_b, w_out, ffn_w_gate, ffn_w_up, ffn_w_down, w_qkv, w_o, rpb, w_router, moe_w_gate, moe_w_up, moe_w_down, final_g):
    cond = jnp.concatenate([c_ctx[None, :], c, jnp.zeros((COND_ROWS - N_COND, D), F32)], axis=0)
    mods = _modulation(cond, w_mod, b_mod).reshape(DEPTH, 6, COND_ROWS, 1, D)

    x, hn = _first_norm(x_prompt.reshape(T_CTX, D), x_sample.reshape(T_LAT, D), norm1_g[0], mods, 0)
    cache_kt = jnp.swapaxes(cache_k, -1, -2)
    cache_vt = jnp.swapaxes(cache_v, -1, -2)
    zero_state = jnp.zeros((BATCH, 2, LRU_W), F32)
    states = []
    new_k = jnp.zeros((BATCH, DEPTH // 2, N_HEADS, HEAD_DIM, SEQ), F32)
    new_v = jnp.zeros((BATCH, DEPTH // 2, N_HEADS, HEAD_DIM, SEQ), F32)
    for l in range(DEPTH):
        idx = l // 2
        if l % 2 == 0:
            wbd = jnp.concatenate([_block_diag_dense(lru_wa[idx, 0]), _block_diag_dense(lru_wx[idx, 0]),
                                   _block_diag_dense(lru_wa[idx, 1]), _block_diag_dense(lru_wx[idx, 1])],
                                  axis=1).astype(BF16)
            bbd = jnp.concatenate([lru_ba[idx, 0], lru_bx[idx, 0], lru_ba[idx, 1], lru_bx[idx, 1]]).reshape(1, 4 * LRU_W)
            lru_args = (lru_conv_w[idx], lru_conv_b[idx], wbd, bbd, lru_lam[idx], sc_conv_w[idx], sc_conv_b[idx])
            y_ctx, st = _lru_mixer(hn, zero_state, w_in, idx, *lru_args, latent=False)
            y_lat, _ = _lru_mixer(hn, state_lru[:, idx], w_in, idx, *lru_args, latent=True)
            states.append(st)
            x, hn2 = _proj_res_norm(y_ctx, y_lat, w_out, idx, x, norm2_g[l], mods, l)
            x, hn = _ffn(hn2, ffn_w_gate, ffn_w_up, ffn_w_down, idx, x, mods, l, norm1_g[l + 1])
        else:
            q_ctx, new_k, new_v = _qkv_heads(hn, w_qkv, idx, latent=False, k_into=new_k, v_into=new_v)
            q_lat, k_lat, v_lat = _qkv_heads(hn, w_qkv, idx, latent=True)
            o_ctx = _context_attention(q_ctx, new_k, new_v, idx)
            o_lat = _neighbourhood_attention(q_lat, k_lat, v_lat, cache_kt, cache_vt, _window_bias(rpb[idx]), idx)
            x, hn2, top_i, top_p = _proj_res_norm(o_ctx, o_lat, w_o, idx, x, norm2_g[l], mods, l, w_router[idx])
            dest, tile_expert, n_used, tile_rows = _route(top_i)
            xs = _moe_scatter(hn2, dest)
            y = _moe_experts(xs, tile_expert, n_used, tile_rows, moe_w_gate, moe_w_up, moe_w_down, idx)
            if l == DEPTH - 1:
                y_prompt, y_sample = _moe_combine(x, y, dest, top_p, mods, l, final_g, True)
            else:
                x, hn = _moe_combine(x, y, dest, top_p, mods, l, norm1_g[l + 1], False)

    return (y_prompt.reshape(BATCH, SEQ, D), y_sample.reshape(DEC_BATCH, DEC_SEQ, D), jnp.stack(states, axis=1),
            jnp.swapaxes(new_k, -1, -2), jnp.swapaxes(new_v, -1, -2))
```

```python
import functools

import jax
import jax.numpy as jnp
import numpy as np
from jax import lax
from jax.experimental import pallas as pl
from jax.experimental.pallas import tpu as pltpu

F32 = jnp.float32
BF16 = jnp.bfloat16

D = 1024
BATCH = 32
SEQ = 256
DEPTH = 4
DEC_BATCH = 2
DEC_SEQ = 1024
PAST_LEN = 256
GRID_W = 64
GRID_ROWS = DEC_SEQ // GRID_W
LRU_W = 512
LRU_HEADS = 8
LRU_BLOCK = LRU_W // LRU_HEADS
LRU_C = 8.0
SC_W = 512
IN_W = 2 * LRU_W + 3 * SC_W
N_HEADS = 16
HEAD_DIM = D // N_HEADS
WIN_R = 8
WIN_C = 16
D_FF = 2816
N_EXPERTS = 8
EPS = 1e-6
NEG_INF = -1e30

T_CTX = BATCH * SEQ
T_LAT = DEC_BATCH * DEC_SEQ
T = T_CTX + T_LAT
N_COND = 3
COND_ROWS = 8

TM = 1024
TF = 256
N_FCH = D_FF // TF
MOE_TM = 1024
MOE_TILE_PARTS = 4
MOE_TILES = (2 * T + N_EXPERTS * (MOE_TM - 1)) // MOE_TM
MOE_ROWS = MOE_TILES * MOE_TM
VMEM_LIMIT = 52 * 1024 * 1024


def _params(sem):
    return pltpu.CompilerParams(dimension_semantics=sem, vmem_limit_bytes=VMEM_LIMIT)


def _cond_of_tile(i, tm=TM):
    r = i * tm
    return jnp.where(r < T_CTX, 0, 1 + (r - T_CTX) // DEC_SEQ)


def _mod_spec(layer, k, grid_rank=1, tm=TM):
    if grid_rank == 1:
        return pl.BlockSpec((1, 1, 1, 1, D), lambda i: (layer, k, _cond_of_tile(i, tm), 0, 0))
    return pl.BlockSpec((1, 1, 1, 1, D), lambda i, f, *_: (layer, k, _cond_of_tile(i, tm), 0, 0))


def _rms_mod(x, g, scale, shift):
    y = x * lax.rsqrt(jnp.mean(x * x, axis=-1, keepdims=True) + EPS)
    return (y * g) * (1.0 + scale) + shift


def _mod_kernel(c_ref, w_ref, b_ref, o_ref):
    c = c_ref[...]
    s = c * jax.nn.sigmoid(c)
    o_ref[0, 0] = jnp.dot(s.astype(BF16), w_ref[0].astype(BF16), preferred_element_type=F32) + b_ref[0, 0]


def _modulation(cond, w_mod, b_mod):
    return pl.pallas_call(
        _mod_kernel,
        grid=(DEPTH, 6),
        in_specs=[
            pl.BlockSpec((COND_ROWS, D), lambda l, k: (0, 0)),
            pl.BlockSpec((1, D, D), lambda l, k: (l, 0, k)),
            pl.BlockSpec((1, 1, 1, D), lambda l, k: (l, k, 0, 0)),
        ],
        out_specs=pl.BlockSpec((1, 1, COND_ROWS, D), lambda l, k: (l, k, 0, 0)),
        out_shape=jax.ShapeDtypeStruct((DEPTH, 6, COND_ROWS, D), F32),
        compiler_params=_params(("arbitrary", "arbitrary")),
        name="adaln_mod",
    )(cond, w_mod, b_mod.reshape(DEPTH, 6, 1, D))


def _norm_kernel(xc_ref, xl_ref, g_ref, sc_ref, sh_ref, x_ref, o_ref):
    x = jnp.where(pl.program_id(0) < T_CTX // TM, xc_ref[...], xl_ref[...])
    x_ref[...] = x
    o_ref[...] = _rms_mod(x, g_ref[...], sc_ref[0, 0, 0], sh_ref[0, 0, 0]).astype(o_ref.dtype)


def _first_norm(x_ctx, x_lat, g, mods, layer):
    n_ctx = T_CTX // TM
    return pl.pallas_call(
        _norm_kernel,
        grid=(T // TM,),
        in_specs=[
            pl.BlockSpec((TM, D), lambda i: (jnp.minimum(i, n_ctx - 1), 0)),
            pl.BlockSpec((TM, D), lambda i: (jnp.maximum(i - n_ctx, 0), 0)),
            pl.BlockSpec((1, D), lambda i: (0, 0)),
            _mod_spec(layer, 1),
            _mod_spec(layer, 0),
        ],
        out_specs=[pl.BlockSpec((TM, D), lambda i: (i, 0)), pl.BlockSpec((TM, D), lambda i: (i, 0))],
        out_shape=[jax.ShapeDtypeStruct((T, D), F32), jax.ShapeDtypeStruct((T, D), BF16)],
        compiler_params=_params(("arbitrary",)),
        name="first_norm",
    )(x_ctx, x_lat, g.reshape(1, D), mods, mods)


QKV_TM = 512


def _qkv_kernel(a_ref, wq_ref, wk_ref, wv_ref, *rest, n_piece, piece_len, kv_transposed, aliased):
    if aliased:
        rest = rest[2:]
    q_ref, k_ref, v_ref, wb_ref, acc_ref = rest

    @pl.when(pl.program_id(0) == 0)
    def _():
        for part, w_ref in enumerate((wq_ref, wk_ref, wv_ref)):
            wb_ref[part] = w_ref[0].astype(BF16)

    a = a_ref[...]
    q = jnp.dot(a, wb_ref[0], preferred_element_type=F32) * (HEAD_DIM ** -0.5)
    for s in range(n_piece):
        for h in range(N_HEADS):
            q_ref[s, 0, h] = q[s * piece_len:(s + 1) * piece_len, h * HEAD_DIM:(h + 1) * HEAD_DIM].astype(q_ref.dtype)
    for part, o_ref in ((1, k_ref), (2, v_ref)):
        acc = jnp.dot(a, wb_ref[part], preferred_element_type=F32)
        if kv_transposed:
            acc_ref[...] = acc
            acc_t = acc_ref[...].T
            for s in range(n_piece):
                for h in range(N_HEADS):
                    o_ref[s, 0, h] = acc_t[h * HEAD_DIM:(h + 1) * HEAD_DIM, s * piece_len:(s + 1) * piece_len]
        else:
            for s in range(n_piece):
                for h in range(N_HEADS):
                    o_ref[s, 0, h] = acc[s * piece_len:(s + 1) * piece_len, h * HEAD_DIM:(h + 1) * HEAD_DIM]


def _qkv_heads(hn, w_qkv, idx, *, latent, k_into=None, v_into=None):
    if latent:
        n_batch, seq_len, row0, n_slots, slot = DEC_BATCH, DEC_SEQ, T_CTX // QKV_TM, 1, 0
    else:
        n_batch, seq_len, row0, n_slots, slot = BATCH, SEQ, 0, DEPTH // 2, idx
    piece_len = min(seq_len, QKV_TM)
    n_piece = QKV_TM // piece_len
    per_seq = seq_len // piece_len
    aliased = k_into is not None

    def head_spec(tail_t, s):
        if tail_t:
            shape = (n_piece, 1, N_HEADS, HEAD_DIM, piece_len)
            return pl.BlockSpec(shape, lambda i: (i, s, 0, 0, 0))
        shape = (n_piece, 1, N_HEADS, piece_len, HEAD_DIM)
        return pl.BlockSpec(shape, lambda i: (i // per_seq, s, 0, i % per_seq, 0))

    wspec = lambda part: pl.BlockSpec((1, D, D), lambda i: (idx, 0, part), pipeline_mode=pl.Buffered(1))
    in_specs = [pl.BlockSpec((QKV_TM, D), lambda i: (row0 + i, 0)), wspec(0), wspec(1), wspec(2)]
    args = [hn, w_qkv, w_qkv, w_qkv]
    aliases = {}
    kv_tail = (HEAD_DIM, seq_len) if aliased else (seq_len, HEAD_DIM)
    if aliased:
        in_specs += [pl.BlockSpec(memory_space=pl.ANY), pl.BlockSpec(memory_space=pl.ANY)]
        args += [k_into, v_into]
        aliases = {4: 1, 5: 2}
    kv_shape = jax.ShapeDtypeStruct((n_batch, n_slots, N_HEADS) + kv_tail, F32)
    return pl.pallas_call(
        functools.partial(_qkv_kernel, n_piece=n_piece, piece_len=piece_len, kv_transposed=aliased, aliased=aliased),
        grid=(n_batch * seq_len // QKV_TM,),
        in_specs=in_specs,
        out_specs=[head_spec(False, 0), head_spec(aliased, slot), head_spec(aliased, slot)],
        out_shape=[jax.ShapeDtypeStruct((n_batch, 1, N_HEADS, seq_len, HEAD_DIM), BF16), kv_shape, kv_shape],
        scratch_shapes=[pltpu.VMEM((3, D, D), BF16), pltpu.VMEM((QKV_TM, D), F32)],
        input_output_aliases=aliases,
        compiler_params=_params(("arbitrary",)),
        name="qkv_heads",
    )(*args)


def _dot_nt(a, b):
    return lax.dot_general(a, b, (((1,), (1,)), ((), ())), preferred_element_type=F32)


def _softmax_pv(s_list, v_list, v_transposed=False):
    m = s_list[0].max(axis=-1, keepdims=True)
    for s in s_list[1:]:
        m = jnp.maximum(m, s.max(axis=-1, keepdims=True))
    den = None
    out = None
    for s, v in zip(s_list, v_list):
        p = jnp.exp(s - m)
        d = p.sum(axis=-1, keepdims=True)
        pb = p.astype(BF16)
        o = _dot_nt(pb, v) if v_transposed else jnp.dot(pb, v, preferred_element_type=F32)
        den = d if den is None else den + d
        out = o if out is None else out + o
    return out / den


HEAD_PAIR_W = 2 * HEAD_DIM


def _ctx_attn_kernel(q_ref, kt_ref, vt_ref, o_ref):
    for hp in range(N_HEADS // 2):
        pair = []
        for h in (2 * hp, 2 * hp + 1):
            s = jnp.dot(q_ref[0, 0, h], kt_ref[0, 0, h].astype(BF16), preferred_element_type=F32)
            pair.append(_softmax_pv([s], [vt_ref[0, 0, h].astype(BF16)], v_transposed=True))
        o_ref[:, hp * HEAD_PAIR_W:(hp + 1) * HEAD_PAIR_W] = jnp.concatenate(pair, axis=-1).astype(o_ref.dtype)


def _context_attention(q, kt, vt, slot):
    kv_spec = pl.BlockSpec((1, 1, N_HEADS, HEAD_DIM, SEQ), lambda b: (b, slot, 0, 0, 0))
    return pl.pallas_call(
        _ctx_attn_kernel,
        grid=(BATCH,),
        in_specs=[pl.BlockSpec((1, 1, N_HEADS, SEQ, HEAD_DIM), lambda b: (b, 0, 0, 0, 0)), kv_spec, kv_spec],
        out_specs=pl.BlockSpec((SEQ, D), lambda b: (b, 0)),
        out_shape=jax.ShapeDtypeStruct((T_CTX, D), BF16),
        compiler_params=_params(("arbitrary",)),
        name="ctx_attention",
    )(q, kt, vt)


def _row_start(r):
    return min(max(r - WIN_R // 2, 0), GRID_ROWS - WIN_R)


def _nbr_attn_kernel(q_ref, k_ref, v_ref, kct_ref, vct_ref, bias_ref, o_ref):
    kct = [kct_ref[0, 0, hh].astype(BF16) for hh in range(2)]
    vct = [vct_ref[0, 0, hh].astype(BF16) for hh in range(2)]
    for r0, r1 in _ROW_GROUPS:
        rs = _row_start(r0)
        n_q = (r1 - r0) * GRID_W
        pair = []
        for hh in range(2):
            q = q_ref[0, 0, hh, r0 * GRID_W:r1 * GRID_W, :]
            kw = k_ref[0, 0, hh, rs * GRID_W:(rs + WIN_R) * GRID_W, :].astype(BF16)
            vw = v_ref[0, 0, hh, rs * GRID_W:(rs + WIN_R) * GRID_W, :].astype(BF16)
            bias = bias_ref[hh, r0 - rs:r1 - rs].reshape(n_q, WIN_R * GRID_W)
            s_win = _dot_nt(q, kw) + bias
            s_ctx = jnp.dot(q, kct[hh], preferred_element_type=F32)
            m = jnp.maximum(s_win.max(axis=-1, keepdims=True), s_ctx.max(axis=-1, keepdims=True))
            p_win = jnp.exp(s_win - m)
            p_ctx = jnp.exp(s_ctx - m)
            den = p_win.sum(axis=-1, keepdims=True) + p_ctx.sum(axis=-1, keepdims=True)
            out = (jnp.dot(p_win.astype(BF16), vw, preferred_element_type=F32)
                   + _dot_nt(p_ctx.astype(BF16), vct[hh]))
            pair.append(out / den)
        o_ref[r0 * GRID_W:r1 * GRID_W, :] = jnp.concatenate(pair, axis=-1).astype(o_ref.dtype)


def _row_groups():
    groups, r0 = [], 0
    for r in range(1, GRID_ROWS + 1):
        if r == GRID_ROWS or _row_start(r) != _row_start(r0):
            groups.append((r0, r))
            r0 = r
    return groups


_ROW_GROUPS = _row_groups()


def _window_bias(rpb_l):
    col = np.arange(GRID_W)
    col_start = np.clip(col - WIN_C // 2, 0, GRID_W - WIN_C)
    col_mask = (col[None, :] >= col_start[:, None]) & (col[None, :] < col_start[:, None] + WIN_C)
    dc = np.clip(col[None, :] - col[:, None], -(WIN_C - 1), WIN_C - 1) + (WIN_C - 1)
    onehot = (dc[None, :, :] == np.arange(2 * WIN_C - 1)[:, None, None]).astype(np.float32)
    rows = jnp.stack([rpb_l[:, WIN_R - 1 - v:2 * WIN_R - 1 - v, :] for v in range(WIN_R)], axis=1)
    bias = jnp.einsum('hvkd,dqc->hvqkc', rows, jnp.asarray(onehot), precision=lax.Precision.HIGHEST)
    bias = jnp.where(jnp.asarray(col_mask)[None, None, :, None, :], bias, NEG_INF)
    return bias.reshape(N_HEADS, WIN_R, GRID_W, WIN_R * GRID_W)


def _neighbourhood_attention(q, k, v, cache_kt, cache_vt, bias, slot):
    lat = lambda: pl.BlockSpec((1, 1, 2, DEC_SEQ, HEAD_DIM), lambda b, hp: (b, 0, hp, 0, 0))
    ctx = lambda: pl.BlockSpec((1, 1, 2, HEAD_DIM, PAST_LEN), lambda b, hp: (b, slot, hp, 0, 0))
    return pl.pallas_call(
        _nbr_attn_kernel,
        grid=(DEC_BATCH, N_HEADS // 2),
        in_specs=[lat(), lat(), lat(), ctx(), ctx(),
                  pl.BlockSpec((2, WIN_R, GRID_W, WIN_R * GRID_W), lambda b, hp: (hp, 0, 0, 0))],
        out_specs=pl.BlockSpec((DEC_SEQ, HEAD_PAIR_W), lambda b, hp: (b, hp)),
        out_shape=jax.ShapeDtypeStruct((T_LAT, D), BF16),
        compiler_params=_params(("arbitrary", "arbitrary")),
        name="nbr_attention",
    )(q, k, v, cache_kt, cache_vt, bias)


def _top2(logits):
    col = lax.broadcasted_iota(jnp.int32, logits.shape, 1)
    m1 = jnp.max(logits, axis=-1, keepdims=True)
    i1 = jnp.min(jnp.where(logits == m1, col, N_EXPERTS), axis=-1, keepdims=True)
    rest = jnp.where(col == i1, -jnp.inf, logits)
    m2 = jnp.max(rest, axis=-1, keepdims=True)
    i2 = jnp.min(jnp.where(rest == m2, col, N_EXPERTS), axis=-1, keepdims=True)
    e = jnp.exp(m2 - m1)
    return i1, i2, 1.0 / (1.0 + e), e / (1.0 + e)


CTX_TILES = T_CTX // TM


def _proj_res_norm_kernel(ac_ref, al_ref, w_ref, x_ref, g1_ref, ng_ref, sc_ref, sh_ref, *rest, router):
    if router:
        wr_ref, xo_ref, hn_ref, idx_ref, gate_ref, wb_ref = rest
    else:
        xo_ref, hn_ref, wb_ref = rest

    @pl.when(pl.program_id(0) == 0)
    def _():
        wb_ref[...] = w_ref[0].astype(BF16)

    a = jnp.where(pl.program_id(0) < CTX_TILES, ac_ref[...], al_ref[...])
    mix = jnp.dot(a, wb_ref[...], preferred_element_type=F32)
    x = x_ref[...] + g1_ref[0, 0, 0] * mix
    xo_ref[...] = x
    hn = _rms_mod(x, ng_ref[...], sc_ref[0, 0, 0], sh_ref[0, 0, 0])
    hn_ref[...] = hn.astype(hn_ref.dtype)
    if router:
        hi = hn.astype(BF16)
        lo = (hn - hi.astype(F32)).astype(BF16)
        l_hi = jnp.dot(hi, wr_ref[...], preferred_element_type=F32)
        l_lo = jnp.dot(lo, wr_ref[...], preferred_element_type=F32)
        logits = l_hi[:, :N_EXPERTS] + l_hi[:, N_EXPERTS:] + l_lo[:, :N_EXPERTS]
        i1, i2, p1, p2 = _top2(logits)
        first = lax.broadcasted_iota(jnp.int32, (logits.shape[0], 2), 1) == 0
        idx_ref[...] = jnp.where(first, i1, i2)
        gate_ref[...] = jnp.where(first, p1, p2)


def _proj_res_norm(a_ctx, a_lat, w, idx, x, norm_g, mods, layer, w_router=None):
    router = w_router is not None
    row = lambda n: pl.BlockSpec((TM, n), lambda i: (i, 0))
    in_specs = [pl.BlockSpec((TM, D), lambda i: (jnp.minimum(i, CTX_TILES - 1), 0)),
                pl.BlockSpec((TM, D), lambda i: (jnp.maximum(i - CTX_TILES, 0), 0)),
                pl.BlockSpec((1, D, D), lambda i: (idx, 0, 0)), row(D),
                _mod_spec(layer, 2), pl.BlockSpec((1, D), lambda i: (0, 0)), _mod_spec(layer, 4), _mod_spec(layer, 3)]
    args = [a_ctx, a_lat, w, x, mods, norm_g.reshape(1, D), mods, mods]
    out_specs = [row(D), row(D)]
    out_shape = [jax.ShapeDtypeStruct((T, D), F32), jax.ShapeDtypeStruct((T, D), F32 if router else BF16)]
    if router:
        w_hi = w_router.astype(BF16)
        w_lo = (w_router - w_hi.astype(F32)).astype(BF16)
        in_specs.append(pl.BlockSpec((D, 2 * N_EXPERTS), lambda i: (0, 0)))
        args.append(jnp.concatenate([w_hi, w_lo], axis=1))
        out_specs += [row(2), row(2)]
        out_shape += [jax.ShapeDtypeStruct((T, 2), jnp.int32), jax.ShapeDtypeStruct((T, 2), F32)]
    return pl.pallas_call(
        functools.partial(_proj_res_norm_kernel, router=router),
        grid=(T // TM,),
        in_specs=in_specs,
        out_specs=out_specs,
        out_shape=out_shape,
        scratch_shapes=[pltpu.VMEM((D, D), BF16)],
        compiler_params=_params(("arbitrary",)),
        name="proj_res_norm",
    )(*args)


def _swiglu_chunk(hn, wg_ref, wu_ref, wd_ref, acc_ref):
    g = jnp.dot(hn, wg_ref[0].astype(BF16), preferred_element_type=F32)
    u = jnp.dot(hn, wu_ref[0].astype(BF16), preferred_element_type=F32)
    h = (g * jax.nn.sigmoid(g) * u).astype(BF16)
    acc_ref[...] += jnp.dot(h, wd_ref[0].astype(BF16), preferred_element_type=F32)


def _ffn_kernel(hn_ref, wg_ref, wu_ref, wd_ref, x_ref, g2_ref, ng_ref, sc_ref, sh_ref, xo_ref, hno_ref, acc_ref):
    @pl.when(pl.program_id(1) == 0)
    def _():
        acc_ref[...] = jnp.zeros_like(acc_ref)

    _swiglu_chunk(hn_ref[...], wg_ref, wu_ref, wd_ref, acc_ref)

    @pl.when(pl.program_id(1) == N_FCH - 1)
    def _():
        x = x_ref[...] + g2_ref[0, 0, 0] * acc_ref[...]
        xo_ref[...] = x
        hno_ref[...] = _rms_mod(x, ng_ref[...], sc_ref[0, 0, 0], sh_ref[0, 0, 0]).astype(hno_ref.dtype)


def _ffn(hn, wg, wu, wd, idx, x, mods, layer, next_g):
    row = lambda: pl.BlockSpec((TM, D), lambda i, f: (i, 0))
    return pl.pallas_call(
        _ffn_kernel,
        grid=(T // TM, N_FCH),
        in_specs=[row(),
                  pl.BlockSpec((1, D, TF), lambda i, f: (idx, 0, f)),
                  pl.BlockSpec((1, D, TF), lambda i, f: (idx, 0, f)),
                  pl.BlockSpec((1, TF, D), lambda i, f: (idx, f, 0)),
                  row(), _mod_spec(layer, 5, 2), pl.BlockSpec((1, D), lambda i, f: (0, 0)),
                  _mod_spec(layer + 1, 1, 2), _mod_spec(layer + 1, 0, 2)],
        out_specs=[row(), row()],
        out_shape=[jax.ShapeDtypeStruct((T, D), F32), jax.ShapeDtypeStruct((T, D), BF16)],
        scratch_shapes=[pltpu.VMEM((TM, D), F32)],
        compiler_params=_params(("arbitrary", "arbitrary")),
        name="ffn",
    )(hn, wg, wu, wd, x, mods, next_g.reshape(1, D), mods, mods)


ROW_UNROLL = 8
SCATTER_TM = 1024
COMBINE_TM = 1024


def _wait_rows(src_ref, dst_ref, sem, n_rows):
    pltpu.make_async_copy(src_ref.at[pl.ds(0, n_rows)], dst_ref.at[pl.ds(0, n_rows)], sem).wait()


def _scatter_kernel(dest_ref, hn_ref, xs_in_ref, xs_ref, sem):
    del xs_in_ref
    base = pl.program_id(0) * (2 * SCATTER_TM)

    def body(g, carry):
        r0 = pl.multiple_of(g * ROW_UNROLL, ROW_UNROLL)
        for u in range(ROW_UNROLL):
            for k in range(2):
                d = dest_ref[base + 2 * (r0 + u) + k]
                pltpu.make_async_copy(hn_ref.at[pl.ds(r0, ROW_UNROLL)].at[pl.ds(u, 1)], xs_ref.at[pl.ds(d, 1)],
                                      sem).start(priority=k)
        return carry

    lax.fori_loop(0, SCATTER_TM // ROW_UNROLL, body, 0)
    for _ in range(2):
        _wait_rows(hn_ref, xs_ref, sem, SCATTER_TM)


def _moe_scatter(hn2, dest_flat):
    return pl.pallas_call(
        _scatter_kernel,
        grid_spec=pltpu.PrefetchScalarGridSpec(
            num_scalar_prefetch=1,
            grid=(T // SCATTER_TM,),
            in_specs=[pl.BlockSpec((SCATTER_TM, D), lambda j, dest: (j, 0)),
                      pl.BlockSpec(memory_space=pl.ANY)],
            out_specs=pl.BlockSpec(memory_space=pl.ANY),
            scratch_shapes=[pltpu.SemaphoreType.DMA],
        ),
        out_shape=jax.ShapeDtypeStruct((MOE_ROWS, D), F32),
        input_output_aliases={2: 0},
        compiler_params=_params(("arbitrary",)),
        name="moe_scatter",
    )(dest_flat, hn2, jnp.zeros((MOE_ROWS, D), F32))


def _moe_kernel(te_ref, nu_ref, rows_ref, xs_ref, wg_ref, wu_ref, wd_ref, y_ref, xsb_ref, acc_ref):
    i = pl.program_id(0)
    f = pl.program_id(1)
    part = MOE_TM // MOE_TILE_PARTS

    @pl.when(i < nu_ref[0])
    def _():
        @pl.when(f == 0)
        def _():
            xsb_ref[...] = xs_ref[...].astype(BF16)
            acc_ref[...] = jnp.zeros_like(acc_ref)

        n_parts = (rows_ref[i] + part - 1) // part
        for m in range(1, MOE_TILE_PARTS + 1):
            @pl.when(n_parts == m)
            def _(m=m):
                _swiglu_chunk(xsb_ref[0:m * part], wg_ref, wu_ref, wd_ref, acc_ref.at[0:m * part])

        @pl.when(f == N_FCH - 1)
        def _():
            y_ref[...] = acc_ref[...]

    @pl.when(jnp.logical_and(i >= nu_ref[0], f == N_FCH - 1))
    def _():
        y_ref[...] = jnp.zeros_like(y_ref)


def _moe_experts(xs, tile_expert, n_used, tile_rows, wg, wu, wd, idx):
    def wspec(shape, col):
        def imap(i, f, te, nu, rows):
            fe = jnp.where(i < nu[0], f, N_FCH - 1)
            return (idx, te[i], 0, fe) if col else (idx, te[i], fe, 0)
        return pl.BlockSpec(shape, imap)

    def kern(te, nu, rows, xs_ref, wg_ref, wu_ref, wd_ref, y_ref, xsb_ref, acc_ref):
        _moe_kernel(te, nu, rows, xs_ref, wg_ref.at[0], wu_ref.at[0], wd_ref.at[0], y_ref, xsb_ref, acc_ref)

    return pl.pallas_call(
        kern,
        grid_spec=pltpu.PrefetchScalarGridSpec(
            num_scalar_prefetch=3,
            grid=(MOE_TILES, N_FCH),
            in_specs=[pl.BlockSpec((MOE_TM, D), lambda i, f, te, nu, rows: (jnp.minimum(i, nu[0] - 1), 0)),
                      wspec((1, 1, D, TF), True), wspec((1, 1, D, TF), True), wspec((1, 1, TF, D), False)],
            out_specs=pl.BlockSpec((MOE_TM, D), lambda i, f, te, nu, rows: (i, 0)),
            scratch_shapes=[pltpu.VMEM((MOE_TM, D), BF16), pltpu.VMEM((MOE_TM, D), F32)],
        ),
        out_shape=jax.ShapeDtypeStruct((MOE_ROWS, D), F32),
        compiler_params=_params(("arbitrary", "arbitrary")),
        name="moe_experts",
    )(tile_expert, n_used, tile_rows, xs, wg, wu, wd)


CTX_COMBINE_TILES = T_CTX // COMBINE_TM


def _combine_kernel(dest_ref, x_ref, y_hbm, gate_ref, g2_ref, ng_ref, sc_ref, sh_ref, *rest, final):
    *outs, ybuf, sem = rest
    j = pl.program_id(0)
    slot = j % 2

    def issue(tile, s):
        base = tile * (2 * COMBINE_TM)

        def body(g, carry):
            r0 = pl.multiple_of(g * ROW_UNROLL, ROW_UNROLL)
            for u in range(ROW_UNROLL):
                for k in range(2):
                    d = dest_ref[base + 2 * (r0 + u) + k]
                    rows = ybuf.at[s, pl.ds(k * COMBINE_TM + r0, ROW_UNROLL)]
                    pltpu.make_async_copy(y_hbm.at[pl.ds(d, 1)], rows.at[pl.ds(u, 1)], sem.at[s]).start(priority=k)
            return carry

        lax.fori_loop(0, COMBINE_TM // ROW_UNROLL, body, 0)

    @pl.when(j == 0)
    def _():
        issue(0, 0)

    @pl.when(j + 1 < pl.num_programs(0))
    def _():
        issue(j + 1, 1 - slot)

    _wait_rows(y_hbm, ybuf.at[slot], sem.at[slot], 2 * COMBINE_TM)
    gates = gate_ref[...]
    moe = gates[:, 0:1] * ybuf[slot, 0:COMBINE_TM] + gates[:, 1:2] * ybuf[slot, COMBINE_TM:2 * COMBINE_TM]
    x = x_ref[...] + g2_ref[0, 0, 0] * moe
    if final:
        yp_ref, ys_ref = outs
        y = _rms_mod(x, ng_ref[...], 0.0, 0.0)

        @pl.when(j < CTX_COMBINE_TILES)
        def _():
            yp_ref[...] = y

        @pl.when(j >= CTX_COMBINE_TILES)
        def _():
            ys_ref[...] = y
    else:
        xo_ref, hno_ref = outs
        xo_ref[...] = x
        hno_ref[...] = _rms_mod(x, ng_ref[...], sc_ref[0, 0, 0], sh_ref[0, 0, 0]).astype(hno_ref.dtype)


def _moe_combine(x, y, dest_flat, gates, mods, layer, next_g, final):
    row = lambda n=D: pl.BlockSpec((COMBINE_TM, n), lambda j, dest: (j, 0))
    vec = lambda: pl.BlockSpec((1, D), lambda j, dest: (0, 0))
    mod = lambda l, k: pl.BlockSpec((1, 1, 1, 1, D), lambda j, dest: (l, k, _cond_of_tile(j, COMBINE_TM), 0, 0))
    nl = layer if final else layer + 1
    if final:
        out_specs = [pl.BlockSpec((COMBINE_TM, D), lambda j, dest: (jnp.minimum(j, CTX_COMBINE_TILES - 1), 0)),
                     pl.BlockSpec((COMBINE_TM, D), lambda j, dest: (jnp.maximum(j - CTX_COMBINE_TILES, 0), 0))]
        out_shape = [jax.ShapeDtypeStruct((T_CTX, D), F32), jax.ShapeDtypeStruct((T_LAT, D), F32)]
    else:
        out_specs = [row(), row()]
        out_shape = [jax.ShapeDtypeStruct((T, D), F32), jax.ShapeDtypeStruct((T, D), BF16)]
    return pl.pallas_call(
        functools.partial(_combine_kernel, final=final),
        grid_spec=pltpu.PrefetchScalarGridSpec(
            num_scalar_prefetch=1,
            grid=(T // COMBINE_TM,),
            in_specs=[row(), pl.BlockSpec(memory_space=pl.ANY), row(2), mod(layer, 5), vec(), mod(nl, 1), mod(nl, 0)],
            out_specs=out_specs,
            scratch_shapes=[pltpu.VMEM((2, 2 * COMBINE_TM, D), F32), pltpu.SemaphoreType.DMA((2,))],
        ),
        out_shape=out_shape,
        compiler_params=_params(("arbitrary",)),
        name="moe_combine",
    )(dest_flat, x, y, gates, mods, next_g.reshape(1, D), mods, mods)


LRU_CHUNK = 256


def _shift_rows(v, s, row):
    n = v.shape[0]
    rolled = pltpu.roll(v, s % n, axis=0)
    keep = (row >= s) if s > 0 else (row < n + s)
    return jnp.where(keep, rolled, 0.0)


W_IN_CHUNK = 512
N_W_IN_CHUNKS = IN_W // W_IN_CHUNK


def _lru_kernel(hn_ref, h0_ref, win_hbm, cw_ref, cb_ref, wbd_ref, bbd_ref, lam_ref, scw_ref, scb_ref,
                y_ref, st_ref, winb_ref, stage_ref, stage_sem, proj_ref, xc_ref, a_ref, b_ref, *, seq_len, idx):
    @pl.when(pl.program_id(0) == 0)
    def _():
        copies = [pltpu.make_async_copy(win_hbm.at[idx, :, pl.ds(j * W_IN_CHUNK, W_IN_CHUNK)],
                                        stage_ref.at[j % 2], stage_sem.at[j % 2]) for j in range(N_W_IN_CHUNKS)]
        copies[0].start()
        for j in range(N_W_IN_CHUNKS):
            if j + 1 < N_W_IN_CHUNKS:
                copies[j + 1].start()
            copies[j].wait()
            winb_ref[:, j * W_IN_CHUNK:(j + 1) * W_IN_CHUNK] = stage_ref[j % 2].astype(BF16)

    proj_ref[...] = jnp.dot(hn_ref[...], winb_ref[...], preferred_element_type=F32)
    row = lax.broadcasted_iota(jnp.int32, (seq_len, 1), 0)
    x = proj_ref[:, 0:LRU_W]
    xc =(cb_ref[...] + _shift_rows(x, 2, row) * cw_ref[0:1] + _shift_rows(x, 1, row) * cw_ref[1:2]
          + x * cw_ref[2:3] + _shift_rows(x, -1, row) * cw_ref[3:4])
    xc_ref[...] = xc

    lam = lam_ref[...]
    neg_csp = -LRU_C * (jnp.maximum(-lam, 0.0) + jnp.log1p(jnp.exp(-jnp.abs(lam))))

    for c in range(seq_len // LRU_CHUNK):
        rows = pl.ds(c * LRU_CHUNK, LRU_CHUNK)
        xcc = xc_ref[rows, :]
        gates = jnp.dot(xcc.astype(BF16), wbd_ref[...], preferred_element_type=F32) + bbd_ref[...]
        for d in range(2):
            r = jax.nn.sigmoid(gates[:, (2 * d) * LRU_W:(2 * d + 1) * LRU_W])
            i = jax.nn.sigmoid(gates[:, (2 * d + 1) * LRU_W:(2 * d + 2) * LRU_W])
            log_a = neg_csp[d:d + 1] * r
            t = jnp.tanh(log_a)
            one_minus_a2 = -2.0 * t / (1.0 - t)
            a_ref[d, rows, :] = jnp.exp(log_a)
            b_ref[d, rows, :] = jnp.sqrt(one_minus_a2) * (i * xcc)

    n_grp = seq_len // 8

    def scan_body(g, carry):
        hf, hb = carry
        rf = pl.ds(pl.multiple_of(g * 8, 8), 8)
        rb = pl.ds(pl.multiple_of((n_grp - 1 - g) * 8, 8), 8)
        af, bf = a_ref[0, rf, :], b_ref[0, rf, :]
        ab, bb = a_ref[1, rb, :], b_ref[1, rb, :]
        outs_f, outs_b = [], []
        for j in range(8):
            hf = af[j:j + 1] * hf + bf[j:j + 1]
            outs_f.append(hf)
            hb = ab[7 - j:8 - j] * hb + bb[7 - j:8 - j]
            outs_b.append(hb)
        b_ref[0, rf, :] = jnp.concatenate(outs_f, axis=0)
        b_ref[1, rb, :] = jnp.concatenate(outs_b[::-1], axis=0)
        return hf, hb

    hf, hb = lax.fori_loop(0, n_grp, scan_body, (h0_ref[0, 0:1, :], h0_ref[0, 1:2, :]))
    st_ref[0, 0:1, :] = hf
    st_ref[0, 1:2, :] = hb

    h_sum = b_ref[0] + b_ref[1]
    y_ref[:, 0:LRU_W] = (h_sum * jax.nn.gelu(proj_ref[:, LRU_W:2 * LRU_W])).astype(y_ref.dtype)
    o = 2 * LRU_W
    cv = proj_ref[:, o + SC_W:o + 2 * SC_W] * proj_ref[:, o + 2 * SC_W:o + 3 * SC_W]
    conv = scb_ref[...] + _shift_rows(cv, 1, row) * scw_ref[0:1] + cv * scw_ref[1:2] + _shift_rows(cv, -1, row) * scw_ref[2:3]
    y_ref[:, LRU_W:LRU_W + SC_W] = (proj_ref[:, o:o + SC_W] * conv).astype(y_ref.dtype)


def _lru_mixer(hn, h0, w_in, idx, cw, cb, wbd, bbd, lam, scw, scb, *, latent):
    if latent:
        n_seq, seq_len, blk0 = DEC_BATCH, DEC_SEQ, T_CTX // DEC_SEQ
    else:
        n_seq, seq_len, blk0 = BATCH, SEQ, 0
    full = lambda shape: pl.BlockSpec(shape, lambda b: (0,) * len(shape))
    return pl.pallas_call(
        functools.partial(_lru_kernel, seq_len=seq_len, idx=idx),
        grid=(n_seq,),
        in_specs=[pl.BlockSpec((seq_len, D), lambda b: (blk0 + b, 0)),
                  pl.BlockSpec((1, 2, LRU_W), lambda b: (b, 0, 0)),
                  pl.BlockSpec(memory_space=pl.ANY),
                  full((4, LRU_W)), full((1, LRU_W)), full((LRU_W, 4 * LRU_W)), full((1, 4 * LRU_W)),
                  full((2, LRU_W)), full((3, SC_W)), full((1, SC_W))],
        out_specs=[pl.BlockSpec((seq_len, D), lambda b: (b, 0)),
                   pl.BlockSpec((1, 2, LRU_W), lambda b: (b, 0, 0))],
        out_shape=[jax.ShapeDtypeStruct((n_seq * seq_len, D), BF16),
                   jax.ShapeDtypeStruct((n_seq, 2, LRU_W), F32)],
        scratch_shapes=[pltpu.VMEM((D, IN_W), BF16),
                        pltpu.VMEM((2, D, W_IN_CHUNK), F32),
                        pltpu.SemaphoreType.DMA((2,)),
                        pltpu.VMEM((seq_len, IN_W), F32),
                        pltpu.VMEM((seq_len, LRU_W), F32),
                        pltpu.VMEM((2, seq_len, LRU_W), F32),
                        pltpu.VMEM((2, seq_len, LRU_W), F32)],
        compiler_params=_params(("arbitrary",)),
        name="lru_mixer",
    )(hn, h0, w_in, cw, cb.reshape(1, LRU_W), wbd, bbd, lam, scw, scb.reshape(1, SC_W))


def _block_diag_dense(w):
    eye = jnp.eye(LRU_HEADS, dtype=w.dtype)
    return jnp.einsum('hij,hg->higj', w, eye).reshape(LRU_W, LRU_W)


def _route(idx):
    e_flat = idx.reshape(-1)
    onehot = (e_flat[:, None] == jnp.arange(N_EXPERTS)[None, :]).astype(jnp.int32)
    csum = jnp.cumsum(onehot, axis=0)
    counts = csum[-1]
    rank = jnp.take_along_axis(csum, e_flat[:, None], axis=1)[:, 0] - 1
    padded = (counts + MOE_TM - 1) // MOE_TM * MOE_TM
    ends = jnp.cumsum(padded)
    dest = ((ends - padded)[e_flat] + rank).astype(jnp.int32)
    tile_end = ends // MOE_TM
    n_used = tile_end[-1]
    tiles = jnp.minimum(jnp.arange(MOE_TILES), n_used - 1)
    tile_expert = jnp.sum(tiles[:, None] >= tile_end[None, :], axis=1).astype(jnp.int32)
    group_tile = jnp.arange(MOE_TILES) - ((ends - padded) // MOE_TM)[tile_expert]
    tile_rows = jnp.clip(counts[tile_expert] - group_tile * MOE_TM, 0, MOE_TM).astype(jnp.int32)
    return dest, tile_expert, n_used.reshape(1).astype(jnp.int32), tile_rows


def kernel(x_prompt, x_sample, state_lru, cache_k, cache_v, c, c_ctx, norm1_g, norm2_g, w_mod, b_mod, w_in, lru_conv_w, lru_conv_b, lru_wa, lru_ba, lru_wx, lru_bx, lru_lam, sc_conv_w, sc_conv_b, w_out, ffn_w_gate, ffn_w_up, ffn_w_down, w_qkv, w_o, rpb, w_router, moe_w_gate, moe_w_up, moe_w_down, final_g):
    cond = jnp.concatenate([c_ctx[None, :], c, jnp.zeros((COND_ROWS - N_COND, D), F32)], axis=0)
    mods = _modulation(cond, w_mod, b_mod).reshape(DEPTH, 6, COND_ROWS, 1, D)

    x, hn = _first_norm(x_prompt.reshape(T_CTX, D), x_sample.reshape(T_LAT, D), norm1_g[0], mods, 0)
    cache_kt = jnp.swapaxes(cache_k, -1, -2)
    cache_vt = jnp.swapaxes(cache_v, -1, -2)
    zero_state = jnp.zeros((BATCH, 2, LRU_W), F32)
    states = []
    new_k = jnp.zeros((BATCH, DEPTH // 2, N_HEADS, HEAD_DIM, SEQ), F32)
    new_v = jnp.zeros((BATCH, DEPTH // 2, N_HEADS, HEAD_DIM, SEQ), F32)
    for l in range(DEPTH):
        idx = l // 2
        if l % 2 == 0:
            wbd = jnp.concatenate([_block_diag_dense(lru_wa[idx, 0]), _block_diag_dense(lru_wx[idx, 0]),
                                   _block_diag_dense(lru_wa[idx, 1]), _block_diag_dense(lru_wx[idx, 1])],
                                  axis=1).astype(BF16)
            bbd = jnp.concatenate([lru_ba[idx, 0], lru_bx[idx, 0], lru_ba[idx, 1], lru_bx[idx, 1]]).reshape(1, 4 * LRU_W)
            lru_args = (lru_conv_w[idx], lru_conv_b[idx], wbd, bbd, lru_lam[idx], sc_conv_w[idx], sc_conv_b[idx])
            y_ctx, st = _lru_mixer(hn, zero_state, w_in, idx, *lru_args, latent=False)
            y_lat, _ = _lru_mixer(hn, state_lru[:, idx], w_in, idx, *lru_args, latent=True)
            states.append(st)
            x, hn2 = _proj_res_norm(y_ctx, y_lat, w_out, idx, x, norm2_g[l], mods, l)
            x, hn = _ffn(hn2, ffn_w_gate, ffn_w_up, ffn_w_down, idx, x, mods, l, norm1_g[l + 1])
        else:
            q_ctx, new_k, new_v = _qkv_heads(hn, w_qkv, idx, latent=False, k_into=new_k, v_into=new_v)
            q_lat, k_lat, v_lat = _qkv_heads(hn, w_qkv, idx, latent=True)
            o_ctx = _context_attention(q_ctx, new_k, new_v, idx)
            o_lat = _neighbourhood_attention(q_lat, k_lat, v_lat, cache_kt, cache_vt, _window_bias(rpb[idx]), idx)
            x, hn2, top_i, top_p = _proj_res_norm(o_ctx, o_lat, w_o, idx, x, norm2_g[l], mods, l, w_router[idx])
            dest, tile_expert, n_used, tile_rows = _route(top_i)
            xs = _moe_scatter(hn2, dest)
            y = _moe_experts(xs, tile_expert, n_used, tile_rows, moe_w_gate, moe_w_up, moe_w_down, idx)
            if l == DEPTH - 1:
                y_prompt, y_sample = _moe_combine(x, y, dest, top_p, mods, l, final_g, True)
            else:
                x, hn = _moe_combine(x, y, dest, top_p, mods, l, norm1_g[l + 1], False)

    return (y_prompt.reshape(BATCH, SEQ, D), y_sample.reshape(DEC_BATCH, DEC_SEQ, D), jnp.stack(states, axis=1),
            jnp.swapaxes(new_k, -1, -2), jnp.swapaxes(new_v, -1, -2))
```

```python
import functools

import jax
import jax.numpy as jnp
import numpy as np
from jax import lax
from jax.experimental import pallas as pl
from jax.experimental.pallas import tpu as pltpu

F32 = jnp.float32
BF16 = jnp.bfloat16

D = 1024
BATCH = 32
SEQ = 256
DEPTH = 4
DEC_BATCH = 2
DEC_SEQ = 1024
PAST_LEN = 256
GRID_W = 64
GRID_ROWS = DEC_SEQ // GRID_W
LRU_W = 512
LRU_HEADS = 8
LRU_BLOCK = LRU_W // LRU_HEADS
LRU_C = 8.0
SC_W = 512
IN_W = 2 * LRU_W + 3 * SC_W
N_HEADS = 16
HEAD_DIM = D // N_HEADS
WIN_R = 8
WIN_C = 16
D_FF = 2816
N_EXPERTS = 8
EPS = 1e-6
NEG_INF = -1e30

T_CTX = BATCH * SEQ
T_LAT = DEC_BATCH * DEC_SEQ
T = T_CTX + T_LAT
N_COND = 3
COND_ROWS = 8

TM = 1024
TF = 256
N_FCH = D_FF // TF
MOE_TM = 1024
MOE_TILE_PARTS = 8
MOE_TILES = (2 * T + N_EXPERTS * (MOE_TM - 1)) // MOE_TM
MOE_ROWS = MOE_TILES * MOE_TM
VMEM_LIMIT = 52 * 1024 * 1024


def _params(sem):
    return pltpu.CompilerParams(dimension_semantics=sem, vmem_limit_bytes=VMEM_LIMIT)


def _cond_of_tile(i, tm=TM):
    r = i * tm
    return jnp.where(r < T_CTX, 0, 1 + (r - T_CTX) // DEC_SEQ)


def _mod_spec(layer, k, grid_rank=1, tm=TM):
    if grid_rank == 1:
        return pl.BlockSpec((1, 1, 1, 1, D), lambda i: (layer, k, _cond_of_tile(i, tm), 0, 0))
    return pl.BlockSpec((1, 1, 1, 1, D), lambda i, f, *_: (layer, k, _cond_of_tile(i, tm), 0, 0))


def _rms_mod(x, g, scale, shift):
    y = x * lax.rsqrt(jnp.mean(x * x, axis=-1, keepdims=True) + EPS)
    return (y * g) * (1.0 + scale) + shift


def _mod_kernel(c_ref, w_ref, b_ref, o_ref):
    c = c_ref[...]
    s = c * jax.nn.sigmoid(c)
    o_ref[0, 0] = jnp.dot(s.astype(BF16), w_ref[0].astype(BF16), preferred_element_type=F32) + b_ref[0, 0]


def _modulation(cond, w_mod, b_mod):
    return pl.pallas_call(
        _mod_kernel,
        grid=(DEPTH, 6),
        in_specs=[
            pl.BlockSpec((COND_ROWS, D), lambda l, k: (0, 0)),
            pl.BlockSpec((1, D, D), lambda l, k: (l, 0, k)),
            pl.BlockSpec((1, 1, 1, D), lambda l, k: (l, k, 0, 0)),
        ],
        out_specs=pl.BlockSpec((1, 1, COND_ROWS, D), lambda l, k: (l, k, 0, 0)),
        out_shape=jax.ShapeDtypeStruct((DEPTH, 6, COND_ROWS, D), F32),
        compiler_params=_params(("arbitrary", "arbitrary")),
        name="adaln_mod",
    )(cond, w_mod, b_mod.reshape(DEPTH, 6, 1, D))


def _norm_kernel(xc_ref, xl_ref, g_ref, sc_ref, sh_ref, x_ref, o_ref):
    x = jnp.where(pl.program_id(0) < T_CTX // TM, xc_ref[...], xl_ref[...])
    x_ref[...] = x
    o_ref[...] = _rms_mod(x, g_ref[...], sc_ref[0, 0, 0], sh_ref[0, 0, 0]).astype(o_ref.dtype)


def _first_norm(x_ctx, x_lat, g, mods, layer):
    n_ctx = T_CTX // TM
    return pl.pallas_call(
        _norm_kernel,
        grid=(T // TM,),
        in_specs=[
            pl.BlockSpec((TM, D), lambda i: (jnp.minimum(i, n_ctx - 1), 0)),
            pl.BlockSpec((TM, D), lambda i: (jnp.maximum(i - n_ctx, 0), 0)),
            pl.BlockSpec((1, D), lambda i: (0, 0)),
            _mod_spec(layer, 1),
            _mod_spec(layer, 0),
        ],
        out_specs=[pl.BlockSpec((TM, D), lambda i: (i, 0)), pl.BlockSpec((TM, D), lambda i: (i, 0))],
        out_shape=[jax.ShapeDtypeStruct((T, D), F32), jax.ShapeDtypeStruct((T, D), BF16)],
        compiler_params=_params(("arbitrary",)),
        name="first_norm",
    )(x_ctx, x_lat, g.reshape(1, D), mods, mods)


QKV_TM = 512


def _qkv_kernel(a_ref, wq_ref, wk_ref, wv_ref, *rest, n_piece, piece_len, kv_transposed, aliased):
    if aliased:
        rest = rest[2:]
    q_ref, k_ref, v_ref, wb_ref, acc_ref = rest

    @pl.when(pl.program_id(0) == 0)
    def _():
        for part, w_ref in enumerate((wq_ref, wk_ref, wv_ref)):
            wb_ref[part] = w_ref[0].astype(BF16)

    a = a_ref[...]
    q = jnp.dot(a, wb_ref[0], preferred_element_type=F32) * (HEAD_DIM ** -0.5)
    for s in range(n_piece):
        for h in range(N_HEADS):
            q_ref[s, 0, h] = q[s * piece_len:(s + 1) * piece_len, h * HEAD_DIM:(h + 1) * HEAD_DIM].astype(q_ref.dtype)
    for part, o_ref in ((1, k_ref), (2, v_ref)):
        acc = jnp.dot(a, wb_ref[part], preferred_element_type=F32)
        if kv_transposed:
            acc_ref[...] = acc
            acc_t = acc_ref[...].T
            for s in range(n_piece):
                for h in range(N_HEADS):
                    o_ref[s, 0, h] = acc_t[h * HEAD_DIM:(h + 1) * HEAD_DIM, s * piece_len:(s + 1) * piece_len]
        else:
            for s in range(n_piece):
                for h in range(N_HEADS):
                    o_ref[s, 0, h] = acc[s * piece_len:(s + 1) * piece_len, h * HEAD_DIM:(h + 1) * HEAD_DIM]


def _qkv_heads(hn, w_qkv, idx, *, latent, k_into=None, v_into=None):
    if latent:
        n_batch, seq_len, row0, n_slots, slot = DEC_BATCH, DEC_SEQ, T_CTX // QKV_TM, 1, 0
    else:
        n_batch, seq_len, row0, n_slots, slot = BATCH, SEQ, 0, DEPTH // 2, idx
    piece_len = min(seq_len, QKV_TM)
    n_piece = QKV_TM // piece_len
    per_seq = seq_len // piece_len
    aliased = k_into is not None

    def head_spec(tail_t, s):
        if tail_t:
            shape = (n_piece, 1, N_HEADS, HEAD_DIM, piece_len)
            return pl.BlockSpec(shape, lambda i: (i, s, 0, 0, 0))
        shape = (n_piece, 1, N_HEADS, piece_len, HEAD_DIM)
        return pl.BlockSpec(shape, lambda i: (i // per_seq, s, 0, i % per_seq, 0))

    wspec = lambda part: pl.BlockSpec((1, D, D), lambda i: (idx, 0, part), pipeline_mode=pl.Buffered(1))
    in_specs = [pl.BlockSpec((QKV_TM, D), lambda i: (row0 + i, 0)), wspec(0), wspec(1), wspec(2)]
    args = [hn, w_qkv, w_qkv, w_qkv]
    aliases = {}
    kv_tail = (HEAD_DIM, seq_len) if aliased else (seq_len, HEAD_DIM)
    if aliased:
        in_specs += [pl.BlockSpec(memory_space=pl.ANY), pl.BlockSpec(memory_space=pl.ANY)]
        args += [k_into, v_into]
        aliases = {4: 1, 5: 2}
    kv_shape = jax.ShapeDtypeStruct((n_batch, n_slots, N_HEADS) + kv_tail, F32)
    return pl.pallas_call(
        functools.partial(_qkv_kernel, n_piece=n_piece, piece_len=piece_len, kv_transposed=aliased, aliased=aliased),
        grid=(n_batch * seq_len // QKV_TM,),
        in_specs=in_specs,
        out_specs=[head_spec(False, 0), head_spec(aliased, slot), head_spec(aliased, slot)],
        out_shape=[jax.ShapeDtypeStruct((n_batch, 1, N_HEADS, seq_len, HEAD_DIM), BF16), kv_shape, kv_shape],
        scratch_shapes=[pltpu.VMEM((3, D, D), BF16), pltpu.VMEM((QKV_TM, D), F32)],
        input_output_aliases=aliases,
        compiler_params=_params(("arbitrary",)),
        name="qkv_heads",
    )(*args)


def _dot_nt(a, b):
    return lax.dot_general(a, b, (((1,), (1,)), ((), ())), preferred_element_type=F32)


def _softmax_pv(s_list, v_list, v_transposed=False):
    m = s_list[0].max(axis=-1, keepdims=True)
    for s in s_list[1:]:
        m = jnp.maximum(m, s.max(axis=-1, keepdims=True))
    den = None
    out = None
    for s, v in zip(s_list, v_list):
        p = jnp.exp(s - m)
        d = p.sum(axis=-1, keepdims=True)
        pb = p.astype(BF16)
        o = _dot_nt(pb, v) if v_transposed else jnp.dot(pb, v, preferred_element_type=F32)
        den = d if den is None else den + d
        out = o if out is None else out + o
    return out / den


HEAD_PAIR_W = 2 * HEAD_DIM


def _ctx_attn_kernel(q_ref, kt_ref, vt_ref, o_ref):
    for hp in range(N_HEADS // 2):
        pair = []
        for h in (2 * hp, 2 * hp + 1):
            s = jnp.dot(q_ref[0, 0, h], kt_ref[0, 0, h].astype(BF16), preferred_element_type=F32)
            pair.append(_softmax_pv([s], [vt_ref[0, 0, h].astype(BF16)], v_transposed=True))
        o_ref[:, hp * HEAD_PAIR_W:(hp + 1) * HEAD_PAIR_W] = jnp.concatenate(pair, axis=-1).astype(o_ref.dtype)


def _context_attention(q, kt, vt, slot):
    kv_spec = pl.BlockSpec((1, 1, N_HEADS, HEAD_DIM, SEQ), lambda b: (b, slot, 0, 0, 0))
    return pl.pallas_call(
        _ctx_attn_kernel,
        grid=(BATCH,),
        in_specs=[pl.BlockSpec((1, 1, N_HEADS, SEQ, HEAD_DIM), lambda b: (b, 0, 0, 0, 0)), kv_spec, kv_spec],
        out_specs=pl.BlockSpec((SEQ, D), lambda b: (b, 0)),
        out_shape=jax.ShapeDtypeStruct((T_CTX, D), BF16),
        compiler_params=_params(("arbitrary",)),
        name="ctx_attention",
    )(q, kt, vt)


def _row_start(r):
    return min(max(r - WIN_R // 2, 0), GRID_ROWS - WIN_R)


def _nbr_attn_kernel(q_ref, k_ref, v_ref, kct_ref, vct_ref, bias_ref, o_ref):
    kct = [kct_ref[0, 0, hh].astype(BF16) for hh in range(2)]
    vct = [vct_ref[0, 0, hh].astype(BF16) for hh in range(2)]
    for r0, r1 in _ROW_GROUPS:
        rs = _row_start(r0)
        n_q = (r1 - r0) * GRID_W
        pair = []
        for hh in range(2):
            q = q_ref[0, 0, hh, r0 * GRID_W:r1 * GRID_W, :]
            kw = k_ref[0, 0, hh, rs * GRID_W:(rs + WIN_R) * GRID_W, :].astype(BF16)
            vw = v_ref[0, 0, hh, rs * GRID_W:(rs + WIN_R) * GRID_W, :].astype(BF16)
            bias = bias_ref[hh, r0 - rs:r1 - rs].reshape(n_q, WIN_R * GRID_W)
            s_win = _dot_nt(q, kw) + bias
            s_ctx = jnp.dot(q, kct[hh], preferred_element_type=F32)
            m = jnp.maximum(s_win.max(axis=-1, keepdims=True), s_ctx.max(axis=-1, keepdims=True))
            p_win = jnp.exp(s_win - m)
            p_ctx = jnp.exp(s_ctx - m)
            den = p_win.sum(axis=-1, keepdims=True) + p_ctx.sum(axis=-1, keepdims=True)
            out = (jnp.dot(p_win.astype(BF16), vw, preferred_element_type=F32)
                   + _dot_nt(p_ctx.astype(BF16), vct[hh]))
            pair.append(out / den)
        o_ref[r0 * GRID_W:r1 * GRID_W, :] = jnp.concatenate(pair, axis=-1).astype(o_ref.dtype)


def _row_groups():
    groups, r0 = [], 0
    for r in range(1, GRID_ROWS + 1):
        if r == GRID_ROWS or _row_start(r) != _row_start(r0):
            groups.append((r0, r))
            r0 = r
    return groups


_ROW_GROUPS = _row_groups()


def _window_bias(rpb_l):
    col = np.arange(GRID_W)
    col_start = np.clip(col - WIN_C // 2, 0, GRID_W - WIN_C)
    col_mask = (col[None, :] >= col_start[:, None]) & (col[None, :] < col_start[:, None] + WIN_C)
    dc = np.clip(col[None, :] - col[:, None], -(WIN_C - 1), WIN_C - 1) + (WIN_C - 1)
    onehot = (dc[None, :, :] == np.arange(2 * WIN_C - 1)[:, None, None]).astype(np.float32)
    rows = jnp.stack([rpb_l[:, WIN_R - 1 - v:2 * WIN_R - 1 - v, :] for v in range(WIN_R)], axis=1)
    bias = jnp.einsum('hvkd,dqc->hvqkc', rows, jnp.asarray(onehot), precision=lax.Precision.HIGHEST)
    bias = jnp.where(jnp.asarray(col_mask)[None, None, :, None, :], bias, NEG_INF)
    return bias.reshape(N_HEADS, WIN_R, GRID_W, WIN_R * GRID_W)


def _neighbourhood_attention(q, k, v, cache_kt, cache_vt, bias, slot):
    lat = lambda: pl.BlockSpec((1, 1, 2, DEC_SEQ, HEAD_DIM), lambda b, hp: (b, 0, hp, 0, 0))
    ctx = lambda: pl.BlockSpec((1, 1, 2, HEAD_DIM, PAST_LEN), lambda b, hp: (b, slot, hp, 0, 0))
    return pl.pallas_call(
        _nbr_attn_kernel,
        grid=(DEC_BATCH, N_HEADS // 2),
        in_specs=[lat(), lat(), lat(), ctx(), ctx(),
                  pl.BlockSpec((2, WIN_R, GRID_W, WIN_R * GRID_W), lambda b, hp: (hp, 0, 0, 0))],
        out_specs=pl.BlockSpec((DEC_SEQ, HEAD_PAIR_W), lambda b, hp: (b, hp)),
        out_shape=jax.ShapeDtypeStruct((T_LAT, D), BF16),
        compiler_params=_params(("arbitrary", "arbitrary")),
        name="nbr_attention",
    )(q, k, v, cache_kt, cache_vt, bias)


def _top2(logits):
    col = lax.broadcasted_iota(jnp.int32, logits.shape, 1)
    m1 = jnp.max(logits, axis=-1, keepdims=True)
    i1 = jnp.min(jnp.where(logits == m1, col, N_EXPERTS), axis=-1, keepdims=True)
    rest = jnp.where(col == i1, -jnp.inf, logits)
    m2 = jnp.max(rest, axis=-1, keepdims=True)
    i2 = jnp.min(jnp.where(rest == m2, col, N_EXPERTS), axis=-1, keepdims=True)
    e = jnp.exp(m2 - m1)
    return i1, i2, 1.0 / (1.0 + e), e / (1.0 + e)


CTX_TILES = T_CTX // TM


def _proj_res_norm_kernel(ac_ref, al_ref, w_ref, x_ref, g1_ref, ng_ref, sc_ref, sh_ref, *rest, router):
    if router:
        wr_ref, xo_ref, hn_ref, idx_ref, gate_ref, wb_ref = rest
    else:
        xo_ref, hn_ref, wb_ref = rest

    @pl.when(pl.program_id(0) == 0)
    def _():
        wb_ref[...] = w_ref[0].astype(BF16)

    a = jnp.where(pl.program_id(0) < CTX_TILES, ac_ref[...], al_ref[...])
    mix = jnp.dot(a, wb_ref[...], preferred_element_type=F32)
    x = x_ref[...] + g1_ref[0, 0, 0] * mix
    xo_ref[...] = x
    hn = _rms_mod(x, ng_ref[...], sc_ref[0, 0, 0], sh_ref[0, 0, 0])
    hn_ref[...] = hn.astype(hn_ref.dtype)
    if router:
        hi = hn.astype(BF16)
        lo = (hn - hi.astype(F32)).astype(BF16)
        l_hi = jnp.dot(hi, wr_ref[...], preferred_element_type=F32)
        l_lo = jnp.dot(lo, wr_ref[...], preferred_element_type=F32)
        logits = l_hi[:, :N_EXPERTS] + l_hi[:, N_EXPERTS:] + l_lo[:, :N_EXPERTS]
        i1, i2, p1, p2 = _top2(logits)
        first = lax.broadcasted_iota(jnp.int32, (logits.shape[0], 2), 1) == 0
        idx_ref[...] = jnp.where(first, i1, i2)
        gate_ref[...] = jnp.where(first, p1, p2)


def _proj_res_norm(a_ctx, a_lat, w, idx, x, norm_g, mods, layer, w_router=None):
    router = w_router is not None
    row = lambda n: pl.BlockSpec((TM, n), lambda i: (i, 0))
    in_specs = [pl.BlockSpec((TM, D), lambda i: (jnp.minimum(i, CTX_TILES - 1), 0)),
                pl.BlockSpec((TM, D), lambda i: (jnp.maximum(i - CTX_TILES, 0), 0)),
                pl.BlockSpec((1, D, D), lambda i: (idx, 0, 0)), row(D),
                _mod_spec(layer, 2), pl.BlockSpec((1, D), lambda i: (0, 0)), _mod_spec(layer, 4), _mod_spec(layer, 3)]
    args = [a_ctx, a_lat, w, x, mods, norm_g.reshape(1, D), mods, mods]
    out_specs = [row(D), row(D)]
    out_shape = [jax.ShapeDtypeStruct((T, D), F32), jax.ShapeDtypeStruct((T, D), F32 if router else BF16)]
    if router:
        w_hi = w_router.astype(BF16)
        w_lo = (w_router - w_hi.astype(F32)).astype(BF16)
        in_specs.append(pl.BlockSpec((D, 2 * N_EXPERTS), lambda i: (0, 0)))
        args.append(jnp.concatenate([w_hi, w_lo], axis=1))
        out_specs += [row(2), row(2)]
        out_shape += [jax.ShapeDtypeStruct((T, 2), jnp.int32), jax.ShapeDtypeStruct((T, 2), F32)]
    return pl.pallas_call(
        functools.partial(_proj_res_norm_kernel, router=router),
        grid=(T // TM,),
        in_specs=in_specs,
        out_specs=out_specs,
        out_shape=out_shape,
        scratch_shapes=[pltpu.VMEM((D, D), BF16)],
        compiler_params=_params(("arbitrary",)),
        name="proj_res_norm",
    )(*args)


def _swiglu_chunk(hn, wg_ref, wu_ref, wd_ref, acc_ref):
    g = jnp.dot(hn, wg_ref[0].astype(BF16), preferred_element_type=F32)
    u = jnp.dot(hn, wu_ref[0].astype(BF16), preferred_element_type=F32)
    h = (g * jax.nn.sigmoid(g) * u).astype(BF16)
    acc_ref[...] += jnp.dot(h, wd_ref[0].astype(BF16), preferred_element_type=F32)


def _ffn_kernel(hn_ref, wg_ref, wu_ref, wd_ref, x_ref, g2_ref, ng_ref, sc_ref, sh_ref, xo_ref, hno_ref, acc_ref):
    @pl.when(pl.program_id(1) == 0)
    def _():
        acc_ref[...] = jnp.zeros_like(acc_ref)

    _swiglu_chunk(hn_ref[...], wg_ref, wu_ref, wd_ref, acc_ref)

    @pl.when(pl.program_id(1) == N_FCH - 1)
    def _():
        x = x_ref[...] + g2_ref[0, 0, 0] * acc_ref[...]
        xo_ref[...] = x
        hno_ref[...] = _rms_mod(x, ng_ref[...], sc_ref[0, 0, 0], sh_ref[0, 0, 0]).astype(hno_ref.dtype)


def _ffn(hn, wg, wu, wd, idx, x, mods, layer, next_g):
    row = lambda: pl.BlockSpec((TM, D), lambda i, f: (i, 0))
    return pl.pallas_call(
        _ffn_kernel,
        grid=(T // TM, N_FCH),
        in_specs=[row(),
                  pl.BlockSpec((1, D, TF), lambda i, f: (idx, 0, f)),
                  pl.BlockSpec((1, D, TF), lambda i, f: (idx, 0, f)),
                  pl.BlockSpec((1, TF, D), lambda i, f: (idx, f, 0)),
                  row(), _mod_spec(layer, 5, 2), pl.BlockSpec((1, D), lambda i, f: (0, 0)),
                  _mod_spec(layer + 1, 1, 2), _mod_spec(layer + 1, 0, 2)],
        out_specs=[row(), row()],
        out_shape=[jax.ShapeDtypeStruct((T, D), F32), jax.ShapeDtypeStruct((T, D), BF16)],
        scratch_shapes=[pltpu.VMEM((TM, D), F32)],
        compiler_params=_params(("arbitrary", "arbitrary")),
        name="ffn",
    )(hn, wg, wu, wd, x, mods, next_g.reshape(1, D), mods, mods)


ROW_UNROLL = 8
SCATTER_TM = 1024
COMBINE_TM = 1024


def _wait_rows(src_ref, dst_ref, sem, n_rows):
    pltpu.make_async_copy(src_ref.at[pl.ds(0, n_rows)], dst_ref.at[pl.ds(0, n_rows)], sem).wait()


MIN_USED_TILES = 2 * T // MOE_TM
N_ZERO_TILES = N_EXPERTS + MOE_TILES - MIN_USED_TILES


def _scatter_kernel(dest_ref, ztile_ref, hn_ref, xs_ref, zero_ref, sem, zsem):
    @pl.when(pl.program_id(0) == 0)
    def _():
        zero_ref[...] = jnp.zeros_like(zero_ref)

        def zero_copy(z):
            start = pl.multiple_of(ztile_ref[z] * MOE_TM, MOE_TM)
            return pltpu.make_async_copy(zero_ref, xs_ref.at[pl.ds(start, MOE_TM)], zsem)

        for z in range(N_ZERO_TILES):
            @pl.when(ztile_ref[z] >= 0)
            def _(z=z):
                zero_copy(z).start()

        for z in range(N_ZERO_TILES):
            @pl.when(ztile_ref[z] >= 0)
            def _(z=z):
                zero_copy(z).wait()

    base = pl.program_id(0) * (2 * SCATTER_TM)

    def body(g, carry):
        r0 = pl.multiple_of(g * ROW_UNROLL, ROW_UNROLL)
        for u in range(ROW_UNROLL):
            for k in range(2):
                d = dest_ref[base + 2 * (r0 + u) + k]
                pltpu.make_async_copy(hn_ref.at[pl.ds(r0, ROW_UNROLL)].at[pl.ds(u, 1)], xs_ref.at[pl.ds(d, 1)],
                                      sem).start(priority=k)
        return carry

    lax.fori_loop(0, SCATTER_TM // ROW_UNROLL, body, 0)
    for _ in range(2):
        _wait_rows(hn_ref, xs_ref, sem, SCATTER_TM)


def _moe_scatter(hn2, dest_flat, zero_tiles):
    return pl.pallas_call(
        _scatter_kernel,
        grid_spec=pltpu.PrefetchScalarGridSpec(
            num_scalar_prefetch=2,
            grid=(T // SCATTER_TM,),
            in_specs=[pl.BlockSpec((SCATTER_TM, D), lambda j, dest, zt: (j, 0))],
            out_specs=pl.BlockSpec(memory_space=pl.ANY),
            scratch_shapes=[pltpu.VMEM((MOE_TM, D), F32), pltpu.SemaphoreType.DMA, pltpu.SemaphoreType.DMA],
        ),
        out_shape=jax.ShapeDtypeStruct((MOE_ROWS, D), F32),
        compiler_params=_params(("arbitrary",)),
        name="moe_scatter",
    )(dest_flat, zero_tiles, hn2)


def _moe_kernel(te_ref, nu_ref, rows_ref, xs_ref, wg_ref, wu_ref, wd_ref, y_ref, xsb_ref, acc_ref):
    i = pl.program_id(0)
    f = pl.program_id(1)
    part = MOE_TM // MOE_TILE_PARTS

    @pl.when(i < nu_ref[0])
    def _():
        @pl.when(f == 0)
        def _():
            xsb_ref[...] = xs_ref[...].astype(BF16)
            acc_ref[...] = jnp.zeros_like(acc_ref)

        n_parts = (rows_ref[i] + part - 1) // part
        for m in range(1, MOE_TILE_PARTS + 1):
            @pl.when(n_parts == m)
            def _(m=m):
                _swiglu_chunk(xsb_ref[0:m * part], wg_ref, wu_ref, wd_ref, acc_ref.at[0:m * part])

        @pl.when(f == N_FCH - 1)
        def _():
            y_ref[...] = acc_ref[...]

    @pl.when(jnp.logical_and(i >= nu_ref[0], f == N_FCH - 1))
    def _():
        y_ref[...] = jnp.zeros_like(y_ref)


def _moe_experts(xs, tile_expert, n_used, tile_rows, wg, wu, wd, idx):
    def wspec(shape, col):
        def imap(i, f, te, nu, rows):
            fe = jnp.where(i < nu[0], f, N_FCH - 1)
            return (idx, te[i], 0, fe) if col else (idx, te[i], fe, 0)
        return pl.BlockSpec(shape, imap)

    def kern(te, nu, rows, xs_ref, wg_ref, wu_ref, wd_ref, y_ref, xsb_ref, acc_ref):
        _moe_kernel(te, nu, rows, xs_ref, wg_ref.at[0], wu_ref.at[0], wd_ref.at[0], y_ref, xsb_ref, acc_ref)

    return pl.pallas_call(
        kern,
        grid_spec=pltpu.PrefetchScalarGridSpec(
            num_scalar_prefetch=3,
            grid=(MOE_TILES, N_FCH),
            in_specs=[pl.BlockSpec((MOE_TM, D), lambda i, f, te, nu, rows: (jnp.minimum(i, nu[0] - 1), 0)),
                      wspec((1, 1, D, TF), True), wspec((1, 1, D, TF), True), wspec((1, 1, TF, D), False)],
            out_specs=pl.BlockSpec((MOE_TM, D), lambda i, f, te, nu, rows: (i, 0)),
            scratch_shapes=[pltpu.VMEM((MOE_TM, D), BF16), pltpu.VMEM((MOE_TM, D), F32)],
        ),
        out_shape=jax.ShapeDtypeStruct((MOE_ROWS, D), F32),
        compiler_params=_params(("arbitrary", "arbitrary")),
        name="moe_experts",
    )(tile_expert, n_used, tile_rows, xs, wg, wu, wd)


CTX_COMBINE_TILES = T_CTX // COMBINE_TM


def _combine_kernel(dest_ref, x_ref, y_hbm, gate_ref, g2_ref, ng_ref, sc_ref, sh_ref, *rest, final):
    *outs, ybuf, sem = rest
    j = pl.program_id(0)
    slot = j % 2

    def issue(tile, s):
        base = tile * (2 * COMBINE_TM)

        def body(g, carry):
            r0 = pl.multiple_of(g * ROW_UNROLL, ROW_UNROLL)
            for u in range(ROW_UNROLL):
                for k in range(2):
                    d = dest_ref[base + 2 * (r0 + u) + k]
                    rows = ybuf.at[s, pl.ds(k * COMBINE_TM + r0, ROW_UNROLL)]
                    pltpu.make_async_copy(y_hbm.at[pl.ds(d, 1)], rows.at[pl.ds(u, 1)], sem.at[s]).start(priority=k)
            return carry

        lax.fori_loop(0, COMBINE_TM // ROW_UNROLL, body, 0)

    @pl.when(j == 0)
    def _():
        issue(0, 0)

    @pl.when(j + 1 < pl.num_programs(0))
    def _():
        issue(j + 1, 1 - slot)

    _wait_rows(y_hbm, ybuf.at[slot], sem.at[slot], 2 * COMBINE_TM)
    gates = gate_ref[...]
    moe = gates[:, 0:1] * ybuf[slot, 0:COMBINE_TM] + gates[:, 1:2] * ybuf[slot, COMBINE_TM:2 * COMBINE_TM]
    x = x_ref[...] + g2_ref[0, 0, 0] * moe
    if final:
        yp_ref, ys_ref = outs
        y = _rms_mod(x, ng_ref[...], 0.0, 0.0)

        @pl.when(j < CTX_COMBINE_TILES)
        def _():
            yp_ref[...] = y

        @pl.when(j >= CTX_COMBINE_TILES)
        def _():
            ys_ref[...] = y
    else:
        xo_ref, hno_ref = outs
        xo_ref[...] = x
        hno_ref[...] = _rms_mod(x, ng_ref[...], sc_ref[0, 0, 0], sh_ref[0, 0, 0]).astype(hno_ref.dtype)


def _moe_combine(x, y, dest_flat, gates, mods, layer, next_g, final):
    row = lambda n=D: pl.BlockSpec((COMBINE_TM, n), lambda j, dest: (j, 0))
    vec = lambda: pl.BlockSpec((1, D), lambda j, dest: (0, 0))
    mod = lambda l, k: pl.BlockSpec((1, 1, 1, 1, D), lambda j, dest: (l, k, _cond_of_tile(j, COMBINE_TM), 0, 0))
    nl = layer if final else layer + 1
    if final:
        out_specs = [pl.BlockSpec((COMBINE_TM, D), lambda j, dest: (jnp.minimum(j, CTX_COMBINE_TILES - 1), 0)),
                     pl.BlockSpec((COMBINE_TM, D), lambda j, dest: (jnp.maximum(j - CTX_COMBINE_TILES, 0), 0))]
        out_shape = [jax.ShapeDtypeStruct((T_CTX, D), F32), jax.ShapeDtypeStruct((T_LAT, D), F32)]
    else:
        out_specs = [row(), row()]
        out_shape = [jax.ShapeDtypeStruct((T, D), F32), jax.ShapeDtypeStruct((T, D), BF16)]
    return pl.pallas_call(
        functools.partial(_combine_kernel, final=final),
        grid_spec=pltpu.PrefetchScalarGridSpec(
            num_scalar_prefetch=1,
            grid=(T // COMBINE_TM,),
            in_specs=[row(), pl.BlockSpec(memory_space=pl.ANY), row(2), mod(layer, 5), vec(), mod(nl, 1), mod(nl, 0)],
            out_specs=out_specs,
            scratch_shapes=[pltpu.VMEM((2, 2 * COMBINE_TM, D), F32), pltpu.SemaphoreType.DMA((2,))],
        ),
        out_shape=out_shape,
        compiler_params=_params(("arbitrary",)),
        name="moe_combine",
    )(dest_flat, x, y, gates, mods, next_g.reshape(1, D), mods, mods)


LRU_CHUNK = 256


def _shift_rows(v, s, row):
    n = v.shape[0]
    rolled = pltpu.roll(v, s % n, axis=0)
    keep = (row >= s) if s > 0 else (row < n + s)
    return jnp.where(keep, rolled, 0.0)


W_IN_CHUNK = 512
N_W_IN_CHUNKS = IN_W // W_IN_CHUNK


def _lru_kernel(hn_ref, h0_ref, win_hbm, cw_ref, cb_ref, wbd_ref, bbd_ref, lam_ref, scw_ref, scb_ref,
                y_ref, st_ref, winb_ref, stage_ref, stage_sem, proj_ref, xc_ref, a_ref, b_ref, *, seq_len, idx):
    @pl.when(pl.program_id(0) == 0)
    def _():
        copies = [pltpu.make_async_copy(win_hbm.at[idx, :, pl.ds(j * W_IN_CHUNK, W_IN_CHUNK)],
                                        stage_ref.at[j % 2], stage_sem.at[j % 2]) for j in range(N_W_IN_CHUNKS)]
        copies[0].start()
        for j in range(N_W_IN_CHUNKS):
            if j + 1 < N_W_IN_CHUNKS:
                copies[j + 1].start()
            copies[j].wait()
            winb_ref[:, j * W_IN_CHUNK:(j + 1) * W_IN_CHUNK] = stage_ref[j % 2].astype(BF16)

    proj_ref[...] = jnp.dot(hn_ref[...], winb_ref[...], preferred_element_type=F32)
    row = lax.broadcasted_iota(jnp.int32, (seq_len, 1), 0)
    x = proj_ref[:, 0:LRU_W]
    xc =(cb_ref[...] + _shift_rows(x, 2, row) * cw_ref[0:1] + _shift_rows(x, 1, row) * cw_ref[1:2]
          + x * cw_ref[2:3] + _shift_rows(x, -1, row) * cw_ref[3:4])
    xc_ref[...] = xc

    lam = lam_ref[...]
    neg_csp = -LRU_C * (jnp.maximum(-lam, 0.0) + jnp.log1p(jnp.exp(-jnp.abs(lam))))

    for c in range(seq_len // LRU_CHUNK):
        rows = pl.ds(c * LRU_CHUNK, LRU_CHUNK)
        xcc = xc_ref[rows, :]
        gates = jnp.dot(xcc.astype(BF16), wbd_ref[...], preferred_element_type=F32) + bbd_ref[...]
        for d in range(2):
            r = jax.nn.sigmoid(gates[:, (2 * d) * LRU_W:(2 * d + 1) * LRU_W])
            i = jax.nn.sigmoid(gates[:, (2 * d + 1) * LRU_W:(2 * d + 2) * LRU_W])
            log_a = neg_csp[d:d + 1] * r
            t = jnp.tanh(log_a)
            one_minus_a2 = -2.0 * t / (1.0 - t)
            a_ref[d, rows, :] = jnp.exp(log_a)
            b_ref[d, rows, :] = jnp.sqrt(one_minus_a2) * (i * xcc)

    n_grp = seq_len // 8

    def scan_body(g, carry):
        hf, hb = carry
        rf = pl.ds(pl.multiple_of(g * 8, 8), 8)
        rb = pl.ds(pl.multiple_of((n_grp - 1 - g) * 8, 8), 8)
        af, bf = a_ref[0, rf, :], b_ref[0, rf, :]
        ab, bb = a_ref[1, rb, :], b_ref[1, rb, :]
        outs_f, outs_b = [], []
        for j in range(8):
            hf = af[j:j + 1] * hf + bf[j:j + 1]
            outs_f.append(hf)
            hb = ab[7 - j:8 - j] * hb + bb[7 - j:8 - j]
            outs_b.append(hb)
        b_ref[0, rf, :] = jnp.concatenate(outs_f, axis=0)
        b_ref[1, rb, :] = jnp.concatenate(outs_b[::-1], axis=0)
        return hf, hb

    hf, hb = lax.fori_loop(0, n_grp, scan_body, (h0_ref[0, 0:1, :], h0_ref[0, 1:2, :]))
    st_ref[0, 0:1, :] = hf
    st_ref[0, 1:2, :] = hb

    h_sum = b_ref[0] + b_ref[1]
    y_ref[:, 0:LRU_W] = (h_sum * jax.nn.gelu(proj_ref[:, LRU_W:2 * LRU_W])).astype(y_ref.dtype)
    o = 2 * LRU_W
    cv = proj_ref[:, o + SC_W:o + 2 * SC_W] * proj_ref[:, o + 2 * SC_W:o + 3 * SC_W]
    conv = scb_ref[...] + _shift_rows(cv, 1, row) * scw_ref[0:1] + cv * scw_ref[1:2] + _shift_rows(cv, -1, row) * scw_ref[2:3]
    y_ref[:, LRU_W:LRU_W + SC_W] = (proj_ref[:, o:o + SC_W] * conv).astype(y_ref.dtype)


def _lru_mixer(hn, h0, w_in, idx, cw, cb, wbd, bbd, lam, scw, scb, *, latent):
    if latent:
        n_seq, seq_len, blk0 = DEC_BATCH, DEC_SEQ, T_CTX // DEC_SEQ
    else:
        n_seq, seq_len, blk0 = BATCH, SEQ, 0
    full = lambda shape: pl.BlockSpec(shape, lambda b: (0,) * len(shape))
    return pl.pallas_call(
        functools.partial(_lru_kernel, seq_len=seq_len, idx=idx),
        grid=(n_seq,),
        in_specs=[pl.BlockSpec((seq_len, D), lambda b: (blk0 + b, 0)),
                  pl.BlockSpec((1, 2, LRU_W), lambda b: (b, 0, 0)),
                  pl.BlockSpec(memory_space=pl.ANY),
                  full((4, LRU_W)), full((1, LRU_W)), full((LRU_W, 4 * LRU_W)), full((1, 4 * LRU_W)),
                  full((2, LRU_W)), full((3, SC_W)), full((1, SC_W))],
        out_specs=[pl.BlockSpec((seq_len, D), lambda b: (b, 0)),
                   pl.BlockSpec((1, 2, LRU_W), lambda b: (b, 0, 0))],
        out_shape=[jax.ShapeDtypeStruct((n_seq * seq_len, D), BF16),
                   jax.ShapeDtypeStruct((n_seq, 2, LRU_W), F32)],
        scratch_shapes=[pltpu.VMEM((D, IN_W), BF16),
                        pltpu.VMEM((2, D, W_IN_CHUNK), F32),
                        pltpu.SemaphoreType.DMA((2,)),
                        pltpu.VMEM((seq_len, IN_W), F32),
                        pltpu.VMEM((seq_len, LRU_W), F32),
                        pltpu.VMEM((2, seq_len, LRU_W), F32),
                        pltpu.VMEM((2, seq_len, LRU_W), F32)],
        compiler_params=_params(("arbitrary",)),
        name="lru_mixer",
    )(hn, h0, w_in, cw, cb.reshape(1, LRU_W), wbd, bbd, lam, scw, scb.reshape(1, SC_W))


def _block_diag_dense(w):
    eye = jnp.eye(LRU_HEADS, dtype=w.dtype)
    return jnp.einsum('hij,hg->higj', w, eye).reshape(LRU_W, LRU_W)


def _route(idx):
    e_flat = idx.reshape(-1)
    onehot = (e_flat[:, None] == jnp.arange(N_EXPERTS)[None, :]).astype(jnp.int32)
    csum = jnp.cumsum(onehot, axis=0)
    counts = csum[-1]
    rank = jnp.take_along_axis(csum, e_flat[:, None], axis=1)[:, 0] - 1
    padded = (counts + MOE_TM - 1) // MOE_TM * MOE_TM
    ends = jnp.cumsum(padded)
    dest = ((ends - padded)[e_flat] + rank).astype(jnp.int32)
    tile_end = ends // MOE_TM
    n_used = tile_end[-1]
    tiles = jnp.minimum(jnp.arange(MOE_TILES), n_used - 1)
    tile_expert = jnp.sum(tiles[:, None] >= tile_end[None, :], axis=1).astype(jnp.int32)
    group_tile = jnp.arange(MOE_TILES) - ((ends - padded) // MOE_TM)[tile_expert]
    tile_rows = jnp.clip(counts[tile_expert] - group_tile * MOE_TM, 0, MOE_TM).astype(jnp.int32)
    last_tiles = jnp.where(padded > 0, tile_end - 1, -1)
    tail = jnp.arange(MIN_USED_TILES, MOE_TILES)
    zero_tiles = jnp.concatenate([last_tiles, jnp.where(tail >= n_used, tail, -1)]).astype(jnp.int32)
    return dest, tile_expert, n_used.reshape(1).astype(jnp.int32), tile_rows, zero_tiles


def kernel(x_prompt, x_sample, state_lru, cache_k, cache_v, c, c_ctx, norm1_g, norm2_g, w_mod, b_mod, w_in, lru_conv_w, lru_conv_b, lru_wa, lru_ba, lru_wx, lru_bx, lru_lam, sc_conv_w, sc_conv_b, w_out, ffn_w_gate, ffn_w_up, ffn_w_down, w_qkv, w_o, rpb, w_router, moe_w_gate, moe_w_up, moe_w_down, final_g):
    cond = jnp.concatenate([c_ctx[None, :], c, jnp.zeros((COND_ROWS - N_COND, D), F32)], axis=0)
    mods = _modulation(cond, w_mod, b_mod).reshape(DEPTH, 6, COND_ROWS, 1, D)

    x, hn = _first_norm(x_prompt.reshape(T_CTX, D), x_sample.reshape(T_LAT, D), norm1_g[0], mods, 0)
    cache_kt = jnp.swapaxes(cache_k, -1, -2)
    cache_vt = jnp.swapaxes(cache_v, -1, -2)
    zero_state = jnp.zeros((BATCH, 2, LRU_W), F32)
    states = []
    new_k = jnp.zeros((BATCH, DEPTH // 2, N_HEADS, HEAD_DIM, SEQ), F32)
    new_v = jnp.zeros((BATCH, DEPTH // 2, N_HEADS, HEAD_DIM, SEQ), F32)
    for l in range(DEPTH):
        idx = l // 2
        if l % 2 == 0:
            wbd = jnp.concatenate([_block_diag_dense(lru_wa[idx, 0]), _block_diag_dense(lru_wx[idx, 0]),
                                   _block_diag_dense(lru_wa[idx, 1]), _block_diag_dense(lru_wx[idx, 1])],
                                  axis=1).astype(BF16)
            bbd = jnp.concatenate([lru_ba[idx, 0], lru_bx[idx, 0], lru_ba[idx, 1], lru_bx[idx, 1]]).reshape(1, 4 * LRU_W)
            lru_args = (lru_conv_w[idx], lru_conv_b[idx], wbd, bbd, lru_lam[idx], sc_conv_w[idx], sc_conv_b[idx])
            y_ctx, st = _lru_mixer(hn, zero_state, w_in, idx, *lru_args, latent=False)
            y_lat, _ = _lru_mixer(hn, state_lru[:, idx], w_in, idx, *lru_args, latent=True)
            states.append(st)
            x, hn2 = _proj_res_norm(y_ctx, y_lat, w_out, idx, x, norm2_g[l], mods, l)
            x, hn = _ffn(hn2, ffn_w_gate, ffn_w_up, ffn_w_down, idx, x, mods, l, norm1_g[l + 1])
        else:
            q_ctx, new_k, new_v = _qkv_heads(hn, w_qkv, idx, latent=False, k_into=new_k, v_into=new_v)
            q_lat, k_lat, v_lat = _qkv_heads(hn, w_qkv, idx, latent=True)
            o_ctx = _context_attention(q_ctx, new_k, new_v, idx)
            o_lat = _neighbourhood_attention(q_lat, k_lat, v_lat, cache_kt, cache_vt, _window_bias(rpb[idx]), idx)
            x, hn2, top_i, top_p = _proj_res_norm(o_ctx, o_lat, w_o, idx, x, norm2_g[l], mods, l, w_router[idx])
            dest, tile_expert, n_used, tile_rows, zero_tiles = _route(top_i)
            xs = _moe_scatter(hn2, dest, zero_tiles)
            y = _moe_experts(xs, tile_expert, n_used, tile_rows, moe_w_gate, moe_w_up, moe_w_down, idx)
            if l == DEPTH - 1:
                y_prompt, y_sample = _moe_combine(x, y, dest, top_p, mods, l, final_g, True)
            else:
                x, hn = _moe_combine(x, y, dest, top_p, mods, l, norm1_g[l + 1], False)

    return (y_prompt.reshape(BATCH, SEQ, D), y_sample.reshape(DEC_BATCH, DEC_SEQ, D), jnp.stack(states, axis=1),
            jnp.swapaxes(new_k, -1, -2), jnp.swapaxes(new_v, -1, -2))
```
